```python
import jax, jax.numpy as jnp
from jax import lax
import numpy as np

D_MODEL = 2048
BATCH = 4
SEQ = 2048
DEPTH = 2
DEC_BATCH = 128
DEC_SEQ = 1
PAST_LEN = 2048
PAGE_SIZE = 128

POOL_WINDOWS = (2, 4, 8, 16)
POOL_W = D_MODEL // 2
POOL_GROUP_W = POOL_W // len(POOL_WINDOWS)
POOL_STATE = max(POOL_WINDOWS) - 1
M_HEADS = 4
M_W = D_MODEL // 2
M_HEAD_DIM = M_W // M_HEADS
M_CHUNK = 64
N_HEADS = 16
HEAD_DIM = D_MODEL // N_HEADS
N_KV = 4
Q_PER_KV = N_HEADS // N_KV
NSA_W = N_HEADS * HEAD_DIM
KV_W = 2 * N_KV * HEAD_DIM
CMP_BLOCK = 32
SEL_BLOCK = 64
SEL_TOPK = 16
WINDOW = 512
ATT_Q_BLOCK = 128
SEL_Q_BLOCK = 64
ROPE_THETA = 10000.0
ATT_SCALE = HEAD_DIM ** -0.5
IN0 = 2 * POOL_W + 5 * M_W + 2 * M_HEADS
IN1 = 2 * NSA_W + 3 * KV_W + 3 * N_HEADS
EPS = 1e-6

kernel_name = 'hybrid_pool_mlstm_nsa_step'


def rmsnorm(x, g):
    xf = x.astype(jnp.float32)
    r = lax.rsqrt(jnp.mean(xf * xf, axis=-1, keepdims=True) + EPS)
    return (xf * r).astype(x.dtype) * g


def split_cols(a, widths):
    outs, start = [], 0
    for w in widths:
        outs.append(a[..., start:start + w])
        start += w
    return outs


def rope(x, pos):
    half = x.shape[-1] // 2
    inv = ROPE_THETA ** (-jnp.arange(half, dtype=jnp.float32) / half)
    ang = pos.astype(jnp.float32)[:, None] * inv[None, :]
    cos = jnp.cos(ang)[None, :, None, :]
    sin = jnp.sin(ang)[None, :, None, :]
    xf = x.astype(jnp.float32)
    x1, x2 = xf[..., :half], xf[..., half:]
    return jnp.concatenate([x1 * cos - x2 * sin, x2 * cos + x1 * sin], axis=-1).astype(x.dtype)


def masked_softmax(s, mask):
    s = jnp.where(mask, s.astype(jnp.float32), -jnp.inf)
    m = jnp.max(s, axis=-1, keepdims=True)
    m = jnp.where(jnp.isfinite(m), m, 0.0)
    p = jnp.exp(s - m)
    return p / jnp.maximum(jnp.sum(p, axis=-1, keepdims=True), 1e-30)


def pool_mix(buf, n_new, w_pool, pool_scale):
    B, R, _ = buf.shape
    bf = buf.astype(jnp.float32)
    csz = jnp.concatenate([jnp.zeros((B, 1, POOL_W), jnp.float32), jnp.cumsum(bf, axis=1)], axis=1)
    outs = []
    for g, w in enumerate(POOL_WINDOWS):
        sl = slice(g * POOL_GROUP_W, (g + 1) * POOL_GROUP_W)
        c = csz[:, :, sl]
        hi = c[:, 1:]
        lo = jnp.concatenate([jnp.zeros((B, w - 1, POOL_GROUP_W), jnp.float32), c], axis=1)[:, :R]
        cnt = jnp.minimum(jnp.arange(1, R + 1), w).astype(jnp.float32)[None, :, None]
        pooled = ((hi - lo) / cnt - bf[:, :, sl])[:, R - n_new:]
        outs.append(jnp.einsum('btc,cd->btd', pooled, w_pool[g].astype(jnp.float32)))
    y = jnp.concatenate(outs, axis=-1) * pool_scale.astype(jnp.float32)
    return y.astype(buf.dtype)


def mlstm_chunk(state, inp):
    c, n, m = state
    q, k, v, ig, lf = inp
    L = q.shape[1]
    b = jnp.cumsum(lf, axis=1)
    inter = b + m[:, None, :]
    dmat = b[:, :, None, :] - b[:, None, :, :] + ig[:, None, :, :]
    causal = jnp.tril(jnp.ones((L, L), dtype=bool))[None, :, :, None]
    dmat = jnp.where(causal, dmat, -jnp.inf)
    m_t = jnp.maximum(inter, jnp.max(dmat, axis=2))
    dw = jnp.exp(dmat - m_t[:, :, None, :])
    iw = jnp.exp(inter - m_t)
    qk = jnp.einsum('bthd,bshd->btsh', q, k) * dw
    num = iw[..., None] * jnp.einsum('bthd,bhde->bthe', q, c) + jnp.einsum('btsh,bshe->bthe', qk, v)
    den = iw * jnp.einsum('bthd,bhd->bth', q, n) + jnp.sum(qk, axis=2)
    h = num / jnp.maximum(jnp.abs(den), jnp.exp(-m_t))[..., None]
    m_new = m_t[:, -1]
    ws = dw[:, -1]
    dec = iw[:, -1]
    c_new = dec[..., None, None] * c + jnp.einsum('bsh,bshd,bshe->bhde', ws, k, v)
    n_new = dec[..., None] * n + jnp.einsum('bsh,bshd->bhd', ws, k)
    return (c_new, n_new, m_new), h


def pool_mlstm_layer(x, pool_prev, c0, n0, m0, prompt, norm_g, w_in, b_gate, w_pool, pool_scale, mh_norm_g, w_out):
    f32 = jnp.float32
    B, T, _ = x.shape
    h = rmsnorm(x, norm_g)
    proj = jnp.einsum('btd,de->bte', h, w_in)
    u, z_p, q, k, v, o, z_m, gts = split_cols(proj, (POOL_W, POOL_W, M_W, M_W, M_W, M_W, M_W, 2 * M_HEADS))
    buf = jnp.concatenate([pool_prev.astype(u.dtype), u], axis=1)
    y_pool = pool_mix(buf, T, w_pool, pool_scale) * jax.nn.silu(z_p)
    new_pool = buf[:, -POOL_STATE:]
    shp = (B, T, M_HEADS, M_HEAD_DIM)
    qh = q.reshape(shp).astype(f32)
    kh = k.reshape(shp).astype(f32) * (M_HEAD_DIM ** -0.5)
    vh = v.reshape(shp).astype(f32)
    gts = (gts + b_gate).astype(f32)
    ig = gts[..., :M_HEADS]
    lf = jax.nn.log_sigmoid(gts[..., M_HEADS:])
    if prompt:
        nc = T // M_CHUNK

        def to_chunks(a):
            return jnp.moveaxis(a.reshape((B, nc, M_CHUNK) + a.shape[2:]), 1, 0)

        init = (jnp.zeros((B, M_HEADS, M_HEAD_DIM, M_HEAD_DIM), f32),
                jnp.zeros((B, M_HEADS, M_HEAD_DIM), f32),
                jnp.zeros((B, M_HEADS), f32))
        (c1, n1, m1), hc = lax.scan(mlstm_chunk, init,
                                    (to_chunks(qh), to_chunks(kh), to_chunks(vh), to_chunks(ig), to_chunks(lf)))
        hcell = jnp.moveaxis(hc, 0, 1).reshape(shp)
    else:
        (c1, n1, m1), hcell = mlstm_chunk((c0.astype(f32), n0.astype(f32), m0.astype(f32)), (qh, kh, vh, ig, lf))
    hcell = hcell * jax.nn.sigmoid(o.astype(f32)).reshape(shp)
    hcell = hcell * lax.rsqrt(jnp.mean(hcell * hcell, axis=-1, keepdims=True) + EPS)
    y_m = (hcell.reshape(B, T, M_W) * mh_norm_g.astype(f32)).astype(x.dtype) * jax.nn.silu(z_m)
    y = jnp.einsum('bte,ed->btd', jnp.concatenate([y_pool, y_m], axis=-1), w_out)
    return x + y.astype(x.dtype), new_pool, c1, n1, m1


def cmp_attend(q, q_pos, blocks):
    B, Tq = q.shape[:2]
    nc = blocks.shape[1]
    qg = q.reshape(B, Tq, N_KV, Q_PER_KV, HEAD_DIM)
    s = jnp.einsum('bqgrd,bngd->bgrqn', qg, blocks[:, :, 0].astype(q.dtype)) * ATT_SCALE
    end = (jnp.arange(nc) + 1) * CMP_BLOCK - 1
    p = masked_softmax(s, end[None, :] <= q_pos[:, None])
    o = jnp.einsum('bgrqn,bngd->bqgrd', p.astype(q.dtype), blocks[:, :, 1].astype(q.dtype))
    return o.reshape(B, Tq, N_HEADS, HEAD_DIM), p


def sel_attend(q, q_pos, idx, rows):
    B, Tq = q.shape[:2]
    K = idx.shape[-1]
    qg = q.reshape(B, Tq, N_KV, Q_PER_KV, HEAD_DIM)
    pos = idx[..., None] * SEL_BLOCK + jnp.arange(SEL_BLOCK)
    mask = (pos <= q_pos[None, None, :, None, None]).reshape(B, N_KV, Tq, 1, K * SEL_BLOCK)
    s = jnp.einsum('bqgrd,bgqkld->bgqrkl', qg, rows[..., 0, :]) * ATT_SCALE
    p = masked_softmax(s.reshape(B, N_KV, Tq, Q_PER_KV, K * SEL_BLOCK), mask)
    p = p.reshape(B, N_KV, Tq, Q_PER_KV, K, SEL_BLOCK).astype(q.dtype)
    o = jnp.einsum('bgqrkl,bgqkld->bqgrd', p, rows[..., 1, :])
    return o.reshape(B, Tq, N_HEADS, HEAD_DIM)


def win_attend(q, q_pos, kv, k_pos):
    B, Tq = q.shape[:2]
    qg = q.reshape(B, Tq, N_KV, Q_PER_KV, HEAD_DIM)
    s = jnp.einsum('bqgrd,bsgd->bgrqs', qg, kv[:, :, 0]) * ATT_SCALE
    dpos = q_pos[:, None] - k_pos[None, :]
    mask = (dpos >= 0) & (dpos <= WINDOW) & (k_pos[None, :] >= 0)
    p = masked_softmax(s, mask).astype(q.dtype)
    o = jnp.einsum('bgrqs,bsgd->bqgrd', p, kv[:, :, 1])
    return o.reshape(B, Tq, N_HEADS, HEAD_DIM)


def nsa_layer(x, prompt, win_buf, cache_kv_cmp, cache_kv_sel, cache_kv_win, page_table, norm_g, w_in, b_gate, w_out):
    f32 = jnp.float32
    B, T, _ = x.shape
    h = rmsnorm(x, norm_g)
    proj = jnp.einsum('btd,de->bte', h, w_in)
    q, kv_c, kv_s, kv_w, gts, z = split_cols(proj, (NSA_W, KV_W, KV_W, KV_W, 3 * N_HEADS, NSA_W))
    q = q.reshape(B, T, N_HEADS, HEAD_DIM)
    kv_shape = (B, T, 2, N_KV, HEAD_DIM)
    kv_c = kv_c.reshape(kv_shape)
    kv_s = kv_s.reshape(kv_shape)
    kv_w = kv_w.reshape(kv_shape)
    past_len = 0 if prompt else page_table.shape[1] * PAGE_SIZE
    q_pos = past_len + jnp.arange(T)
    q_rot = rope(q, q_pos)
    kv_s = jnp.stack([rope(kv_s[:, :, 0], q_pos), kv_s[:, :, 1]], axis=2)
    kv_w = jnp.stack([rope(kv_w[:, :, 0], q_pos), kv_w[:, :, 1]], axis=2)
    s_total = past_len + T

    nb_new = T // CMP_BLOCK
    new_blocks = kv_c[:, :nb_new * CMP_BLOCK].astype(f32).reshape(B, nb_new, CMP_BLOCK, 2, N_KV, HEAD_DIM).mean(2)
    if prompt:
        blocks = new_blocks
    else:
        past_c = cache_kv_cmp[page_table].astype(f32)
        past_blocks = past_c.reshape(B, past_len // CMP_BLOCK, CMP_BLOCK, 2, N_KV, HEAD_DIM).mean(2)
        blocks = jnp.concatenate([past_blocks, new_blocks], axis=1)
    o_c, p_c = cmp_attend(q, q_pos, blocks.astype(q.dtype))

    n_sel = -(-s_total // SEL_BLOCK)
    nc = blocks.shape[1]
    imp = jnp.pad(p_c.sum(axis=2), ((0, 0), (0, 0), (0, 0), (0, 2 * n_sel - nc)))
    imp = imp.reshape(B, N_KV, T, n_sel, 2).sum(-1)
    blk = jnp.arange(n_sel)
    imp = jnp.where(blk[None, :] == (q_pos // SEL_BLOCK)[:, None], jnp.inf, imp)
    imp = jnp.where(blk[None, :] * SEL_BLOCK > q_pos[:, None], -jnp.inf, imp)
    k_top = min(SEL_TOPK, n_sel)
    _, idx = lax.top_k(imp, k_top)

    if prompt:
        kvs_blocks = jnp.moveaxis(kv_s.reshape(B, n_sel, SEL_BLOCK, 2, N_KV, HEAD_DIM), 4, 1)
        bi = jnp.arange(B)[:, None, None, None]
        gi = jnp.arange(N_KV)[None, :, None, None]
        nsb = T // SEL_Q_BLOCK
        q_blk = jnp.moveaxis(q_rot.reshape(B, nsb, SEL_Q_BLOCK, N_HEADS, HEAD_DIM), 1, 0)
        p_blk = q_pos.reshape(nsb, SEL_Q_BLOCK)
        i_blk = jnp.moveaxis(idx.reshape(B, N_KV, nsb, SEL_Q_BLOCK, k_top), 2, 0)
        o_s = lax.map(lambda a: sel_attend(a[0], a[1], a[2], kvs_blocks[bi, gi, a[2]]), (q_blk, p_blk, i_blk))
        o_s = jnp.moveaxis(o_s, 0, 1).reshape(B, T, N_HEADS, HEAD_DIM)

        nqb = T // ATT_Q_BLOCK
        kv_pad = jnp.concatenate([jnp.zeros((B, WINDOW, 2, N_KV, HEAD_DIM), kv_w.dtype), kv_w], axis=1)
        qw_blk = jnp.moveaxis(q_rot.reshape(B, nqb, ATT_Q_BLOCK, N_HEADS, HEAD_DIM), 1, 0)

        def win_block(a):
            qb, i = a
            start = i * ATT_Q_BLOCK
            kvb = lax.dynamic_slice_in_dim(kv_pad, start, WINDOW + ATT_Q_BLOCK, axis=1)
            k_pos = start - WINDOW + jnp.arange(WINDOW + ATT_Q_BLOCK)
            return win_attend(qb, start + jnp.arange(ATT_Q_BLOCK), kvb, k_pos)

        o_w = lax.map(win_block, (qw_blk, jnp.arange(nqb)))
        o_w = jnp.moveaxis(o_w, 0, 1).reshape(B, T, N_HEADS, HEAD_DIM)
        new_win = kv_pad[:, -win_buf:]
    else:
        pos = idx[..., None] * SEL_BLOCK + jnp.arange(SEL_BLOCK)
        bix = jnp.arange(B)[:, None, None, None, None]
        gix = jnp.arange(N_KV)[None, :, None, None, None]
        pp = jnp.minimum(pos, past_len - 1)
        phys = page_table[bix, pp // PAGE_SIZE]
        past_rows = cache_kv_sel[phys, pp % PAGE_SIZE, :, gix, :]
        new_rows = kv_s[bix, jnp.clip(pos - past_len, 0, T - 1), :, gix, :]
        rows = jnp.where((pos < past_len)[..., None, None], past_rows.astype(kv_s.dtype), new_rows)
        o_s = sel_attend(q_rot, q_pos, idx, rows)

        kvw_all = jnp.concatenate([cache_kv_win.astype(kv_w.dtype), kv_w], axis=1)
        k_pos = past_len - win_buf + jnp.arange(win_buf + T)
        o_w = win_attend(q_rot, q_pos, kvw_all, k_pos)
        new_win = kvw_all[:, -win_buf:]

    gate = jax.nn.sigmoid((gts + b_gate).astype(f32)).reshape(B, T, N_HEADS, 3)
    o = gate[..., 0:1] * o_c + gate[..., 1:2] * o_s + gate[..., 2:3] * o_w
    y = jnp.einsum('bte,ed->btd', o.reshape(B, T, NSA_W).astype(x.dtype) * jax.nn.silu(z), w_out)
    return x + y.astype(x.dtype), kv_c, kv_s, new_win


def setup_inputs(seed: int = 0) -> dict:
    key = jax.random.key(seed)
    ks = jax.random.split(key, 24)
    f32 = jnp.float32
    n_pages = PAST_LEN // PAGE_SIZE
    n_used = DEC_BATCH * n_pages
    n_pool = n_used + (n_used + 3) // 4
    page_table = jax.random.permutation(ks[0], n_pool)[:n_used].reshape(DEC_BATCH, n_pages).astype(jnp.int32)
    win_buf = min(WINDOW, PAST_LEN)

    def nrm(k, shape, scale):
        return jax.random.normal(k, shape, f32) * scale

    b_gate0 = jnp.concatenate([nrm(ks[13], (M_HEADS,), 0.1), 3.0 + nrm(ks[14], (M_HEADS,), 0.5)])
    return {
        'x_prompt': nrm(ks[1], (BATCH, SEQ, D_MODEL), 1.0),
        'x_sample': nrm(ks[2], (DEC_BATCH, DEC_SEQ, D_MODEL), 1.0),
        'state_pool': nrm(ks[3], (DEC_BATCH, POOL_STATE, POOL_W), 1.0),
        'state_mlstm_c': nrm(ks[4], (DEC_BATCH, M_HEADS, M_HEAD_DIM, M_HEAD_DIM), 0.05),
        'state_mlstm_n': nrm(ks[5], (DEC_BATCH, M_HEADS, M_HEAD_DIM), 0.5),
        'state_mlstm_m': nrm(ks[6], (DEC_BATCH, M_HEADS), 1.0),
        'cache_kv_cmp': nrm(ks[7], (n_pool, PAGE_SIZE, 2, N_KV, HEAD_DIM), 1.0),
        'cache_kv_sel': nrm(ks[8], (n_pool, PAGE_SIZE, 2, N_KV, HEAD_DIM), 1.0),
        'cache_kv_win': nrm(ks[9], (DEC_BATCH, win_buf, 2, N_KV, HEAD_DIM), 1.0),
        'page_table': page_table,
        'norm0_g': 1.0 + nrm(ks[10], (D_MODEL,), 0.1),
        'w_in0': nrm(ks[11], (D_MODEL, IN0), D_MODEL ** -0.5),
        'b_gate0': b_gate0,
        'w_pool': nrm(ks[12], (len(POOL_WINDOWS), POOL_GROUP_W, POOL_GROUP_W), POOL_GROUP_W ** -0.5),
        'pool_scale': 1.0 + nrm(ks[15], (POOL_W,), 0.1),
        'mh_norm_g': 1.0 + nrm(ks[16], (M_W,), 0.1),
        'w_out0': nrm(ks[17], (POOL_W + M_W, D_MODEL), (POOL_W + M_W) ** -0.5),
        'norm1_g': 1.0 + nrm(ks[18], (D_MODEL,), 0.1),
        'w_in1': nrm(ks[19], (D_MODEL, IN1), D_MODEL ** -0.5),
        'b_gate1': nrm(ks[20], (3 * N_HEADS,), 0.1),
        'w_out1': nrm(ks[21], (NSA_W, D_MODEL), NSA_W ** -0.5),
        'final_g': 1.0 + nrm(ks[22], (D_MODEL,), 0.1),
    }


def reference(x_prompt, x_sample, state_pool, state_mlstm_c, state_mlstm_n, state_mlstm_m,
              cache_kv_cmp, cache_kv_sel, cache_kv_win, page_table,
              norm0_g, w_in0, b_gate0, w_pool, pool_scale, mh_norm_g, w_out0,
              norm1_g, w_in1, b_gate1, w_out1, final_g):
    xp, xs = x_prompt, x_sample
    win_buf = cache_kv_win.shape[1]
    empty_pool = jnp.zeros((xp.shape[0], 0, POOL_W), xp.dtype)
    for layer in range(DEPTH):
        if layer % 2 == 0:
            xp, pool_p, c_p, n_p, m_p = pool_mlstm_layer(
                xp, empty_pool, None, None, None, True,
                norm0_g, w_in0, b_gate0, w_pool, pool_scale, mh_norm_g, w_out0)
            xs, pool_s, c_s, n_s, m_s = pool_mlstm_layer(
                xs, state_pool, state_mlstm_c, state_mlstm_n, state_mlstm_m, False,
                norm0_g, w_in0, b_gate0, w_pool, pool_scale, mh_norm_g, w_out0)
        else:
            xp, kvc_p, kvs_p, kvw_p = nsa_layer(
                xp, True, win_buf, None, None, None, None, norm1_g, w_in1, b_gate1, w_out1)
            xs, kvc_s, kvs_s, kvw_s = nsa_layer(
                xs, False, win_buf, cache_kv_cmp, cache_kv_sel, cache_kv_win, page_table,
                norm1_g, w_in1, b_gate1, w_out1)
    y_prompt = rmsnorm(xp, final_g)
    y_sample = rmsnorm(xs, final_g)
    return (y_prompt, y_sample, pool_p, pool_s, c_p, c_s, n_p, n_s, m_p, m_s,
            kvc_p, kvc_s, kvs_p, kvs_s, kvw_p, kvw_s)
```

```python
import functools

import jax
import jax.numpy as jnp
from jax import lax
from jax.experimental import pallas as pl
from jax.experimental.pallas import tpu as pltpu

f32 = jnp.float32
bf16 = jnp.bfloat16

D_MODEL = 2048
POOL_WINDOWS = (2, 4, 8, 16)
POOL_W = 1024
POOL_GROUP_W = 256
POOL_STATE = 15
M_HEADS = 4
M_W = 1024
M_HEAD_DIM = 256
N_HEADS = 16
HEAD_DIM = 128
N_KV = 4
Q_PER_KV = 4
NSA_W = 2048
KV_W = 1024
CMP_BLOCK = 32
SEL_BLOCK = 64
SEL_TOPK = 16
WINDOW = 512
PAGE_SIZE = 128
ROPE_THETA = 10000.0
ATT_SCALE = HEAD_DIM ** -0.5
EPS = 1e-6
MAIN_W = 7168
LANES = 128
NEG = -1e30
VMEM_LIMIT = 48 * 1024 * 1024

_NT = (((1,), (1,)), ((), ()))


def _cparams(sem):
    return pltpu.CompilerParams(dimension_semantics=sem, vmem_limit_bytes=VMEM_LIMIT)


def _sigmoid(x):
    return 1.0 / (1.0 + jnp.exp(-x))


def _silu(x):
    return x * _sigmoid(x)


def _log_sigmoid(x):
    return jnp.minimum(x, 0.0) - jnp.log1p(jnp.exp(-jnp.abs(x)))


def _mm(a, b):
    return jnp.dot(a, b, preferred_element_type=f32)


def _mm_nt(a, b):
    return lax.dot_general(a, b, _NT, preferred_element_type=f32)


def _norm_proj_kernel(x_ref, g_ref, w_ref, wg_ref, o_ref, og_ref, h_ref):
    @pl.when(pl.program_id(1) == 0)
    def _():
        x = x_ref[...]
        r = lax.rsqrt(jnp.mean(x * x, axis=-1, keepdims=True) + EPS)
        h = ((x * r) * g_ref[...]).astype(bf16)
        h_ref[...] = h
        og_ref[...] = _mm(h, wg_ref[...])

    o_ref[...] = _mm(h_ref[...], w_ref[...])


def norm_proj(x, g, w, wg, tm, tn):
    m, d = x.shape
    n, ng = w.shape[1], wg.shape[1]
    return pl.pallas_call(
        _norm_proj_kernel,
        grid=(m // tm, n // tn),
        in_specs=[pl.BlockSpec((tm, d), lambda i, j: (i, 0)),
                  pl.BlockSpec((1, d), lambda i, j: (0, 0)),
                  pl.BlockSpec((d, tn), lambda i, j: (0, j)),
                  pl.BlockSpec((d, ng), lambda i, j: (0, 0))],
        out_specs=[pl.BlockSpec((tm, tn), lambda i, j: (i, j)),
                   pl.BlockSpec((tm, ng), lambda i, j: (i, 0))],
        out_shape=[jax.ShapeDtypeStruct((m, n), f32), jax.ShapeDtypeStruct((m, ng), f32)],
        scratch_shapes=[pltpu.VMEM((tm, d), bf16)],
        compiler_params=_cparams(("parallel", "arbitrary")),
        name="norm_proj",
    )(x, g.reshape(1, d), w, wg)


def _out_proj_kernel(*refs, n_parts, final_norm):
    a_refs = refs[:n_parts]
    w_refs = refs[n_parts:2 * n_parts]
    x_ref = refs[2 * n_parts]
    o_ref = refs[-1]
    acc = x_ref[...]
    for a_ref, w_ref in zip(a_refs, w_refs):
        acc = acc + _mm(a_ref[...], w_ref[...])
    if final_norm:
        fg_ref = refs[2 * n_parts + 1]
        r = lax.rsqrt(jnp.mean(acc * acc, axis=-1, keepdims=True) + EPS)
        acc = (acc * r) * fg_ref[...]
    o_ref[...] = acc


def out_proj(parts, weights, x, final_g, tm):
    m, d = x.shape
    n_parts = len(parts)
    final_norm = final_g is not None
    in_specs = [pl.BlockSpec((tm, a.shape[1]), lambda i: (i, 0)) for a in parts]
    in_specs += [pl.BlockSpec(w.shape, lambda i: (0, 0)) for w in weights]
    in_specs += [pl.BlockSpec((tm, d), lambda i: (i, 0))]
    args = list(parts) + list(weights) + [x]
    if final_norm:
        in_specs += [pl.BlockSpec((1, d), lambda i: (0, 0))]
        args += [final_g.reshape(1, d)]
    return pl.pallas_call(
        functools.partial(_out_proj_kernel, n_parts=n_parts, final_norm=final_norm),
        grid=(m // tm,),
        in_specs=in_specs,
        out_specs=pl.BlockSpec((tm, d), lambda i: (i, 0)),
        out_shape=jax.ShapeDtypeStruct((m, d), f32),
        compiler_params=_cparams(("parallel",)),
        name="out_proj",
    )(*args)


def _pool_kernel(u_ref, z_ref, w_ref, sc_ref, o_ref):
    g = pl.program_id(1)
    x = u_ref[...]
    row = lax.broadcasted_iota(jnp.int32, x.shape, 0)

    def back(a, s):
        return jnp.where(row >= s, pltpu.roll(a, s, axis=0), 0.0)

    s2 = x + back(x, 1)
    s4 = s2 + back(s2, 2)
    s8 = s4 + back(s4, 4)
    s16 = s8 + back(s8, 8)
    win = jnp.where(g == 0, s2, jnp.where(g == 1, s4, jnp.where(g == 2, s8, s16)))
    wlen = lax.shift_left(jnp.int32(2), g)
    cnt = jnp.minimum(row + 1, wlen).astype(f32)
    pooled = win / cnt - x
    y = _mm(pooled.astype(bf16), w_ref[0]) * sc_ref[...]
    o_ref[...] = (y * _silu(z_ref[...])).astype(bf16)


def pool_prompt(proj, w_pool, pool_scale, nb, t):
    ng = len(POOL_WINDOWS)
    return pl.pallas_call(
        _pool_kernel,
        grid=(nb, ng),
        in_specs=[pl.BlockSpec((t, POOL_GROUP_W), lambda b, g: (b, g)),
                  pl.BlockSpec((t, POOL_GROUP_W), lambda b, g: (b, ng + g)),
                  pl.BlockSpec((1, POOL_GROUP_W, POOL_GROUP_W), lambda b, g: (g, 0, 0)),
                  pl.BlockSpec((1, POOL_GROUP_W), lambda b, g: (0, g))],
        out_specs=pl.BlockSpec((t, POOL_GROUP_W), lambda b, g: (b, g)),
        out_shape=jax.ShapeDtypeStruct((nb * t, POOL_W), bf16),
        compiler_params=_cparams(("parallel", "arbitrary")),
        name="pool_prompt",
    )(proj, proj, w_pool, pool_scale.reshape(1, POOL_W))


def _pool_step_kernel(st_ref, u_ref, z_ref, w_ref, sc_ref, o_ref):
    u = u_ref[...]
    for g, wlen in enumerate(POOL_WINDOWS):
        lo = g * POOL_GROUP_W
        ug = u[:, lo:lo + POOL_GROUP_W]
        acc = ug
        for r in range(POOL_STATE + 1 - wlen, POOL_STATE):
            acc = acc + st_ref[:, r * POOL_W + lo:r * POOL_W + lo + POOL_GROUP_W]
        pooled = acc / float(wlen) - ug
        y = _mm(pooled.astype(bf16), w_ref[g]) * sc_ref[:, lo:lo + POOL_GROUP_W]
        o_ref[:, lo:lo + POOL_GROUP_W] = (y * _silu(z_ref[:, lo:lo + POOL_GROUP_W])).astype(bf16)


def pool_step(state_flat, proj, w_pool, pool_scale):
    nb = proj.shape[0]
    return pl.pallas_call(
        _pool_step_kernel,
        grid=(1,),
        in_specs=[pl.BlockSpec(state_flat.shape, lambda i: (0, 0)),
                  pl.BlockSpec((nb, POOL_W), lambda i: (0, 0)),
                  pl.BlockSpec((nb, POOL_W), lambda i: (0, 1)),
                  pl.BlockSpec(w_pool.shape, lambda i: (0, 0, 0)),
                  pl.BlockSpec((1, POOL_W), lambda i: (0, 0))],
        out_specs=pl.BlockSpec((nb, POOL_W), lambda i: (0, 0)),
        out_shape=jax.ShapeDtypeStruct((nb, POOL_W), bf16),
        compiler_params=_cparams(("arbitrary",)),
        name="pool_step",
    )(state_flat, proj, proj, w_pool, pool_scale.reshape(1, POOL_W))


def _head_out(hc, o, z, g):
    hc = hc * _sigmoid(o)
    hc = hc * lax.rsqrt(jnp.mean(hc * hc, axis=-1, keepdims=True) + EPS)
    return ((hc * g) * _silu(z)).astype(bf16)


def _mlstm_kernel(q_ref, k_ref, v_ref, o_ref, z_ref, gi_ref, gf_ref, bi_ref, bf_ref, mhg_ref,
                  y_ref, c_ref, n_ref, m_ref):
    @pl.when(pl.program_id(1) == 0)
    def _():
        c_ref[...] = jnp.zeros_like(c_ref)
        n_ref[...] = jnp.zeros_like(n_ref)
        m_ref[...] = jnp.zeros_like(m_ref)

    ln = q_ref.shape[0]
    gi = gi_ref[...] + bi_ref[...]
    lf = _log_sigmoid(gf_ref[...] + bf_ref[...])
    row = lax.broadcasted_iota(jnp.int32, lf.shape, 0)
    b = lf
    s = 1
    while s < ln:
        b = b + jnp.where(row >= s, pltpu.roll(b, s, axis=0), 0.0)
        s *= 2
    r_t = (gi - b).T
    tt = lax.broadcasted_iota(jnp.int32, (ln, ln), 0)
    ss = lax.broadcasted_iota(jnp.int32, (ln, ln), 1)
    causal = ss <= tt
    lane = lax.broadcasted_iota(jnp.int32, (1, LANES), 1)
    m_vec = m_ref[0]
    for h in range(M_HEADS):
        hs = slice(h * M_HEAD_DIM, (h + 1) * M_HEAD_DIM)
        b_col = b[:, h:h + 1]
        ig_col = gi[:, h:h + 1]
        m_prev = m_vec[:, h:h + 1]
        inter = b_col + m_prev
        dmat = jnp.where(causal, b_col + r_t[h:h + 1, :], -jnp.inf)
        m_t = jnp.maximum(inter, jnp.max(dmat, axis=1, keepdims=True))
        dw = jnp.exp(dmat - m_t)
        iw = jnp.exp(inter - m_t)
        q = q_ref[:, hs]
        k = k_ref[:, hs] * (M_HEAD_DIM ** -0.5)
        v = v_ref[:, hs]
        qb, kb, vb = q.astype(bf16), k.astype(bf16), v.astype(bf16)
        c = c_ref[0, h]
        n = n_ref[0, h:h + 1, :]
        qk = _mm_nt(qb, kb) * dw
        num = iw * _mm(qb, c.astype(bf16)) + _mm(qk.astype(bf16), vb)
        den = iw * jnp.sum(q * n, axis=1, keepdims=True) + jnp.sum(qk, axis=1, keepdims=True)
        hc = num / jnp.maximum(jnp.abs(den), jnp.exp(-m_t))
        y_ref[:, hs] = _head_out(hc, o_ref[:, hs], z_ref[:, hs], mhg_ref[:, hs])
        m_last = m_t[ln - 1:ln, :]
        b_last = b_col[ln - 1:ln, :]
        ws = jnp.exp(b_last - b_col + ig_col - m_last)
        dec = jnp.exp(b_last + m_prev - m_last)
        kw = ws * k
        c_ref[0, h] = dec * c + _mm(kw.T.astype(bf16), vb)
        n_ref[0, h:h + 1, :] = dec * n + jnp.sum(kw, axis=0, keepdims=True)
        m_vec = jnp.where(lane == h, m_last, m_vec)
    m_ref[0] = m_vec


def mlstm_prompt(proj, gates, bias_i, bias_f, mh_norm_g, nb, t, ln):
    nc = t // ln
    col = lambda cb: pl.BlockSpec((ln, M_W), lambda b, c: (b * nc + c, cb))
    gcol = lambda cb: pl.BlockSpec((ln, LANES), lambda b, c: (b * nc + c, cb))
    vec = lambda w: pl.BlockSpec((1, w), lambda b, c: (0, 0))
    return pl.pallas_call(
        _mlstm_kernel,
        grid=(nb, nc),
        in_specs=[col(2), col(3), col(4), col(5), col(6), gcol(0), gcol(1), vec(LANES), vec(LANES), vec(M_W)],
        out_specs=[pl.BlockSpec((ln, M_W), lambda b, c: (b * nc + c, 0)),
                   pl.BlockSpec((1, M_HEADS, M_HEAD_DIM, M_HEAD_DIM), lambda b, c: (b, 0, 0, 0)),
                   pl.BlockSpec((1, M_HEADS, M_HEAD_DIM), lambda b, c: (b, 0, 0)),
                   pl.BlockSpec((1, 1, LANES), lambda b, c: (b, 0, 0))],
        out_shape=[jax.ShapeDtypeStruct((nb * t, M_W), bf16),
                   jax.ShapeDtypeStruct((nb, M_HEADS, M_HEAD_DIM, M_HEAD_DIM), f32),
                   jax.ShapeDtypeStruct((nb, M_HEADS, M_HEAD_DIM), f32),
                   jax.ShapeDtypeStruct((nb, 1, LANES), f32)],
        compiler_params=_cparams(("parallel", "arbitrary")),
        name="mlstm_prompt",
    )(proj, proj, proj, proj, proj, gates, gates, bias_i, bias_f, mh_norm_g.reshape(1, M_W))


def _mlstm_step_kernel(q_ref, k_ref, v_ref, o_ref, z_ref, gi_ref, gf_ref, bi_ref, bf_ref, mhg_ref,
                       c_ref, n_ref, m_ref, y_ref, co_ref, no_ref, mo_ref):
    nb = q_ref.shape[0]
    gi = gi_ref[...] + bi_ref[...]
    lf = _log_sigmoid(gf_ref[...] + bf_ref[...])
    inter = lf + m_ref[...]
    m_t = jnp.maximum(inter, gi)
    dw_all = jnp.exp(gi - m_t)
    iw_all = jnp.exp(inter - m_t)
    em_all = jnp.exp(-m_t)
    mo_ref[...] = m_t
    d0 = lax.broadcasted_iota(jnp.int32, (M_HEAD_DIM, M_HEAD_DIM), 0)
    d1 = lax.broadcasted_iota(jnp.int32, (M_HEAD_DIM, M_HEAD_DIM), 1)
    eye = d0 == d1
    for j in range(nb):
        for h in range(M_HEADS):
            hs = slice(h * M_HEAD_DIM, (h + 1) * M_HEAD_DIM)
            dw = dw_all[j:j + 1, h:h + 1]
            iw = iw_all[j:j + 1, h:h + 1]
            em = em_all[j:j + 1, h:h + 1]
            q = q_ref[j:j + 1, hs]
            k = k_ref[j:j + 1, hs] * (M_HEAD_DIM ** -0.5)
            v = v_ref[j:j + 1, hs]
            c = c_ref[j, h]
            n = n_ref[j, h:h + 1, :]
            qc = _mm(jnp.broadcast_to(q, (8, M_HEAD_DIM)).astype(bf16), c.astype(bf16))[0:1, :]
            qk = jnp.sum(q * k, axis=1, keepdims=True) * dw
            num = iw * qc + qk * v
            den = iw * jnp.sum(q * n, axis=1, keepdims=True) + qk
            hc = num / jnp.maximum(jnp.abs(den), em)
            y_ref[j:j + 1, hs] = _head_out(hc, o_ref[j:j + 1, hs], z_ref[j:j + 1, hs], mhg_ref[:, hs])
            kdiag = jnp.where(eye, jnp.broadcast_to(k, (M_HEAD_DIM, M_HEAD_DIM)), 0.0).astype(bf16)
            vrep = jnp.broadcast_to(v, (M_HEAD_DIM, M_HEAD_DIM)).astype(bf16)
            co_ref[j, h] = iw * c + dw * _mm(kdiag, vrep)
            no_ref[j, h:h + 1, :] = iw * n + dw * k


def mlstm_step(proj, gates, bias_i, bias_f, mh_norm_g, c0, n0, m0_pad, bb):
    nb = proj.shape[0]
    col = lambda cb: pl.BlockSpec((bb, M_W), lambda i: (i, cb))
    gcol = lambda cb: pl.BlockSpec((bb, LANES), lambda i: (i, cb))
    vec = lambda w: pl.BlockSpec((1, w), lambda i: (0, 0))
    cspec = pl.BlockSpec((bb, M_HEADS, M_HEAD_DIM, M_HEAD_DIM), lambda i: (i, 0, 0, 0))
    nspec = pl.BlockSpec((bb, M_HEADS, M_HEAD_DIM), lambda i: (i, 0, 0))
    return pl.pallas_call(
        _mlstm_step_kernel,
        grid=(nb // bb,),
        in_specs=[col(2), col(3), col(4), col(5), col(6), gcol(0), gcol(1), vec(LANES), vec(LANES), vec(M_W),
                  cspec, nspec, gcol(0)],
        out_specs=[pl.BlockSpec((bb, M_W), lambda i: (i, 0)), cspec, nspec, gcol(0)],
        out_shape=[jax.ShapeDtypeStruct((nb, M_W), bf16),
                   jax.ShapeDtypeStruct(c0.shape, f32),
                   jax.ShapeDtypeStruct(n0.shape, f32),
                   jax.ShapeDtypeStruct((nb, LANES), f32)],
        compiler_params=_cparams(("parallel",)),
        name="mlstm_step",
    )(proj, proj, proj, proj, proj, gates, gates, bias_i, bias_f, mh_norm_g.reshape(1, M_W), c0, n0, m0_pad)


def _rope(x, cos, sin_signed):
    return x * cos + pltpu.roll(x, HEAD_DIM // 2, axis=1) * sin_signed


def _nsa_prep_kernel(*refs, with_blocks):
    q_ref, kvc_ref, kvs_ref, kvw_ref, cos_ref, sin_ref = refs[:6]
    qr_ref, kvs_o, kvw_o = refs[6:9]
    cos, sin = cos_ref[...], sin_ref[...]
    for h in range(N_HEADS):
        hs = slice(h * HEAD_DIM, (h + 1) * HEAD_DIM)
        qr_ref[:, hs] = _rope(q_ref[:, hs], cos, sin).astype(bf16)
    for src, dst in ((kvs_ref, kvs_o), (kvw_ref, kvw_o)):
        for g in range(N_KV):
            gs = slice(g * HEAD_DIM, (g + 1) * HEAD_DIM)
            dst[:, gs] = _rope(src[:, gs], cos, sin)
        dst[:, KV_W // 2:] = src[:, KV_W // 2:]
    if with_blocks:
        blk_ref = refs[9]
        x = kvc_ref[...]
        nblk = x.shape[0] // CMP_BLOCK
        blk_ref[...] = jnp.sum(x.reshape(nblk, CMP_BLOCK, KV_W), axis=1) * (1.0 / CMP_BLOCK)


def nsa_prep(proj, cos, sin, tq, n_pos_blocks, with_blocks):
    m = proj.shape[0]
    row = lambda w, cb: pl.BlockSpec((tq, w), lambda i: (i, cb))
    tab = pl.BlockSpec((tq, HEAD_DIM), lambda i: (i % n_pos_blocks, 0))
    out_specs = [row(NSA_W, 0), row(KV_W, 0), row(KV_W, 0)]
    out_shape = [jax.ShapeDtypeStruct((m, NSA_W), bf16), jax.ShapeDtypeStruct((m, KV_W), f32),
                 jax.ShapeDtypeStruct((m, KV_W), f32)]
    if with_blocks:
        out_specs.append(pl.BlockSpec((tq // CMP_BLOCK, KV_W), lambda i: (i, 0)))
        out_shape.append(jax.ShapeDtypeStruct((m // CMP_BLOCK, KV_W), f32))
    return pl.pallas_call(
        functools.partial(_nsa_prep_kernel, with_blocks=with_blocks),
        grid=(m // tq,),
        in_specs=[row(NSA_W, 0), row(KV_W, 4), row(KV_W, 5), row(KV_W, 6), tab, tab],
        out_specs=out_specs,
        out_shape=out_shape,
        compiler_params=_cparams(("parallel",)),
        name="nsa_prep",
    )(proj, proj, proj, proj, cos, sin)


def _select_blocks(imp, q_pos, n_cand):
    lane = lax.broadcasted_iota(jnp.int32, imp.shape, 1)
    pair = imp + pltpu.roll(imp, LANES - 1, axis=1)
    cur2 = lax.shift_left(lax.shift_right_logical(q_pos, 6), 1)
    valid = ((lane & 1) == 0) & (lane <= cur2)
    v = jnp.where(lane == cur2, jnp.inf, pair)
    v = jnp.where(valid, v, -jnp.inf)
    cnt = jnp.zeros(imp.shape, f32)
    for i in range(n_cand):
        vi = v[:, 2 * i:2 * i + 1]
        before = jnp.where(lane > 2 * i, 1.0, 0.0)
        cnt = cnt + jnp.where(vi > v, 1.0, 0.0) + jnp.where(vi == v, before, 0.0)
    return jnp.where(valid & (cnt < SEL_TOPK), 1.0, 0.0)


def _pad_rows(x, rows):
    return jnp.concatenate([x, jnp.zeros((rows - x.shape[0], x.shape[1]), x.dtype)], axis=0)


def _masked_softmax(s, mask):
    s = jnp.where(mask, s, -jnp.inf)
    m = jnp.max(s, axis=-1, keepdims=True)
    m = jnp.where(m > -jnp.inf, m, 0.0)
    p = jnp.exp(s - m)
    return p / jnp.maximum(jnp.sum(p, axis=-1, keepdims=True), 1e-30)


def _cmp_prompt_kernel(q_ref, blk_ref, oc_ref, sel_ref, *, n_blocks):
    tq = q_ref.shape[0]
    t0 = pl.program_id(1) * tq
    lane = lax.broadcasted_iota(jnp.int32, (tq, LANES), 1)
    q_pos = lax.broadcasted_iota(jnp.int32, (tq, LANES), 0) + t0
    vis = (lane < n_blocks) & ((lane + 1) * CMP_BLOCK - 1 <= q_pos)
    for g in range(N_KV):
        kg = _pad_rows(blk_ref[:, g * HEAD_DIM:(g + 1) * HEAD_DIM], LANES).astype(bf16)
        vg = _pad_rows(blk_ref[:, KV_W // 2 + g * HEAD_DIM:KV_W // 2 + (g + 1) * HEAD_DIM], LANES).astype(bf16)
        imp = jnp.zeros((tq, LANES), f32)
        for r in range(Q_PER_KV):
            hs = slice((g * Q_PER_KV + r) * HEAD_DIM, (g * Q_PER_KV + r + 1) * HEAD_DIM)
            s = _mm_nt(q_ref[:, hs].astype(bf16), kg) * ATT_SCALE
            p = _masked_softmax(s, vis)
            oc_ref[:, hs] = _mm(p.astype(bf16), vg)
            imp = imp + p
        sel_ref[:, g * LANES:(g + 1) * LANES] = _select_blocks(imp, q_pos, n_blocks // 2).astype(bf16)


def cmp_prompt(proj, blocks, nb, t, tq):
    nq = t // tq
    n_blocks = t // CMP_BLOCK
    return pl.pallas_call(
        functools.partial(_cmp_prompt_kernel, n_blocks=n_blocks),
        grid=(nb, nq),
        in_specs=[pl.BlockSpec((tq, NSA_W), lambda b, i: (b * nq + i, 0)),
                  pl.BlockSpec((n_blocks, KV_W), lambda b, i: (b, 0))],
        out_specs=[pl.BlockSpec((tq, NSA_W), lambda b, i: (b * nq + i, 0)),
                   pl.BlockSpec((tq, N_KV * LANES), lambda b, i: (b * nq + i, 0))],
        out_shape=[jax.ShapeDtypeStruct((nb * t, NSA_W), f32),
                   jax.ShapeDtypeStruct((nb * t, N_KV * LANES), bf16)],
        compiler_params=_cparams(("parallel", "parallel")),
        name="cmp_prompt",
    )(proj, blocks)


def _flash_kernel(qi_tab, kj_tab, first_tab, *refs, mode):
    if mode == "sel":
        q_ref, k_ref, v_ref, sel_ref, o_ref, m_sc, l_sc, acc_sc = refs
    else:
        q_ref, k_ref, v_ref, o_ref, m_sc, l_sc, acc_sc = refs
    step = pl.program_id(2)
    qi, kj = qi_tab[step], kj_tab[step]
    tq, tk = q_ref.shape[0], k_ref.shape[0]

    @pl.when(first_tab[step] == 1)
    def _():
        m_sc[...] = jnp.full_like(m_sc, NEG)
        l_sc[...] = jnp.zeros_like(l_sc)
        acc_sc[...] = jnp.zeros_like(acc_sc)

    kb = k_ref[...].astype(bf16)
    vb = v_ref[...].astype(bf16)
    dpos = (lax.broadcasted_iota(jnp.int32, (tq, tk), 0) + qi * tq
            - lax.broadcasted_iota(jnp.int32, (tq, tk), 1) - kj * tk)
    if mode == "sel":
        er = lax.broadcasted_iota(jnp.int32, (LANES, tk), 0)
        ec = lax.broadcasted_iota(jnp.int32, (LANES, tk), 1) + kj * tk
        hit = ((er & 1) == 0) & (lax.shift_right_logical(er, 1) == lax.shift_right_logical(ec, 6))
        expand = jnp.where(hit, 1.0, 0.0).astype(bf16)
        mask = (dpos >= 0) & (_mm(sel_ref[...], expand) > 0.5)
    else:
        mask = (dpos >= 0) & (dpos <= WINDOW)
    for r in range(Q_PER_KV):
        hs = slice(r * HEAD_DIM, (r + 1) * HEAD_DIM)
        s = jnp.where(mask, _mm_nt(q_ref[:, hs], kb) * ATT_SCALE, NEG)
        m_old = m_sc[r]
        m_new = jnp.maximum(m_old, jnp.max(s, axis=1, keepdims=True))
        alpha = jnp.exp(m_old - m_new)
        p = jnp.exp(s - m_new)
        l_sc[r] = alpha * l_sc[r] + jnp.sum(p, axis=1, keepdims=True)
        acc_sc[r] = alpha * acc_sc[r] + _mm(p.astype(bf16), vb)
        m_sc[r] = m_new

    @pl.when(kj == qi)
    def _():
        for r in range(Q_PER_KV):
            o_ref[:, r * HEAD_DIM:(r + 1) * HEAD_DIM] = acc_sc[r] / l_sc[r]


def _tile_pairs(nq, reach):
    qi, kj, first = [], [], []
    for i in range(nq):
        lo = max(0, i - reach)
        for j in range(lo, i + 1):
            qi.append(i)
            kj.append(j)
            first.append(1 if j == lo else 0)
    as_i32 = lambda a: jnp.asarray(a, jnp.int32)
    return as_i32(qi), as_i32(kj), as_i32(first)


def flash_prompt(q_rot, kv, sel, nb, t, tile, mode):
    nq = t // tile
    reach = nq if mode == "sel" else -(-WINDOW // tile)
    qi_tab, kj_tab, first_tab = _tile_pairs(nq, reach)
    qw = Q_PER_KV * HEAD_DIM
    in_specs = [pl.BlockSpec((tile, qw), lambda b, g, s, qi, kj, fi: (b * nq + qi[s], g)),
                pl.BlockSpec((tile, HEAD_DIM), lambda b, g, s, qi, kj, fi: (b * nq + kj[s], g)),
                pl.BlockSpec((tile, HEAD_DIM), lambda b, g, s, qi, kj, fi: (b * nq + kj[s], N_KV + g))]
    args = [q_rot, kv, kv]
    if mode == "sel":
        in_specs.append(pl.BlockSpec((tile, LANES), lambda b, g, s, qi, kj, fi: (b * nq + qi[s], g)))
        args.append(sel)
    return pl.pallas_call(
        functools.partial(_flash_kernel, mode=mode),
        grid_spec=pltpu.PrefetchScalarGridSpec(
            num_scalar_prefetch=3,
            grid=(nb, N_KV, int(qi_tab.shape[0])),
            in_specs=in_specs,
            out_specs=pl.BlockSpec((tile, qw), lambda b, g, s, qi, kj, fi: (b * nq + qi[s], g)),
            scratch_shapes=[pltpu.VMEM((Q_PER_KV, tile, 1), f32), pltpu.VMEM((Q_PER_KV, tile, 1), f32),
                            pltpu.VMEM((Q_PER_KV, tile, HEAD_DIM), f32)]),
        out_shape=jax.ShapeDtypeStruct((nb * t, NSA_W), f32),
        compiler_params=_cparams(("parallel", "parallel", "arbitrary")),
        name="flash_" + mode,
    )(qi_tab, kj_tab, first_tab, *args)


def _combine_kernel(oc_ref, os_ref, ow_ref, g_ref, b_ref, z_ref, o_ref):
    gate = _sigmoid(g_ref[...] + b_ref[...])
    for h in range(N_HEADS):
        hs = slice(h * HEAD_DIM, (h + 1) * HEAD_DIM)
        o = (gate[:, 3 * h:3 * h + 1] * oc_ref[:, hs] + gate[:, 3 * h + 1:3 * h + 2] * os_ref[:, hs]
             + gate[:, 3 * h + 2:3 * h + 3] * ow_ref[:, hs])
        o_ref[:, hs] = (o * _silu(z_ref[:, hs])).astype(bf16)


def combine(o_c, o_s, o_w, gates, bias, proj, tm):
    m = o_c.shape[0]
    row = lambda w, cb: pl.BlockSpec((tm, w), lambda i: (i, cb))
    return pl.pallas_call(
        _combine_kernel,
        grid=(m // tm,),
        in_specs=[row(NSA_W, 0), row(NSA_W, 0), row(NSA_W, 0), row(LANES, 0),
                  pl.BlockSpec((1, LANES), lambda i: (0, 0)),
                  pl.BlockSpec((tm, NSA_W), lambda i: (i, 1))],
        out_specs=row(NSA_W, 0),
        out_shape=jax.ShapeDtypeStruct((m, NSA_W), bf16),
        compiler_params=_cparams(("parallel",)),
        name="nsa_combine",
    )(o_c, o_s, o_w, gates, bias, proj)


def _head_group(shape):
    return lax.shift_right_logical(lax.broadcasted_iota(jnp.int32, shape, 0), 2)


def _cmp_step_kernel(pt_ref, q_ref, *refs, n_pages, q_pos):
    pages = refs[:n_pages]
    oc_ref, sel_ref, blk_sc = refs[n_pages:]
    for i in range(n_pages // 2):
        x = jnp.concatenate([pages[2 * i][0], pages[2 * i + 1][0]], axis=0)
        nblk = x.shape[0] // CMP_BLOCK
        blk_sc[i * nblk:(i + 1) * nblk, :] = jnp.sum(x.reshape(nblk, CMP_BLOCK, KV_W), axis=1) * (1.0 / CMP_BLOCK)
    n_blocks = blk_sc.shape[0]
    qb = q_ref[0].astype(bf16)
    grp = _head_group((N_HEADS, LANES))
    lane = lax.broadcasted_iota(jnp.int32, (N_HEADS, LANES), 1)
    s = jnp.zeros((N_HEADS, LANES), f32)
    for g in range(N_KV):
        kg = _pad_rows(blk_sc[:, g * HEAD_DIM:(g + 1) * HEAD_DIM], LANES).astype(bf16)
        s = jnp.where(grp == g, _mm_nt(qb, kg), s)
    vis = (lane < n_blocks) & ((lane + 1) * CMP_BLOCK - 1 <= q_pos)
    p = _masked_softmax(s * ATT_SCALE, vis)
    pb = p.astype(bf16)
    o = jnp.zeros((N_HEADS, HEAD_DIM), f32)
    for g in range(N_KV):
        vg = _pad_rows(blk_sc[:, KV_W // 2 + g * HEAD_DIM:KV_W // 2 + (g + 1) * HEAD_DIM], LANES).astype(bf16)
        o = jnp.where(grp == g, _mm(pb, vg), o)
    oc_ref[0] = o
    row8 = lax.broadcasted_iota(jnp.int32, (8, LANES), 0)
    imp = jnp.zeros((8, LANES), f32)
    for g in range(N_KV):
        imp_g = jnp.sum(p[g * Q_PER_KV:(g + 1) * Q_PER_KV, :], axis=0, keepdims=True)
        imp = jnp.where(row8 == g, imp_g, imp)
    sel_ref[0] = _select_blocks(imp, jnp.full((8, LANES), q_pos, jnp.int32), q_pos // SEL_BLOCK + 1)


def _page_specs(n_pages, rows):
    return [pl.BlockSpec((1, rows, KV_W), lambda b, pt, p=p: (pt[b * n_pages + p], 0, 0)) for p in range(n_pages)]


def cmp_step(q3, cache, pt_flat, n_pages, q_pos):
    nb = q3.shape[0]
    head_blk = pl.BlockSpec((1, N_HEADS, HEAD_DIM), lambda b, pt: (b, 0, 0))
    return pl.pallas_call(
        functools.partial(_cmp_step_kernel, n_pages=n_pages, q_pos=q_pos),
        grid_spec=pltpu.PrefetchScalarGridSpec(
            num_scalar_prefetch=1,
            grid=(nb,),
            in_specs=[head_blk] + _page_specs(n_pages, PAGE_SIZE),
            out_specs=[head_blk, pl.BlockSpec((1, 8, LANES), lambda b, pt: (b, 0, 0))],
            scratch_shapes=[pltpu.VMEM((n_pages * PAGE_SIZE // CMP_BLOCK, KV_W), f32)]),
        out_shape=[jax.ShapeDtypeStruct((nb, N_HEADS, HEAD_DIM), f32),
                   jax.ShapeDtypeStruct((nb, 8, LANES), f32)],
        compiler_params=_cparams(("parallel",)),
        name="cmp_step",
    )(pt_flat, q3, *([cache] * n_pages))


def _decode_attend(qb, tiles, flags, new_row, s_sc):
    grp = _head_group((N_HEADS, LANES))
    lane = lax.broadcasted_iota(jnp.int32, (N_HEADS, LANES), 1)
    for p, tile in enumerate(tiles):
        x = tile()
        sp = jnp.zeros((N_HEADS, LANES), f32)
        for g in range(N_KV):
            sg = _mm_nt(qb, x[:, g * HEAD_DIM:(g + 1) * HEAD_DIM].astype(bf16)) * ATT_SCALE
            if flags is not None:
                f0 = flags[g:g + 1, 4 * p:4 * p + 1]
                f1 = flags[g:g + 1, 4 * p + 2:4 * p + 3]
                sg = jnp.where(jnp.where(lane < SEL_BLOCK, f0, f1) > 0.5, sg, NEG)
            sp = jnp.where(grp == g, sg, sp)
        s_sc[:, p * LANES:(p + 1) * LANES] = sp
    qf = qb.astype(f32)
    s_new = jnp.zeros((N_HEADS, 1), f32)
    grp1 = _head_group((N_HEADS, 1))
    for g in range(N_KV):
        kn = new_row[:, g * HEAD_DIM:(g + 1) * HEAD_DIM].astype(bf16).astype(f32)
        s_new = jnp.where(grp1 == g, jnp.sum(qf * kn, axis=1, keepdims=True) * ATT_SCALE, s_new)
    s_all = s_sc[...]
    m = jnp.maximum(jnp.max(s_all, axis=1, keepdims=True), s_new)
    p_all = jnp.exp(s_all - m)
    p_new = jnp.exp(s_new - m)
    den = jnp.sum(p_all, axis=1, keepdims=True) + p_new
    o = jnp.zeros((N_HEADS, HEAD_DIM), f32)
    for g in range(N_KV):
        vn = new_row[:, KV_W // 2 + g * HEAD_DIM:KV_W // 2 + (g + 1) * HEAD_DIM].astype(bf16).astype(f32)
        o = jnp.where(grp == g, p_new * vn, o)
    for p, tile in enumerate(tiles):
        x = tile()
        pb = p_all[:, p * LANES:(p + 1) * LANES].astype(bf16)
        for g in range(N_KV):
            vg = x[:, KV_W // 2 + g * HEAD_DIM:KV_W // 2 + (g + 1) * HEAD_DIM].astype(bf16)
            o = o + jnp.where(grp == g, _mm(pb, vg), 0.0)
    return o / den


def _sel_step_kernel(pt_ref, q_ref, sel_ref, new_ref, *refs, n_pages):
    pages = refs[:n_pages]
    o_ref, s_sc = refs[n_pages:]
    tiles = [lambda r=r: r[0] for r in pages]
    o_ref[0] = _decode_attend(q_ref[0], tiles, sel_ref[0], new_ref[0], s_sc)


def sel_step(q3, sel, new_rows, cache, pt_flat, n_pages):
    nb = q3.shape[0]
    head_blk = pl.BlockSpec((1, N_HEADS, HEAD_DIM), lambda b, pt: (b, 0, 0))
    return pl.pallas_call(
        functools.partial(_sel_step_kernel, n_pages=n_pages),
        grid_spec=pltpu.PrefetchScalarGridSpec(
            num_scalar_prefetch=1,
            grid=(nb,),
            in_specs=[head_blk, pl.BlockSpec((1, 8, LANES), lambda b, pt: (b, 0, 0)),
                      pl.BlockSpec((1, 1, KV_W), lambda b, pt: (b, 0, 0))] + _page_specs(n_pages, PAGE_SIZE),
            out_specs=head_blk,
            scratch_shapes=[pltpu.VMEM((N_HEADS, n_pages * PAGE_SIZE), f32)]),
        out_shape=jax.ShapeDtypeStruct((nb, N_HEADS, HEAD_DIM), f32),
        compiler_params=_cparams(("parallel",)),
        name="sel_step",
    )(pt_flat, q3, sel, new_rows, *([cache] * n_pages))


def _win_step_kernel(q_ref, new_ref, win_ref, o_ref, s_sc):
    n_tiles = win_ref.shape[1] // LANES
    tiles = [lambda p=p: win_ref[0, p * LANES:(p + 1) * LANES, :] for p in range(n_tiles)]
    o_ref[0] = _decode_attend(q_ref[0], tiles, None, new_ref[0], s_sc)


def win_step(q3, new_rows, win):
    nb, wb = win.shape[0], win.shape[1]
    head_blk = pl.BlockSpec((1, N_HEADS, HEAD_DIM), lambda b: (b, 0, 0))
    return pl.pallas_call(
        _win_step_kernel,
        grid=(nb,),
        in_specs=[head_blk, pl.BlockSpec((1, 1, KV_W), lambda b: (b, 0, 0)),
                  pl.BlockSpec((1, wb, KV_W), lambda b: (b, 0, 0))],
        out_specs=head_blk,
        out_shape=jax.ShapeDtypeStruct((nb, N_HEADS, HEAD_DIM), f32),
        scratch_shapes=[pltpu.VMEM((N_HEADS, wb), f32)],
        compiler_params=_cparams(("parallel",)),
        name="win_step",
    )(q3, new_rows, win)


def _rope_tables(pos):
    half = HEAD_DIM // 2
    inv = ROPE_THETA ** (-jnp.arange(half, dtype=f32) / half)
    ang = pos.astype(f32)[:, None] * inv[None, :]
    cos, sin = jnp.cos(ang), jnp.sin(ang)
    return jnp.concatenate([cos, cos], axis=1), jnp.concatenate([-sin, sin], axis=1)


def _pad_cols(a, width):
    return jnp.pad(a, ((0, 0), (0, width - a.shape[1])))


def kernel(x_prompt, x_sample, state_pool, state_mlstm_c, state_mlstm_n, state_mlstm_m, cache_kv_cmp, cache_kv_sel, cache_kv_win, page_table, norm0_g, w_in0, b_gate0, w_pool, pool_scale, mh_norm_g, w_out0, norm1_g, w_in1, b_gate1, w_out1, final_g):
    nbp, t, d = x_prompt.shape
    nbs = x_sample.shape[0]
    mp = nbp * t
    n_pages = page_table.shape[1]
    past_len = n_pages * PAGE_SIZE
    wbuf = cache_kv_win.shape[1]

    w0 = w_in0[:, :MAIN_W].astype(bf16)
    wg0 = jnp.concatenate([_pad_cols(w_in0[:, MAIN_W:MAIN_W + M_HEADS], LANES),
                           _pad_cols(w_in0[:, MAIN_W + M_HEADS:], LANES)], axis=1).astype(bf16)
    bias_i = _pad_cols(b_gate0[None, :M_HEADS], LANES)
    bias_f = _pad_cols(b_gate0[None, M_HEADS:], LANES)
    g_lo = NSA_W + 3 * KV_W
    g_hi = g_lo + 3 * N_HEADS
    w1 = jnp.concatenate([w_in1[:, :NSA_W], w_in1[:, g_hi:], w_in1[:, NSA_W:g_lo]], axis=1).astype(bf16)
    wg1 = _pad_cols(w_in1[:, g_lo:g_hi], LANES).astype(bf16)
    bias1 = _pad_cols(b_gate1[None, :], LANES)
    w_pool_b = w_pool.astype(bf16)
    wo0_pool = w_out0[:POOL_W].astype(bf16)
    wo0_m = w_out0[POOL_W:].astype(bf16)
    wo1 = w_out1.astype(bf16)

    xp = x_prompt.reshape(mp, d)
    xs = x_sample.reshape(nbs, d)

    proj_p, gates_p = norm_proj(xp, norm0_g, w0, wg0, 1024, 512)
    proj_s, gates_s = norm_proj(xs, norm0_g, w0, wg0, nbs, 512)

    ypool_p = pool_prompt(proj_p, w_pool_b, pool_scale, nbp, t)
    ym_p, c_p, n_p, m_p = mlstm_prompt(proj_p, gates_p, bias_i, bias_f, mh_norm_g, nbp, t, 256)
    xp1 = out_proj([ypool_p, ym_p], [wo0_pool, wo0_m], xp, None, 512)

    ypool_s = pool_step(state_pool.reshape(nbs, POOL_STATE * POOL_W), proj_s, w_pool_b, pool_scale)
    m0_pad = _pad_cols(state_mlstm_m, LANES)
    ym_s, c_s, n_s, m_s = mlstm_step(proj_s, gates_s, bias_i, bias_f, mh_norm_g,
                                     state_mlstm_c, state_mlstm_n, m0_pad, 8)
    xs1 = out_proj([ypool_s, ym_s], [wo0_pool, wo0_m], xs, None, nbs)

    pool_p = proj_p.reshape(nbp, t, MAIN_W)[:, t - POOL_STATE:, :POOL_W]
    pool_s = jnp.concatenate([state_pool[:, 1:], proj_s[:, None, :POOL_W]], axis=1)

    proj1_p, gates1_p = norm_proj(xp1, norm1_g, w1, wg1, 1024, 512)
    proj1_s, gates1_s = norm_proj(xs1, norm1_g, w1, wg1, nbs, 512)

    tq = 256
    cos_p, sin_p = _rope_tables(jnp.arange(t))
    qrot_p, kvs_p, kvw_p, blocks_p = nsa_prep(proj1_p, cos_p, sin_p, tq, t // tq, True)
    oc_p, sel_p = cmp_prompt(proj1_p, blocks_p, nbp, t, tq)
    os_p = flash_prompt(qrot_p, kvs_p, sel_p, nbp, t, tq, "sel")
    ow_p = flash_prompt(qrot_p, kvw_p, None, nbp, t, tq, "win")
    a1_p = combine(oc_p, os_p, ow_p, gates1_p, bias1, proj1_p, 256)
    y_p = out_proj([a1_p], [wo1], xp1, final_g, 512)

    cos_s, sin_s = _rope_tables(jnp.full((nbs,), past_len))
    qrot_s, kvs_s, kvw_s = nsa_prep(proj1_s, cos_s, sin_s, nbs, 1, False)
    pt_flat = page_table.reshape(-1)
    q3_s = proj1_s[:, :NSA_W].reshape(nbs, N_HEADS, HEAD_DIM)
    qrot3_s = qrot_s.reshape(nbs, N_HEADS, HEAD_DIM)
    n_pool = cache_kv_cmp.shape[0]
    oc_s, sel_s = cmp_step(q3_s, cache_kv_cmp.reshape(n_pool, PAGE_SIZE, KV_W), pt_flat, n_pages, past_len)
    os_s = sel_step(qrot3_s, sel_s, kvs_s.reshape(nbs, 1, KV_W),
                    cache_kv_sel.reshape(n_pool, PAGE_SIZE, KV_W), pt_flat, n_pages)
    ow_s = win_step(qrot3_s, kvw_s.reshape(nbs, 1, KV_W), cache_kv_win.reshape(nbs, wbuf, KV_W))
    a1_s = combine(oc_s.reshape(nbs, NSA_W), os_s.reshape(nbs, NSA_W), ow_s.reshape(nbs, NSA_W),
                   gates1_s, bias1, proj1_s, nbs)
    y_s = out_proj([a1_s], [wo1], xs1, final_g, nbs)

    kv5 = lambda a, rows: a.reshape(-1, rows, 2, N_KV, HEAD_DIM)
    return (y_p.reshape(nbp, t, d), y_s.reshape(nbs, 1, d),
            pool_p, pool_s,
            c_p, c_s, n_p, n_s, m_p[:, 0, :M_HEADS], m_s[:, :M_HEADS],
            kv5(proj1_p[:, 2 * NSA_W:2 * NSA_W + KV_W], t), kv5(proj1_s[:, 2 * NSA_W:2 * NSA_W + KV_W], 1),
            kv5(kvs_p, t), kv5(kvs_s, 1),
            kv5(kvw_p, t)[:, t - wbuf:],
            jnp.concatenate([cache_kv_win, kv5(kvw_s, 1)], axis=1)[:, 1:])
```

```python
import functools

import jax
import jax.numpy as jnp
from jax import lax
from jax.experimental import pallas as pl
from jax.experimental.pallas import tpu as pltpu

f32 = jnp.float32
bf16 = jnp.bfloat16

D_MODEL = 2048
POOL_WINDOWS = (2, 4, 8, 16)
POOL_W = 1024
POOL_GROUP_W = 256
POOL_STATE = 15
M_HEADS = 4
M_W = 1024
M_HEAD_DIM = 256
N_HEADS = 16
HEAD_DIM = 128
N_KV = 4
Q_PER_KV = 4
NSA_W = 2048
KV_W = 1024
KV_ROWS = 2 * N_KV
CMP_BLOCK = 32
SEL_BLOCK = 64
SEL_TOPK = 16
WINDOW = 512
PAGE_SIZE = 128
ROPE_THETA = 10000.0
ATT_SCALE = HEAD_DIM ** -0.5
EPS = 1e-6
MAIN_W = 7168
LANES = 128
NEG = -1e30
VMEM_LIMIT = 48 * 1024 * 1024

_NT = (((1,), (1,)), ((), ()))


def _cparams(sem):
    return pltpu.CompilerParams(dimension_semantics=sem, vmem_limit_bytes=VMEM_LIMIT)


def _sigmoid(x):
    return 1.0 / (1.0 + jnp.exp(-x))


def _silu(x):
    return x * _sigmoid(x)


def _log_sigmoid(x):
    return jnp.minimum(x, 0.0) - jnp.log1p(jnp.exp(-jnp.abs(x)))


def _mm(a, b):
    return jnp.dot(a, b, preferred_element_type=f32)


def _mm_nt(a, b):
    return lax.dot_general(a, b, _NT, preferred_element_type=f32)


def _norm_proj_kernel(x_ref, g_ref, w_ref, wg_ref, o_ref, og_ref, h_ref):
    @pl.when(pl.program_id(1) == 0)
    def _():
        x = x_ref[...]
        r = lax.rsqrt(jnp.mean(x * x, axis=-1, keepdims=True) + EPS)
        h = ((x * r) * g_ref[...]).astype(bf16)
        h_ref[...] = h
        og_ref[...] = _mm(h, wg_ref[...])

    o_ref[...] = _mm(h_ref[...], w_ref[...])


def norm_proj(x, g, w, wg, tm, tn):
    m, d = x.shape
    n, ng = w.shape[1], wg.shape[1]
    return pl.pallas_call(
        _norm_proj_kernel,
        grid=(m // tm, n // tn),
        in_specs=[pl.BlockSpec((tm, d), lambda i, j: (i, 0)),
                  pl.BlockSpec((1, d), lambda i, j: (0, 0)),
                  pl.BlockSpec((d, tn), lambda i, j: (0, j)),
                  pl.BlockSpec((d, ng), lambda i, j: (0, 0))],
        out_specs=[pl.BlockSpec((tm, tn), lambda i, j: (i, j)),
                   pl.BlockSpec((tm, ng), lambda i, j: (i, 0))],
        out_shape=[jax.ShapeDtypeStruct((m, n), f32), jax.ShapeDtypeStruct((m, ng), f32)],
        scratch_shapes=[pltpu.VMEM((tm, d), bf16)],
        compiler_params=_cparams(("parallel", "arbitrary")),
        name="norm_proj",
    )(x, g.reshape(1, d), w, wg)


def _out_proj_kernel(*refs, n_parts, final_norm):
    a_refs = refs[:n_parts]
    w_refs = refs[n_parts:2 * n_parts]
    x_ref = refs[2 * n_parts]
    o_ref = refs[-1]
    acc = x_ref[...]
    for a_ref, w_ref in zip(a_refs, w_refs):
        acc = acc + _mm(a_ref[...], w_ref[...])
    if final_norm:
        fg_ref = refs[2 * n_parts + 1]
        r = lax.rsqrt(jnp.mean(acc * acc, axis=-1, keepdims=True) + EPS)
        acc = (acc * r) * fg_ref[...]
    o_ref[...] = acc


def out_proj(parts, weights, x, final_g, tm):
    m, d = x.shape
    n_parts = len(parts)
    final_norm = final_g is not None
    in_specs = [pl.BlockSpec((tm, a.shape[1]), lambda i: (i, 0)) for a in parts]
    in_specs += [pl.BlockSpec(w.shape, lambda i: (0, 0)) for w in weights]
    in_specs += [pl.BlockSpec((tm, d), lambda i: (i, 0))]
    args = list(parts) + list(weights) + [x]
    if final_norm:
        in_specs += [pl.BlockSpec((1, d), lambda i: (0, 0))]
        args += [final_g.reshape(1, d)]
    return pl.pallas_call(
        functools.partial(_out_proj_kernel, n_parts=n_parts, final_norm=final_norm),
        grid=(m // tm,),
        in_specs=in_specs,
        out_specs=pl.BlockSpec((tm, d), lambda i: (i, 0)),
        out_shape=jax.ShapeDtypeStruct((m, d), f32),
        compiler_params=_cparams(("parallel",)),
        name="out_proj",
    )(*args)


def _pool_kernel(u_ref, z_ref, w_ref, sc_ref, o_ref):
    g = pl.program_id(1)
    x = u_ref[...]
    row = lax.broadcasted_iota(jnp.int32, x.shape, 0)

    def back(a, s):
        return jnp.where(row >= s, pltpu.roll(a, s, axis=0), 0.0)

    s2 = x + back(x, 1)
    s4 = s2 + back(s2, 2)
    s8 = s4 + back(s4, 4)
    s16 = s8 + back(s8, 8)
    win = jnp.where(g == 0, s2, jnp.where(g == 1, s4, jnp.where(g == 2, s8, s16)))
    wlen = lax.shift_left(jnp.int32(2), g)
    cnt = jnp.minimum(row + 1, wlen).astype(f32)
    pooled = win / cnt - x
    y = _mm(pooled.astype(bf16), w_ref[0]) * sc_ref[...]
    o_ref[...] = (y * _silu(z_ref[...])).astype(bf16)


def pool_prompt(proj, w_pool, pool_scale, nb, t):
    ng = len(POOL_WINDOWS)
    return pl.pallas_call(
        _pool_kernel,
        grid=(nb, ng),
        in_specs=[pl.BlockSpec((t, POOL_GROUP_W), lambda b, g: (b, g)),
                  pl.BlockSpec((t, POOL_GROUP_W), lambda b, g: (b, ng + g)),
                  pl.BlockSpec((1, POOL_GROUP_W, POOL_GROUP_W), lambda b, g: (g, 0, 0)),
                  pl.BlockSpec((1, POOL_GROUP_W), lambda b, g: (0, g))],
        out_specs=pl.BlockSpec((t, POOL_GROUP_W), lambda b, g: (b, g)),
        out_shape=jax.ShapeDtypeStruct((nb * t, POOL_W), bf16),
        compiler_params=_cparams(("parallel", "arbitrary")),
        name="pool_prompt",
    )(proj, proj, w_pool, pool_scale.reshape(1, POOL_W))


def _pool_step_kernel(st_ref, u_ref, z_ref, w_ref, sc_ref, o_ref):
    u = u_ref[...]
    for g, wlen in enumerate(POOL_WINDOWS):
        lo = g * POOL_GROUP_W
        ug = u[:, lo:lo + POOL_GROUP_W]
        acc = ug
        for r in range(POOL_STATE + 1 - wlen, POOL_STATE):
            acc = acc + st_ref[:, r * POOL_W + lo:r * POOL_W + lo + POOL_GROUP_W]
        pooled = acc / float(wlen) - ug
        y = _mm(pooled.astype(bf16), w_ref[g]) * sc_ref[:, lo:lo + POOL_GROUP_W]
        o_ref[:, lo:lo + POOL_GROUP_W] = (y * _silu(z_ref[:, lo:lo + POOL_GROUP_W])).astype(bf16)


def pool_step(state_flat, proj, w_pool, pool_scale):
    nb = proj.shape[0]
    return pl.pallas_call(
        _pool_step_kernel,
        grid=(1,),
        in_specs=[pl.BlockSpec(state_flat.shape, lambda i: (0, 0)),
                  pl.BlockSpec((nb, POOL_W), lambda i: (0, 0)),
                  pl.BlockSpec((nb, POOL_W), lambda i: (0, 1)),
                  pl.BlockSpec(w_pool.shape, lambda i: (0, 0, 0)),
                  pl.BlockSpec((1, POOL_W), lambda i: (0, 0))],
        out_specs=pl.BlockSpec((nb, POOL_W), lambda i: (0, 0)),
        out_shape=jax.ShapeDtypeStruct((nb, POOL_W), bf16),
        compiler_params=_cparams(("arbitrary",)),
        name="pool_step",
    )(state_flat, proj, proj, w_pool, pool_scale.reshape(1, POOL_W))


def _head_out(hc, o, z, g):
    hc = hc * _sigmoid(o)
    hc = hc * lax.rsqrt(jnp.mean(hc * hc, axis=-1, keepdims=True) + EPS)
    return ((hc * g) * _silu(z)).astype(bf16)


def _mlstm_kernel(q_ref, k_ref, v_ref, o_ref, z_ref, gi_ref, gf_ref, bi_ref, bf_ref, mhg_ref,
                  y_ref, c_ref, n_ref, m_ref):
    @pl.when(pl.program_id(1) == 0)
    def _():
        c_ref[...] = jnp.zeros_like(c_ref)
        n_ref[...] = jnp.zeros_like(n_ref)
        m_ref[...] = jnp.zeros_like(m_ref)

    ln = q_ref.shape[0]
    gi = gi_ref[...] + bi_ref[...]
    lf = _log_sigmoid(gf_ref[...] + bf_ref[...])
    row = lax.broadcasted_iota(jnp.int32, lf.shape, 0)
    b = lf
    s = 1
    while s < ln:
        b = b + jnp.where(row >= s, pltpu.roll(b, s, axis=0), 0.0)
        s *= 2
    r_t = (gi - b).T
    tt = lax.broadcasted_iota(jnp.int32, (ln, ln), 0)
    ss = lax.broadcasted_iota(jnp.int32, (ln, ln), 1)
    causal = ss <= tt
    lane = lax.broadcasted_iota(jnp.int32, (1, LANES), 1)
    m_vec = m_ref[0]
    for h in range(M_HEADS):
        hs = slice(h * M_HEAD_DIM, (h + 1) * M_HEAD_DIM)
        b_col = b[:, h:h + 1]
        ig_col = gi[:, h:h + 1]
        m_prev = m_vec[:, h:h + 1]
        inter = b_col + m_prev
        dmat = jnp.where(causal, b_col + r_t[h:h + 1, :], -jnp.inf)
        m_t = jnp.maximum(inter, jnp.max(dmat, axis=1, keepdims=True))
        dw = jnp.exp(dmat - m_t)
        iw = jnp.exp(inter - m_t)
        q = q_ref[:, hs]
        k = k_ref[:, hs] * (M_HEAD_DIM ** -0.5)
        v = v_ref[:, hs]
        qb, kb, vb = q.astype(bf16), k.astype(bf16), v.astype(bf16)
        c = c_ref[0, h]
        n = n_ref[0, h:h + 1, :]
        qk = _mm_nt(qb, kb) * dw
        num = iw * _mm(qb, c.astype(bf16)) + _mm(qk.astype(bf16), vb)
        den = iw * jnp.sum(q * n, axis=1, keepdims=True) + jnp.sum(qk, axis=1, keepdims=True)
        hc = num / jnp.maximum(jnp.abs(den), jnp.exp(-m_t))
        y_ref[:, hs] = _head_out(hc, o_ref[:, hs], z_ref[:, hs], mhg_ref[:, hs])
        m_last = m_t[ln - 1:ln, :]
        b_last = b_col[ln - 1:ln, :]
        ws = jnp.exp(b_last - b_col + ig_col - m_last)
        dec = jnp.exp(b_last + m_prev - m_last)
        kw = ws * k
        c_ref[0, h] = dec * c + _mm(kw.T.astype(bf16), vb)
        n_ref[0, h:h + 1, :] = dec * n + jnp.sum(kw, axis=0, keepdims=True)
        m_vec = jnp.where(lane == h, m_last, m_vec)
    m_ref[0] = m_vec


def mlstm_prompt(proj, gates, bias_i, bias_f, mh_norm_g, nb, t, ln):
    nc = t // ln
    col = lambda cb: pl.BlockSpec((ln, M_W), lambda b, c: (b * nc + c, cb))
    gcol = lambda cb: pl.BlockSpec((ln, LANES), lambda b, c: (b * nc + c, cb))
    vec = lambda w: pl.BlockSpec((1, w), lambda b, c: (0, 0))
    return pl.pallas_call(
        _mlstm_kernel,
        grid=(nb, nc),
        in_specs=[col(2), col(3), col(4), col(5), col(6), gcol(0), gcol(1), vec(LANES), vec(LANES), vec(M_W)],
        out_specs=[pl.BlockSpec((ln, M_W), lambda b, c: (b * nc + c, 0)),
                   pl.BlockSpec((1, M_HEADS, M_HEAD_DIM, M_HEAD_DIM), lambda b, c: (b, 0, 0, 0)),
                   pl.BlockSpec((1, M_HEADS, M_HEAD_DIM), lambda b, c: (b, 0, 0)),
                   pl.BlockSpec((1, 1, LANES), lambda b, c: (b, 0, 0))],
        out_shape=[jax.ShapeDtypeStruct((nb * t, M_W), bf16),
                   jax.ShapeDtypeStruct((nb, M_HEADS, M_HEAD_DIM, M_HEAD_DIM), f32),
                   jax.ShapeDtypeStruct((nb, M_HEADS, M_HEAD_DIM), f32),
                   jax.ShapeDtypeStruct((nb, 1, LANES), f32)],
        compiler_params=_cparams(("parallel", "arbitrary")),
        name="mlstm_prompt",
    )(proj, proj, proj, proj, proj, gates, gates, bias_i, bias_f, mh_norm_g.reshape(1, M_W))


def _mlstm_step_kernel(q_ref, k_ref, v_ref, o_ref, z_ref, gi_ref, gf_ref, bi_ref, bf_ref, mhg_ref,
                       c_ref, n_ref, m_ref, y_ref, co_ref, no_ref, mo_ref):
    nb = q_ref.shape[0]
    gi = gi_ref[...] + bi_ref[...]
    lf = _log_sigmoid(gf_ref[...] + bf_ref[...])
    inter = lf + m_ref[...]
    m_t = jnp.maximum(inter, gi)
    dw_all = jnp.exp(gi - m_t)
    iw_all = jnp.exp(inter - m_t)
    em_all = jnp.exp(-m_t)
    mo_ref[...] = m_t
    d0 = lax.broadcasted_iota(jnp.int32, (M_HEAD_DIM, M_HEAD_DIM), 0)
    d1 = lax.broadcasted_iota(jnp.int32, (M_HEAD_DIM, M_HEAD_DIM), 1)
    eye = d0 == d1
    for j in range(nb):
        for h in range(M_HEADS):
            hs = slice(h * M_HEAD_DIM, (h + 1) * M_HEAD_DIM)
            dw = dw_all[j:j + 1, h:h + 1]
            iw = iw_all[j:j + 1, h:h + 1]
            em = em_all[j:j + 1, h:h + 1]
            q = q_ref[j:j + 1, hs]
            k = k_ref[j:j + 1, hs] * (M_HEAD_DIM ** -0.5)
            v = v_ref[j:j + 1, hs]
            c = c_ref[j, h]
            n = n_ref[j, h:h + 1, :]
            qc = _mm(jnp.broadcast_to(q, (8, M_HEAD_DIM)).astype(bf16), c.astype(bf16))[0:1, :]
            qk = jnp.sum(q * k, axis=1, keepdims=True) * dw
            num = iw * qc + qk * v
            den = iw * jnp.sum(q * n, axis=1, keepdims=True) + qk
            hc = num / jnp.maximum(jnp.abs(den), em)
            y_ref[j:j + 1, hs] = _head_out(hc, o_ref[j:j + 1, hs], z_ref[j:j + 1, hs], mhg_ref[:, hs])
            kdiag = jnp.where(eye, jnp.broadcast_to(k, (M_HEAD_DIM, M_HEAD_DIM)), 0.0).astype(bf16)
            vrep = jnp.broadcast_to(v, (M_HEAD_DIM, M_HEAD_DIM)).astype(bf16)
            co_ref[j, h] = iw * c + dw * _mm(kdiag, vrep)
            no_ref[j, h:h + 1, :] = iw * n + dw * k


def mlstm_step(proj, gates, bias_i, bias_f, mh_norm_g, c0, n0, m0_pad, bb):
    nb = proj.shape[0]
    col = lambda cb: pl.BlockSpec((bb, M_W), lambda i: (i, cb))
    gcol = lambda cb: pl.BlockSpec((bb, LANES), lambda i: (i, cb))
    vec = lambda w: pl.BlockSpec((1, w), lambda i: (0, 0))
    cspec = pl.BlockSpec((bb, M_HEADS, M_HEAD_DIM, M_HEAD_DIM), lambda i: (i, 0, 0, 0))
    nspec = pl.BlockSpec((bb, M_HEADS, M_HEAD_DIM), lambda i: (i, 0, 0))
    return pl.pallas_call(
        _mlstm_step_kernel,
        grid=(nb // bb,),
        in_specs=[col(2), col(3), col(4), col(5), col(6), gcol(0), gcol(1), vec(LANES), vec(LANES), vec(M_W),
                  cspec, nspec, gcol(0)],
        out_specs=[pl.BlockSpec((bb, M_W), lambda i: (i, 0)), cspec, nspec, gcol(0)],
        out_shape=[jax.ShapeDtypeStruct((nb, M_W), bf16),
                   jax.ShapeDtypeStruct(c0.shape, f32),
                   jax.ShapeDtypeStruct(n0.shape, f32),
                   jax.ShapeDtypeStruct((nb, LANES), f32)],
        compiler_params=_cparams(("parallel",)),
        name="mlstm_step",
    )(proj, proj, proj, proj, proj, gates, gates, bias_i, bias_f, mh_norm_g.reshape(1, M_W), c0, n0, m0_pad)


def _rope(x, cos, sin_signed):
    return x * cos + pltpu.roll(x, HEAD_DIM // 2, axis=1) * sin_signed


def _nsa_prep_kernel(*refs, with_blocks):
    q_ref, kvc_ref, kvs_ref, kvw_ref, cos_ref, sin_ref = refs[:6]
    qr_ref, kvs_o, kvw_o = refs[6:9]
    cos, sin = cos_ref[...], sin_ref[...]
    for h in range(N_HEADS):
        hs = slice(h * HEAD_DIM, (h + 1) * HEAD_DIM)
        qr_ref[:, hs] = _rope(q_ref[:, hs], cos, sin).astype(bf16)
    for src, dst in ((kvs_ref, kvs_o), (kvw_ref, kvw_o)):
        for g in range(N_KV):
            gs = slice(g * HEAD_DIM, (g + 1) * HEAD_DIM)
            dst[:, gs] = _rope(src[:, gs], cos, sin)
        dst[:, KV_W // 2:] = src[:, KV_W // 2:]
    if with_blocks:
        blk_ref = refs[9]
        x = kvc_ref[...]
        nblk = x.shape[0] // CMP_BLOCK
        blk_ref[...] = jnp.sum(x.reshape(nblk, CMP_BLOCK, KV_W), axis=1) * (1.0 / CMP_BLOCK)


def nsa_prep(proj, cos, sin, tq, n_pos_blocks, with_blocks):
    m = proj.shape[0]
    row = lambda w, cb: pl.BlockSpec((tq, w), lambda i: (i, cb))
    tab = pl.BlockSpec((tq, HEAD_DIM), lambda i: (i % n_pos_blocks, 0))
    out_specs = [row(NSA_W, 0), row(KV_W, 0), row(KV_W, 0)]
    out_shape = [jax.ShapeDtypeStruct((m, NSA_W), bf16), jax.ShapeDtypeStruct((m, KV_W), f32),
                 jax.ShapeDtypeStruct((m, KV_W), f32)]
    if with_blocks:
        out_specs.append(pl.BlockSpec((tq // CMP_BLOCK, KV_W), lambda i: (i, 0)))
        out_shape.append(jax.ShapeDtypeStruct((m // CMP_BLOCK, KV_W), f32))
    return pl.pallas_call(
        functools.partial(_nsa_prep_kernel, with_blocks=with_blocks),
        grid=(m // tq,),
        in_specs=[row(NSA_W, 0), row(KV_W, 4), row(KV_W, 5), row(KV_W, 6), tab, tab],
        out_specs=out_specs,
        out_shape=out_shape,
        compiler_params=_cparams(("parallel",)),
        name="nsa_prep",
    )(proj, proj, proj, proj, cos, sin)


def _select_blocks(imp, q_pos, n_cand):
    lane = lax.broadcasted_iota(jnp.int32, imp.shape, 1)
    pair = imp + pltpu.roll(imp, LANES - 1, axis=1)
    cur2 = lax.shift_left(lax.shift_right_logical(q_pos, 6), 1)
    valid = ((lane & 1) == 0) & (lane <= cur2)
    v = jnp.where(lane == cur2, jnp.inf, pair)
    v = jnp.where(valid, v, -jnp.inf)
    cnt = jnp.zeros(imp.shape, f32)
    for i in range(n_cand):
        vi = v[:, 2 * i:2 * i + 1]
        before = jnp.where(lane > 2 * i, 1.0, 0.0)
        cnt = cnt + jnp.where(vi > v, 1.0, 0.0) + jnp.where(vi == v, before, 0.0)
    return jnp.where(valid & (cnt < SEL_TOPK), 1.0, 0.0)


def _pad_rows(x, rows):
    return jnp.concatenate([x, jnp.zeros((rows - x.shape[0], x.shape[1]), x.dtype)], axis=0)


def _masked_softmax(s, mask):
    s = jnp.where(mask, s, -jnp.inf)
    m = jnp.max(s, axis=-1, keepdims=True)
    m = jnp.where(m > -jnp.inf, m, 0.0)
    p = jnp.exp(s - m)
    return p / jnp.maximum(jnp.sum(p, axis=-1, keepdims=True), 1e-30)


def _cmp_prompt_kernel(q_ref, blk_ref, oc_ref, sel_ref, *, n_blocks):
    tq = q_ref.shape[0]
    t0 = pl.program_id(1) * tq
    lane = lax.broadcasted_iota(jnp.int32, (tq, LANES), 1)
    q_pos = lax.broadcasted_iota(jnp.int32, (tq, LANES), 0) + t0
    vis = (lane < n_blocks) & ((lane + 1) * CMP_BLOCK - 1 <= q_pos)
    for g in range(N_KV):
        kg = _pad_rows(blk_ref[:, g * HEAD_DIM:(g + 1) * HEAD_DIM], LANES).astype(bf16)
        vg = _pad_rows(blk_ref[:, KV_W // 2 + g * HEAD_DIM:KV_W // 2 + (g + 1) * HEAD_DIM], LANES).astype(bf16)
        imp = jnp.zeros((tq, LANES), f32)
        for r in range(Q_PER_KV):
            hs = slice((g * Q_PER_KV + r) * HEAD_DIM, (g * Q_PER_KV + r + 1) * HEAD_DIM)
            s = _mm_nt(q_ref[:, hs].astype(bf16), kg) * ATT_SCALE
            p = _masked_softmax(s, vis)
            oc_ref[:, hs] = _mm(p.astype(bf16), vg)
            imp = imp + p
        sel_ref[:, g * LANES:(g + 1) * LANES] = _select_blocks(imp, q_pos, n_blocks // 2).astype(bf16)


def cmp_prompt(proj, blocks, nb, t, tq):
    nq = t // tq
    n_blocks = t // CMP_BLOCK
    return pl.pallas_call(
        functools.partial(_cmp_prompt_kernel, n_blocks=n_blocks),
        grid=(nb, nq),
        in_specs=[pl.BlockSpec((tq, NSA_W), lambda b, i: (b * nq + i, 0)),
                  pl.BlockSpec((n_blocks, KV_W), lambda b, i: (b, 0))],
        out_specs=[pl.BlockSpec((tq, NSA_W), lambda b, i: (b * nq + i, 0)),
                   pl.BlockSpec((tq, N_KV * LANES), lambda b, i: (b * nq + i, 0))],
        out_shape=[jax.ShapeDtypeStruct((nb * t, NSA_W), f32),
                   jax.ShapeDtypeStruct((nb * t, N_KV * LANES), bf16)],
        compiler_params=_cparams(("parallel", "parallel")),
        name="cmp_prompt",
    )(proj, blocks)


def _flash_kernel(qi_tab, kj_tab, first_tab, *refs, mode):
    if mode == "sel":
        q_ref, k_ref, v_ref, sel_ref, o_ref, m_sc, l_sc, acc_sc = refs
    else:
        q_ref, k_ref, v_ref, o_ref, m_sc, l_sc, acc_sc = refs
    step = pl.program_id(2)
    qi, kj = qi_tab[step], kj_tab[step]
    tq, tk = q_ref.shape[0], k_ref.shape[0]

    @pl.when(first_tab[step] == 1)
    def _():
        m_sc[...] = jnp.full_like(m_sc, NEG)
        l_sc[...] = jnp.zeros_like(l_sc)
        acc_sc[...] = jnp.zeros_like(acc_sc)

    kb = k_ref[...].astype(bf16)
    vb = v_ref[...].astype(bf16)
    dpos = (lax.broadcasted_iota(jnp.int32, (tq, tk), 0) + qi * tq
            - lax.broadcasted_iota(jnp.int32, (tq, tk), 1) - kj * tk)
    if mode == "sel":
        er = lax.broadcasted_iota(jnp.int32, (LANES, tk), 0)
        ec = lax.broadcasted_iota(jnp.int32, (LANES, tk), 1) + kj * tk
        hit = ((er & 1) == 0) & (lax.shift_right_logical(er, 1) == lax.shift_right_logical(ec, 6))
        expand = jnp.where(hit, 1.0, 0.0).astype(bf16)
        mask = (dpos >= 0) & (_mm(sel_ref[...], expand) > 0.5)
    else:
        mask = (dpos >= 0) & (dpos <= WINDOW)
    for r in range(Q_PER_KV):
        hs = slice(r * HEAD_DIM, (r + 1) * HEAD_DIM)
        s = jnp.where(mask, _mm_nt(q_ref[:, hs], kb) * ATT_SCALE, NEG)
        m_old = m_sc[r]
        m_new = jnp.maximum(m_old, jnp.max(s, axis=1, keepdims=True))
        alpha = jnp.exp(m_old - m_new)
        p = jnp.exp(s - m_new)
        l_sc[r] = alpha * l_sc[r] + jnp.sum(p, axis=1, keepdims=True)
        acc_sc[r] = alpha * acc_sc[r] + _mm(p.astype(bf16), vb)
        m_sc[r] = m_new

    @pl.when(kj == qi)
    def _():
        for r in range(Q_PER_KV):
            o_ref[:, r * HEAD_DIM:(r + 1) * HEAD_DIM] = acc_sc[r] / l_sc[r]


def _tile_pairs(nq, reach):
    qi, kj, first = [], [], []
    for i in range(nq):
        lo = max(0, i - reach)
        for j in range(lo, i + 1):
            qi.append(i)
            kj.append(j)
            first.append(1 if j == lo else 0)
    as_i32 = lambda a: jnp.asarray(a, jnp.int32)
    return as_i32(qi), as_i32(kj), as_i32(first)


def flash_prompt(q_rot, kv, sel, nb, t, tile, mode):
    nq = t // tile
    reach = nq if mode == "sel" else -(-WINDOW // tile)
    qi_tab, kj_tab, first_tab = _tile_pairs(nq, reach)
    qw = Q_PER_KV * HEAD_DIM
    in_specs = [pl.BlockSpec((tile, qw), lambda b, g, s, qi, kj, fi: (b * nq + qi[s], g)),
                pl.BlockSpec((tile, HEAD_DIM), lambda b, g, s, qi, kj, fi: (b * nq + kj[s], g)),
                pl.BlockSpec((tile, HEAD_DIM), lambda b, g, s, qi, kj, fi: (b * nq + kj[s], N_KV + g))]
    args = [q_rot, kv, kv]
    if mode == "sel":
        in_specs.append(pl.BlockSpec((tile, LANES), lambda b, g, s, qi, kj, fi: (b * nq + qi[s], g)))
        args.append(sel)
    return pl.pallas_call(
        functools.partial(_flash_kernel, mode=mode),
        grid_spec=pltpu.PrefetchScalarGridSpec(
            num_scalar_prefetch=3,
            grid=(nb, N_KV, int(qi_tab.shape[0])),
            in_specs=in_specs,
            out_specs=pl.BlockSpec((tile, qw), lambda b, g, s, qi, kj, fi: (b * nq + qi[s], g)),
            scratch_shapes=[pltpu.VMEM((Q_PER_KV, tile, 1), f32), pltpu.VMEM((Q_PER_KV, tile, 1), f32),
                            pltpu.VMEM((Q_PER_KV, tile, HEAD_DIM), f32)]),
        out_shape=jax.ShapeDtypeStruct((nb * t, NSA_W), f32),
        compiler_params=_cparams(("parallel", "parallel", "arbitrary")),
        name="flash_" + mode,
    )(qi_tab, kj_tab, first_tab, *args)


def _combine_kernel(oc_ref, os_ref, ow_ref, g_ref, b_ref, z_ref, o_ref):
    gate = _sigmoid(g_ref[...] + b_ref[...])
    for h in range(N_HEADS):
        hs = slice(h * HEAD_DIM, (h + 1) * HEAD_DIM)
        o = (gate[:, 3 * h:3 * h + 1] * oc_ref[:, hs] + gate[:, 3 * h + 1:3 * h + 2] * os_ref[:, hs]
             + gate[:, 3 * h + 2:3 * h + 3] * ow_ref[:, hs])
        o_ref[:, hs] = (o * _silu(z_ref[:, hs])).astype(bf16)


def combine(o_c, o_s, o_w, gates, bias, proj, tm):
    m = o_c.shape[0]
    row = lambda w, cb: pl.BlockSpec((tm, w), lambda i: (i, cb))
    return pl.pallas_call(
        _combine_kernel,
        grid=(m // tm,),
        in_specs=[row(NSA_W, 0), row(NSA_W, 0), row(NSA_W, 0), row(LANES, 0),
                  pl.BlockSpec((1, LANES), lambda i: (0, 0)),
                  pl.BlockSpec((tm, NSA_W), lambda i: (i, 1))],
        out_specs=row(NSA_W, 0),
        out_shape=jax.ShapeDtypeStruct((m, NSA_W), bf16),
        compiler_params=_cparams(("parallel",)),
        name="nsa_combine",
    )(o_c, o_s, o_w, gates, bias, proj)


def _head_group(shape):
    return lax.shift_right_logical(lax.broadcasted_iota(jnp.int32, shape, 0), 2)


def _cmp_step_kernel(pt_ref, q_ref, *refs, n_pages, q_pos):
    pages = refs[:n_pages]
    oc_ref, sel_ref, blk_sc = refs[n_pages:]
    per_page = PAGE_SIZE // CMP_BLOCK
    for p in range(n_pages):
        x = pages[p][0].reshape(per_page, CMP_BLOCK, KV_ROWS, HEAD_DIM)
        means = jnp.sum(x, axis=1) * (1.0 / CMP_BLOCK)
        blk_sc[p * per_page * KV_ROWS:(p + 1) * per_page * KV_ROWS, :] = means.reshape(per_page * KV_ROWS, HEAD_DIM)
    n_blocks = blk_sc.shape[0] // KV_ROWS
    qb = q_ref[0].astype(bf16)
    grp = _head_group((N_HEADS, LANES))
    lane = lax.broadcasted_iota(jnp.int32, (N_HEADS, LANES), 1)
    s = jnp.zeros((N_HEADS, LANES), f32)
    for g in range(N_KV):
        kg = _pad_rows(blk_sc[pl.ds(g, n_blocks, stride=KV_ROWS), :], LANES).astype(bf16)
        s = jnp.where(grp == g, _mm_nt(qb, kg), s)
    vis = (lane < n_blocks) & ((lane + 1) * CMP_BLOCK - 1 <= q_pos)
    p = _masked_softmax(s * ATT_SCALE, vis)
    pb = p.astype(bf16)
    o = jnp.zeros((N_HEADS, HEAD_DIM), f32)
    for g in range(N_KV):
        vg = _pad_rows(blk_sc[pl.ds(N_KV + g, n_blocks, stride=KV_ROWS), :], LANES).astype(bf16)
        o = jnp.where(grp == g, _mm(pb, vg), o)
    oc_ref[0] = o
    row8 = lax.broadcasted_iota(jnp.int32, (8, LANES), 0)
    imp = jnp.zeros((8, LANES), f32)
    for g in range(N_KV):
        imp_g = jnp.sum(p[g * Q_PER_KV:(g + 1) * Q_PER_KV, :], axis=0, keepdims=True)
        imp = jnp.where(row8 == g, imp_g, imp)
    sel_ref[0] = _select_blocks(imp, jnp.full((8, LANES), q_pos, jnp.int32), q_pos // SEL_BLOCK + 1)


def _page_specs(n_pages):
    return [pl.BlockSpec((1, PAGE_SIZE * KV_ROWS, HEAD_DIM), lambda b, pt, p=p: (pt[b * n_pages + p], 0, 0))
            for p in range(n_pages)]


def cmp_step(q3, cache, pt_flat, n_pages, q_pos):
    nb = q3.shape[0]
    head_blk = pl.BlockSpec((1, N_HEADS, HEAD_DIM), lambda b, pt: (b, 0, 0))
    return pl.pallas_call(
        functools.partial(_cmp_step_kernel, n_pages=n_pages, q_pos=q_pos),
        grid_spec=pltpu.PrefetchScalarGridSpec(
            num_scalar_prefetch=1,
            grid=(nb,),
            in_specs=[head_blk] + _page_specs(n_pages),
            out_specs=[head_blk, pl.BlockSpec((1, 8, LANES), lambda b, pt: (b, 0, 0))],
            scratch_shapes=[pltpu.VMEM((n_pages * PAGE_SIZE // CMP_BLOCK * KV_ROWS, HEAD_DIM), f32)]),
        out_shape=[jax.ShapeDtypeStruct((nb, N_HEADS, HEAD_DIM), f32),
                   jax.ShapeDtypeStruct((nb, 8, LANES), f32)],
        compiler_params=_cparams(("parallel",)),
        name="cmp_step",
    )(pt_flat, q3, *([cache] * n_pages))


def _decode_attend(qb, n_tiles, kv_tile, flags, new_row, s_sc):
    grp = _head_group((N_HEADS, LANES))
    lane = lax.broadcasted_iota(jnp.int32, (N_HEADS, LANES), 1)
    for p in range(n_tiles):
        sp = jnp.zeros((N_HEADS, LANES), f32)
        for g in range(N_KV):
            sg = _mm_nt(qb, kv_tile(p, g).astype(bf16)) * ATT_SCALE
            if flags is not None:
                f0 = flags[g:g + 1, 4 * p:4 * p + 1]
                f1 = flags[g:g + 1, 4 * p + 2:4 * p + 3]
                sg = jnp.where(jnp.where(lane < SEL_BLOCK, f0, f1) > 0.5, sg, NEG)
            sp = jnp.where(grp == g, sg, sp)
        s_sc[:, p * LANES:(p + 1) * LANES] = sp
    qf = qb.astype(f32)
    s_new = jnp.zeros((N_HEADS, 1), f32)
    grp1 = _head_group((N_HEADS, 1))
    for g in range(N_KV):
        kn = new_row[:, g * HEAD_DIM:(g + 1) * HEAD_DIM].astype(bf16).astype(f32)
        s_new = jnp.where(grp1 == g, jnp.sum(qf * kn, axis=1, keepdims=True) * ATT_SCALE, s_new)
    s_all = s_sc[...]
    m = jnp.maximum(jnp.max(s_all, axis=1, keepdims=True), s_new)
    p_all = jnp.exp(s_all - m)
    p_new = jnp.exp(s_new - m)
    den = jnp.sum(p_all, axis=1, keepdims=True) + p_new
    o = jnp.zeros((N_HEADS, HEAD_DIM), f32)
    for g in range(N_KV):
        vn = new_row[:, KV_W // 2 + g * HEAD_DIM:KV_W // 2 + (g + 1) * HEAD_DIM].astype(bf16).astype(f32)
        o = jnp.where(grp == g, p_new * vn, o)
    for p in range(n_tiles):
        pb = p_all[:, p * LANES:(p + 1) * LANES].astype(bf16)
        for g in range(N_KV):
            o = o + jnp.where(grp == g, _mm(pb, kv_tile(p, N_KV + g).astype(bf16)), 0.0)
    return o / den


def _sel_step_kernel(pt_ref, q_ref, sel_ref, new_ref, *refs, n_pages):
    pages = refs[:n_pages]
    o_ref, s_sc = refs[n_pages:]
    kv_tile = lambda p, c: pages[p][0, pl.ds(c, PAGE_SIZE, stride=KV_ROWS), :]
    o_ref[0] = _decode_attend(q_ref[0], n_pages, kv_tile, sel_ref[0], new_ref[0], s_sc)


def sel_step(q3, sel, new_rows, cache, pt_flat, n_pages):
    nb = q3.shape[0]
    head_blk = pl.BlockSpec((1, N_HEADS, HEAD_DIM), lambda b, pt: (b, 0, 0))
    return pl.pallas_call(
        functools.partial(_sel_step_kernel, n_pages=n_pages),
        grid_spec=pltpu.PrefetchScalarGridSpec(
            num_scalar_prefetch=1,
            grid=(nb,),
            in_specs=[head_blk, pl.BlockSpec((1, 8, LANES), lambda b, pt: (b, 0, 0)),
                      pl.BlockSpec((1, 1, KV_W), lambda b, pt: (b, 0, 0))] + _page_specs(n_pages),
            out_specs=head_blk,
            scratch_shapes=[pltpu.VMEM((N_HEADS, n_pages * PAGE_SIZE), f32)]),
        out_shape=jax.ShapeDtypeStruct((nb, N_HEADS, HEAD_DIM), f32),
        compiler_params=_cparams(("parallel",)),
        name="sel_step",
    )(pt_flat, q3, sel, new_rows, *([cache] * n_pages))


def _win_step_kernel(q_ref, new_ref, win_ref, o_ref, s_sc):
    n_tiles = win_ref.shape[1] // (LANES * KV_ROWS)
    kv_tile = lambda p, c: win_ref[0, pl.ds(p * LANES * KV_ROWS + c, LANES, stride=KV_ROWS), :]
    o_ref[0] = _decode_attend(q_ref[0], n_tiles, kv_tile, None, new_ref[0], s_sc)


def win_step(q3, new_rows, win):
    nb, wrows = win.shape[0], win.shape[1]
    head_blk = pl.BlockSpec((1, N_HEADS, HEAD_DIM), lambda b: (b, 0, 0))
    return pl.pallas_call(
        _win_step_kernel,
        grid=(nb,),
        in_specs=[head_blk, pl.BlockSpec((1, 1, KV_W), lambda b: (b, 0, 0)),
                  pl.BlockSpec((1, wrows, HEAD_DIM), lambda b: (b, 0, 0))],
        out_specs=head_blk,
        out_shape=jax.ShapeDtypeStruct((nb, N_HEADS, HEAD_DIM), f32),
        scratch_shapes=[pltpu.VMEM((N_HEADS, wrows // KV_ROWS), f32)],
        compiler_params=_cparams(("parallel",)),
        name="win_step",
    )(q3, new_rows, win)


def _rope_tables(pos):
    half = HEAD_DIM // 2
    inv = ROPE_THETA ** (-jnp.arange(half, dtype=f32) / half)
    ang = pos.astype(f32)[:, None] * inv[None, :]
    cos, sin = jnp.cos(ang), jnp.sin(ang)
    return jnp.concatenate([cos, cos], axis=1), jnp.concatenate([-sin, sin], axis=1)


def _pad_cols(a, width):
    return jnp.pad(a, ((0, 0), (0, width - a.shape[1])))


def kernel(x_prompt, x_sample, state_pool, state_mlstm_c, state_mlstm_n, state_mlstm_m, cache_kv_cmp, cache_kv_sel, cache_kv_win, page_table, norm0_g, w_in0, b_gate0, w_pool, pool_scale, mh_norm_g, w_out0, norm1_g, w_in1, b_gate1, w_out1, final_g):
    nbp, t, d = x_prompt.shape
    nbs = x_sample.shape[0]
    mp = nbp * t
    n_pages = page_table.shape[1]
    past_len = n_pages * PAGE_SIZE
    wbuf = cache_kv_win.shape[1]

    w0 = w_in0[:, :MAIN_W].astype(bf16)
    wg0 = jnp.concatenate([_pad_cols(w_in0[:, MAIN_W:MAIN_W + M_HEADS], LANES),
                           _pad_cols(w_in0[:, MAIN_W + M_HEADS:], LANES)], axis=1).astype(bf16)
    bias_i = _pad_cols(b_gate0[None, :M_HEADS], LANES)
    bias_f = _pad_cols(b_gate0[None, M_HEADS:], LANES)
    g_lo = NSA_W + 3 * KV_W
    g_hi = g_lo + 3 * N_HEADS
    w1 = jnp.concatenate([w_in1[:, :NSA_W], w_in1[:, g_hi:], w_in1[:, NSA_W:g_lo]], axis=1).astype(bf16)
    wg1 = _pad_cols(w_in1[:, g_lo:g_hi], LANES).astype(bf16)
    bias1 = _pad_cols(b_gate1[None, :], LANES)
    w_pool_b = w_pool.astype(bf16)
    wo0_pool = w_out0[:POOL_W].astype(bf16)
    wo0_m = w_out0[POOL_W:].astype(bf16)
    wo1 = w_out1.astype(bf16)

    xp = x_prompt.reshape(mp, d)
    xs = x_sample.reshape(nbs, d)

    proj_p, gates_p = norm_proj(xp, norm0_g, w0, wg0, 1024, 512)
    proj_s, gates_s = norm_proj(xs, norm0_g, w0, wg0, nbs, 512)

    ypool_p = pool_prompt(proj_p, w_pool_b, pool_scale, nbp, t)
    ym_p, c_p, n_p, m_p = mlstm_prompt(proj_p, gates_p, bias_i, bias_f, mh_norm_g, nbp, t, 256)
    xp1 = out_proj([ypool_p, ym_p], [wo0_pool, wo0_m], xp, None, 512)

    ypool_s = pool_step(state_pool.reshape(nbs, POOL_STATE * POOL_W), proj_s, w_pool_b, pool_scale)
    m0_pad = _pad_cols(state_mlstm_m, LANES)
    ym_s, c_s, n_s, m_s = mlstm_step(proj_s, gates_s, bias_i, bias_f, mh_norm_g,
                                     state_mlstm_c, state_mlstm_n, m0_pad, 8)
    xs1 = out_proj([ypool_s, ym_s], [wo0_pool, wo0_m], xs, None, nbs)

    pool_p = proj_p.reshape(nbp, t, MAIN_W)[:, t - POOL_STATE:, :POOL_W]
    pool_s = jnp.concatenate([state_pool[:, 1:], proj_s[:, None, :POOL_W]], axis=1)

    proj1_p, gates1_p = norm_proj(xp1, norm1_g, w1, wg1, 1024, 512)
    proj1_s, gates1_s = norm_proj(xs1, norm1_g, w1, wg1, nbs, 512)

    tq = 256
    cos_p, sin_p = _rope_tables(jnp.arange(t))
    qrot_p, kvs_p, kvw_p, blocks_p = nsa_prep(proj1_p, cos_p, sin_p, tq, t // tq, True)
    oc_p, sel_p = cmp_prompt(proj1_p, blocks_p, nbp, t, tq)
    os_p = flash_prompt(qrot_p, kvs_p, sel_p, nbp, t, tq, "sel")
    ow_p = flash_prompt(qrot_p, kvw_p, None, nbp, t, tq, "win")
    a1_p = combine(oc_p, os_p, ow_p, gates1_p, bias1, proj1_p, 256)
    y_p = out_proj([a1_p], [wo1], xp1, final_g, 512)

    cos_s, sin_s = _rope_tables(jnp.full((nbs,), past_len))
    qrot_s, kvs_s, kvw_s = nsa_prep(proj1_s, cos_s, sin_s, nbs, 1, False)
    pt_flat = page_table.reshape(-1)
    q3_s = proj1_s[:, :NSA_W].reshape(nbs, N_HEADS, HEAD_DIM)
    qrot3_s = qrot_s.reshape(nbs, N_HEADS, HEAD_DIM)
    n_pool = cache_kv_cmp.shape[0]
    page_rows = PAGE_SIZE * KV_ROWS
    oc_s, sel_s = cmp_step(q3_s, cache_kv_cmp.reshape(n_pool, page_rows, HEAD_DIM), pt_flat, n_pages, past_len)
    os_s = sel_step(qrot3_s, sel_s, kvs_s.reshape(nbs, 1, KV_W),
                    cache_kv_sel.reshape(n_pool, page_rows, HEAD_DIM), pt_flat, n_pages)
    ow_s = win_step(qrot3_s, kvw_s.reshape(nbs, 1, KV_W), cache_kv_win.reshape(nbs, wbuf * KV_ROWS, HEAD_DIM))
    a1_s = combine(oc_s.reshape(nbs, NSA_W), os_s.reshape(nbs, NSA_W), ow_s.reshape(nbs, NSA_W),
                   gates1_s, bias1, proj1_s, nbs)
    y_s = out_proj([a1_s], [wo1], xs1, final_g, nbs)

    kv5 = lambda a, rows: a.reshape(-1, rows, 2, N_KV, HEAD_DIM)
    return (y_p.reshape(nbp, t, d), y_s.reshape(nbs, 1, d),
            pool_p, pool_s,
            c_p, c_s, n_p, n_s, m_p[:, 0, :M_HEADS], m_s[:, :M_HEADS],
            kv5(proj1_p[:, 2 * NSA_W:2 * NSA_W + KV_W], t), kv5(proj1_s[:, 2 * NSA_W:2 * NSA_W + KV_W], 1),
            kv5(kvs_p, t), kv5(kvs_s, 1),
            kv5(kvw_p, t)[:, t - wbuf:],
            jnp.concatenate([cache_kv_win, kv5(kvw_s, 1)], axis=1)[:, 1:])
```

```python
import functools

import jax
import jax.numpy as jnp
from jax import lax
from jax.experimental import pallas as pl
from jax.experimental.pallas import tpu as pltpu

f32 = jnp.float32
bf16 = jnp.bfloat16

D_MODEL = 2048
POOL_WINDOWS = (2, 4, 8, 16)
POOL_W = 1024
POOL_GROUP_W = 256
POOL_STATE = 15
M_HEADS = 4
M_W = 1024
M_HEAD_DIM = 256
N_HEADS = 16
HEAD_DIM = 128
N_KV = 4
Q_PER_KV = 4
NSA_W = 2048
KV_W = 1024
KV_ROWS = 2 * N_KV
CMP_BLOCK = 32
SEL_BLOCK = 64
SEL_TOPK = 16
WINDOW = 512
PAGE_SIZE = 128
ROPE_THETA = 10000.0
ATT_SCALE = HEAD_DIM ** -0.5
EPS = 1e-6
MAIN_W = 7168
LANES = 128
NEG = -1e30
MASK_BIAS = -(2.0 ** 100)
LOG2E = 1.4426950408889634
VMEM_LIMIT = 48 * 1024 * 1024

_NT = (((1,), (1,)), ((), ()))


def _cparams(sem):
    return pltpu.CompilerParams(dimension_semantics=sem, vmem_limit_bytes=VMEM_LIMIT)


def _sigmoid(x):
    return 1.0 / (1.0 + jnp.exp(-x))


def _silu(x):
    return x * _sigmoid(x)


def _log_sigmoid(x):
    return jnp.minimum(x, 0.0) - jnp.log1p(jnp.exp(-jnp.abs(x)))


def _mm(a, b):
    return jnp.dot(a, b, preferred_element_type=f32)


def _mm_nt(a, b):
    return lax.dot_general(a, b, _NT, preferred_element_type=f32)


def _norm_proj_kernel(*refs, starts):
    n_seg = len(starts) - 1
    x_ref, g_ref = refs[:2]
    w_refs = refs[2:2 + n_seg]
    wg_ref, o_ref, og_ref, h_ref = refs[2 + n_seg:]
    j = pl.program_id(1)

    @pl.when(j == 0)
    def _():
        x = x_ref[...]
        r = lax.rsqrt(jnp.mean(x * x, axis=-1, keepdims=True) + EPS)
        h = ((x * r) * g_ref[...]).astype(bf16)
        h_ref[...] = h
        og_ref[...] = _mm(h, wg_ref[...])

    for k, w_ref in enumerate(w_refs):
        @pl.when((j >= starts[k]) & (j < starts[k + 1]))
        def _(w_ref=w_ref):
            o_ref[...] = _mm(h_ref[...], w_ref[...])


def norm_proj(x, g, ws, wg, tm, tn):
    m, d = x.shape
    ng = wg.shape[1]
    starts = [0]
    for w in ws:
        starts.append(starts[-1] + w.shape[1] // tn)
    n_tiles = starts[-1]

    def seg_spec(k):
        return pl.BlockSpec((d, tn), lambda i, j: (0, jnp.clip(j - starts[k], 0, starts[k + 1] - starts[k] - 1)))

    return pl.pallas_call(
        functools.partial(_norm_proj_kernel, starts=tuple(starts)),
        grid=(m // tm, n_tiles),
        in_specs=[pl.BlockSpec((tm, d), lambda i, j: (i, 0)),
                  pl.BlockSpec((1, d), lambda i, j: (0, 0))]
                 + [seg_spec(k) for k in range(len(ws))]
                 + [pl.BlockSpec((d, ng), lambda i, j: (0, 0))],
        out_specs=[pl.BlockSpec((tm, tn), lambda i, j: (i, j)),
                   pl.BlockSpec((tm, ng), lambda i, j: (i, 0))],
        out_shape=[jax.ShapeDtypeStruct((m, n_tiles * tn), f32), jax.ShapeDtypeStruct((m, ng), f32)],
        scratch_shapes=[pltpu.VMEM((tm, d), bf16)],
        compiler_params=_cparams(("parallel", "arbitrary")),
        name="norm_proj",
    )(x, g.reshape(1, d), *ws, wg)


def _out_proj_kernel(*refs, n_parts, final_norm):
    a_refs = refs[:n_parts]
    w_refs = refs[n_parts:2 * n_parts]
    x_ref = refs[2 * n_parts]
    o_ref = refs[-1]
    acc = x_ref[...]
    for a_ref, w_ref in zip(a_refs, w_refs):
        acc = acc + _mm(a_ref[...], w_ref[...])
    if final_norm:
        fg_ref = refs[2 * n_parts + 1]
        r = lax.rsqrt(jnp.mean(acc * acc, axis=-1, keepdims=True) + EPS)
        acc = (acc * r) * fg_ref[...]
    o_ref[...] = acc


def out_proj(parts, weights, x, final_g, tm):
    m, d = x.shape
    n_parts = len(parts)
    final_norm = final_g is not None
    in_specs = [pl.BlockSpec((tm, a.shape[1]), lambda i: (i, 0)) for a in parts]
    in_specs += [pl.BlockSpec(w.shape, lambda i: (0, 0)) for w in weights]
    in_specs += [pl.BlockSpec((tm, d), lambda i: (i, 0))]
    args = list(parts) + list(weights) + [x]
    if final_norm:
        in_specs += [pl.BlockSpec((1, d), lambda i: (0, 0))]
        args += [final_g.reshape(1, d)]
    return pl.pallas_call(
        functools.partial(_out_proj_kernel, n_parts=n_parts, final_norm=final_norm),
        grid=(m // tm,),
        in_specs=in_specs,
        out_specs=pl.BlockSpec((tm, d), lambda i: (i, 0)),
        out_shape=jax.ShapeDtypeStruct((m, d), f32),
        compiler_params=_cparams(("parallel",)),
        name="out_proj",
    )(*args)


def _pool_kernel(u_ref, z_ref, w_ref, sc_ref, o_ref):
    g = pl.program_id(1)
    x = u_ref[...]
    row = lax.broadcasted_iota(jnp.int32, x.shape, 0)

    def back(a, s):
        return jnp.where(row >= s, pltpu.roll(a, s, axis=0), 0.0)

    s2 = x + back(x, 1)
    s4 = s2 + back(s2, 2)
    s8 = s4 + back(s4, 4)
    s16 = s8 + back(s8, 8)
    win = jnp.where(g == 0, s2, jnp.where(g == 1, s4, jnp.where(g == 2, s8, s16)))
    wlen = lax.shift_left(jnp.int32(2), g)
    cnt = jnp.minimum(row + 1, wlen).astype(f32)
    pooled = win / cnt - x
    y = _mm(pooled.astype(bf16), w_ref[0]) * sc_ref[...]
    o_ref[...] = (y * _silu(z_ref[...])).astype(bf16)


def pool_prompt(proj, w_pool, pool_scale, nb, t):
    ng = len(POOL_WINDOWS)
    return pl.pallas_call(
        _pool_kernel,
        grid=(nb, ng),
        in_specs=[pl.BlockSpec((t, POOL_GROUP_W), lambda b, g: (b, g)),
                  pl.BlockSpec((t, POOL_GROUP_W), lambda b, g: (b, ng + g)),
                  pl.BlockSpec((1, POOL_GROUP_W, POOL_GROUP_W), lambda b, g: (g, 0, 0)),
                  pl.BlockSpec((1, POOL_GROUP_W), lambda b, g: (0, g))],
        out_specs=pl.BlockSpec((t, POOL_GROUP_W), lambda b, g: (b, g)),
        out_shape=jax.ShapeDtypeStruct((nb * t, POOL_W), bf16),
        compiler_params=_cparams(("parallel", "arbitrary")),
        name="pool_prompt",
    )(proj, proj, w_pool, pool_scale.reshape(1, POOL_W))


def _pool_step_kernel(st_ref, u_ref, z_ref, w_ref, sc_ref, o_ref):
    u = u_ref[...]
    for g, wlen in enumerate(POOL_WINDOWS):
        lo = g * POOL_GROUP_W
        ug = u[:, lo:lo + POOL_GROUP_W]
        acc = ug
        for r in range(POOL_STATE + 1 - wlen, POOL_STATE):
            acc = acc + st_ref[:, r * POOL_W + lo:r * POOL_W + lo + POOL_GROUP_W]
        pooled = acc / float(wlen) - ug
        y = _mm(pooled.astype(bf16), w_ref[g]) * sc_ref[:, lo:lo + POOL_GROUP_W]
        o_ref[:, lo:lo + POOL_GROUP_W] = (y * _silu(z_ref[:, lo:lo + POOL_GROUP_W])).astype(bf16)


def pool_step(state_flat, proj, w_pool, pool_scale):
    nb = proj.shape[0]
    return pl.pallas_call(
        _pool_step_kernel,
        grid=(1,),
        in_specs=[pl.BlockSpec(state_flat.shape, lambda i: (0, 0)),
                  pl.BlockSpec((nb, POOL_W), lambda i: (0, 0)),
                  pl.BlockSpec((nb, POOL_W), lambda i: (0, 1)),
                  pl.BlockSpec(w_pool.shape, lambda i: (0, 0, 0)),
                  pl.BlockSpec((1, POOL_W), lambda i: (0, 0))],
        out_specs=pl.BlockSpec((nb, POOL_W), lambda i: (0, 0)),
        out_shape=jax.ShapeDtypeStruct((nb, POOL_W), bf16),
        compiler_params=_cparams(("arbitrary",)),
        name="pool_step",
    )(state_flat, proj, proj, w_pool, pool_scale.reshape(1, POOL_W))


def _head_out(hc, o, z, g):
    hc = hc * _sigmoid(o)
    hc = hc * lax.rsqrt(jnp.mean(hc * hc, axis=-1, keepdims=True) + EPS)
    return ((hc * g) * _silu(z)).astype(bf16)


def _mlstm_kernel(q_ref, k_ref, v_ref, o_ref, z_ref, gi_ref, gf_ref, bi_ref, bf_ref, mhg_ref,
                  y_ref, c_ref, n_ref, m_ref):
    @pl.when(pl.program_id(1) == 0)
    def _():
        c_ref[...] = jnp.zeros_like(c_ref)
        n_ref[...] = jnp.zeros_like(n_ref)
        m_ref[...] = jnp.zeros_like(m_ref)

    ln = q_ref.shape[0]
    gi = gi_ref[...] + bi_ref[...]
    lf = _log_sigmoid(gf_ref[...] + bf_ref[...])
    row = lax.broadcasted_iota(jnp.int32, lf.shape, 0)
    b = lf
    s = 1
    while s < ln:
        b = b + jnp.where(row >= s, pltpu.roll(b, s, axis=0), 0.0)
        s *= 2
    r_t = (gi - b).T
    tt = lax.broadcasted_iota(jnp.int32, (ln, ln), 0)
    ss = lax.broadcasted_iota(jnp.int32, (ln, ln), 1)
    causal = ss <= tt
    lane = lax.broadcasted_iota(jnp.int32, (1, LANES), 1)
    m_vec = m_ref[0]
    for h in range(M_HEADS):
        hs = slice(h * M_HEAD_DIM, (h + 1) * M_HEAD_DIM)
        b_col = b[:, h:h + 1]
        ig_col = gi[:, h:h + 1]
        m_prev = m_vec[:, h:h + 1]
        inter = b_col + m_prev
        dmat = jnp.where(causal, b_col + r_t[h:h + 1, :], -jnp.inf)
        m_t = jnp.maximum(inter, jnp.max(dmat, axis=1, keepdims=True))
        dw = jnp.exp(dmat - m_t)
        iw = jnp.exp(inter - m_t)
        q = q_ref[:, hs]
        k = k_ref[:, hs] * (M_HEAD_DIM ** -0.5)
        v = v_ref[:, hs]
        qb, kb, vb = q.astype(bf16), k.astype(bf16), v.astype(bf16)
        c = c_ref[0, h]
        n = n_ref[0, h:h + 1, :]
        qk = _mm_nt(qb, kb) * dw
        num = iw * _mm(qb, c.astype(bf16)) + _mm(qk.astype(bf16), vb)
        den = iw * jnp.sum(q * n, axis=1, keepdims=True) + jnp.sum(qk, axis=1, keepdims=True)
        hc = num / jnp.maximum(jnp.abs(den), jnp.exp(-m_t))
        y_ref[:, hs] = _head_out(hc, o_ref[:, hs], z_ref[:, hs], mhg_ref[:, hs])
        m_last = m_t[ln - 1:ln, :]
        b_last = b_col[ln - 1:ln, :]
        ws = jnp.exp(b_last - b_col + ig_col - m_last)
        dec = jnp.exp(b_last + m_prev - m_last)
        kw = ws * k
        c_ref[0, h] = dec * c + _mm(kw.T.astype(bf16), vb)
        n_ref[0, h:h + 1, :] = dec * n + jnp.sum(kw, axis=0, keepdims=True)
        m_vec = jnp.where(lane == h, m_last, m_vec)
    m_ref[0] = m_vec


def mlstm_prompt(proj, gates, bias_i, bias_f, mh_norm_g, nb, t, ln):
    nc = t // ln
    col = lambda cb: pl.BlockSpec((ln, M_W), lambda b, c: (b * nc + c, cb))
    gcol = lambda cb: pl.BlockSpec((ln, LANES), lambda b, c: (b * nc + c, cb))
    vec = lambda w: pl.BlockSpec((1, w), lambda b, c: (0, 0))
    return pl.pallas_call(
        _mlstm_kernel,
        grid=(nb, nc),
        in_specs=[col(2), col(3), col(4), col(5), col(6), gcol(0), gcol(1), vec(LANES), vec(LANES), vec(M_W)],
        out_specs=[pl.BlockSpec((ln, M_W), lambda b, c: (b * nc + c, 0)),
                   pl.BlockSpec((1, M_HEADS, M_HEAD_DIM, M_HEAD_DIM), lambda b, c: (b, 0, 0, 0)),
                   pl.BlockSpec((1, M_HEADS, M_HEAD_DIM), lambda b, c: (b, 0, 0)),
                   pl.BlockSpec((1, 1, LANES), lambda b, c: (b, 0, 0))],
        out_shape=[jax.ShapeDtypeStruct((nb * t, M_W), bf16),
                   jax.ShapeDtypeStruct((nb, M_HEADS, M_HEAD_DIM, M_HEAD_DIM), f32),
                   jax.ShapeDtypeStruct((nb, M_HEADS, M_HEAD_DIM), f32),
                   jax.ShapeDtypeStruct((nb, 1, LANES), f32)],
        compiler_params=_cparams(("parallel", "arbitrary")),
        name="mlstm_prompt",
    )(proj, proj, proj, proj, proj, gates, gates, bias_i, bias_f, mh_norm_g.reshape(1, M_W))


def _mlstm_step_kernel(q_ref, k_ref, v_ref, o_ref, z_ref, gi_ref, gf_ref, bi_ref, bf_ref, mhg_ref,
                       c_ref, n_ref, m_ref, y_ref, co_ref, no_ref, mo_ref):
    nb = q_ref.shape[0]
    gi = gi_ref[...] + bi_ref[...]
    lf = _log_sigmoid(gf_ref[...] + bf_ref[...])
    inter = lf + m_ref[...]
    m_t = jnp.maximum(inter, gi)
    dw_all = jnp.exp(gi - m_t)
    iw_all = jnp.exp(inter - m_t)
    em_all = jnp.exp(-m_t)
    mo_ref[...] = m_t
    d0 = lax.broadcasted_iota(jnp.int32, (M_HEAD_DIM, M_HEAD_DIM), 0)
    d1 = lax.broadcasted_iota(jnp.int32, (M_HEAD_DIM, M_HEAD_DIM), 1)
    eye = d0 == d1
    for j in range(nb):
        for h in range(M_HEADS):
            hs = slice(h * M_HEAD_DIM, (h + 1) * M_HEAD_DIM)
            dw = dw_all[j:j + 1, h:h + 1]
            iw = iw_all[j:j + 1, h:h + 1]
            em = em_all[j:j + 1, h:h + 1]
            q = q_ref[j:j + 1, hs]
            k = k_ref[j:j + 1, hs] * (M_HEAD_DIM ** -0.5)
            v = v_ref[j:j + 1, hs]
            c = c_ref[j, h]
            n = n_ref[j, h:h + 1, :]
            qc = _mm(jnp.broadcast_to(q, (8, M_HEAD_DIM)).astype(bf16), c.astype(bf16))[0:1, :]
            qk = jnp.sum(q * k, axis=1, keepdims=True) * dw
            num = iw * qc + qk * v
            den = iw * jnp.sum(q * n, axis=1, keepdims=True) + qk
            hc = num / jnp.maximum(jnp.abs(den), em)
            y_ref[j:j + 1, hs] = _head_out(hc, o_ref[j:j + 1, hs], z_ref[j:j + 1, hs], mhg_ref[:, hs])
            kdiag = jnp.where(eye, jnp.broadcast_to(k, (M_HEAD_DIM, M_HEAD_DIM)), 0.0).astype(bf16)
            vrep = jnp.broadcast_to(v, (M_HEAD_DIM, M_HEAD_DIM)).astype(bf16)
            co_ref[j, h] = iw * c + dw * _mm(kdiag, vrep)
            no_ref[j, h:h + 1, :] = iw * n + dw * k


def mlstm_step(proj, gates, bias_i, bias_f, mh_norm_g, c0, n0, m0_pad, bb):
    nb = proj.shape[0]
    col = lambda cb: pl.BlockSpec((bb, M_W), lambda i: (i, cb))
    gcol = lambda cb: pl.BlockSpec((bb, LANES), lambda i: (i, cb))
    vec = lambda w: pl.BlockSpec((1, w), lambda i: (0, 0))
    cspec = pl.BlockSpec((bb, M_HEADS, M_HEAD_DIM, M_HEAD_DIM), lambda i: (i, 0, 0, 0))
    nspec = pl.BlockSpec((bb, M_HEADS, M_HEAD_DIM), lambda i: (i, 0, 0))
    return pl.pallas_call(
        _mlstm_step_kernel,
        grid=(nb // bb,),
        in_specs=[col(2), col(3), col(4), col(5), col(6), gcol(0), gcol(1), vec(LANES), vec(LANES), vec(M_W),
                  cspec, nspec, gcol(0)],
        out_specs=[pl.BlockSpec((bb, M_W), lambda i: (i, 0)), cspec, nspec, gcol(0)],
        out_shape=[jax.ShapeDtypeStruct((nb, M_W), bf16),
                   jax.ShapeDtypeStruct(c0.shape, f32),
                   jax.ShapeDtypeStruct(n0.shape, f32),
                   jax.ShapeDtypeStruct((nb, LANES), f32)],
        compiler_params=_cparams(("parallel",)),
        name="mlstm_step",
    )(proj, proj, proj, proj, proj, gates, gates, bias_i, bias_f, mh_norm_g.reshape(1, M_W), c0, n0, m0_pad)


def _rope(x, cos, sin_signed):
    return x * cos + pltpu.roll(x, HEAD_DIM // 2, axis=1) * sin_signed


def _nsa_prep_kernel(*refs, prompt):
    q_ref, kvc_ref, kvs_ref, kvw_ref, cos_ref, sin_ref, qr_ref = refs[:7]
    tq = q_ref.shape[0]
    cos, sin = cos_ref[...], sin_ref[...]
    for h in range(N_HEADS):
        hs = slice(h * HEAD_DIM, (h + 1) * HEAD_DIM)
        qr_ref[:, hs] = _rope(q_ref[:, hs], cos, sin).astype(bf16)

    def chunks(src, rotate):
        for c in range(KV_ROWS):
            x = src[:, c * HEAD_DIM:(c + 1) * HEAD_DIM]
            yield c, (_rope(x, cos, sin) if rotate and c < N_KV else x)

    if prompt:
        kvc_i, kvs_i, kvw_i, blk_ref, kvs_b, kvw_b = refs[7:13]
        for c, x in chunks(kvc_ref, False):
            kvc_i[pl.ds(c, tq, stride=KV_ROWS), :] = x
        for src, dst_i, dst_b in ((kvs_ref, kvs_i, kvs_b), (kvw_ref, kvw_i, kvw_b)):
            for c, x in chunks(src, True):
                dst_i[pl.ds(c, tq, stride=KV_ROWS), :] = x
                dst_b[:, c * HEAD_DIM:(c + 1) * HEAD_DIM] = x.astype(bf16)
        nblk = tq // CMP_BLOCK
        means = jnp.sum(kvc_ref[...].reshape(nblk, CMP_BLOCK, KV_W), axis=1) * (1.0 / CMP_BLOCK)
        for c in range(KV_ROWS):
            blk_ref[pl.ds(c, nblk, stride=KV_ROWS), :] = means[:, c * HEAD_DIM:(c + 1) * HEAD_DIM]
    else:
        for src, dst in ((kvs_ref, refs[7]), (kvw_ref, refs[8])):
            for c, x in chunks(src, True):
                dst[:, c * HEAD_DIM:(c + 1) * HEAD_DIM] = x


def nsa_prep(proj, cos, sin, tq, n_pos_blocks, prompt):
    m = proj.shape[0]
    row = lambda w, cb: pl.BlockSpec((tq, w), lambda i: (i, cb))
    tab = pl.BlockSpec((tq, HEAD_DIM), lambda i: (i % n_pos_blocks, 0))
    if prompt:
        inter = pl.BlockSpec((tq * KV_ROWS, HEAD_DIM), lambda i: (i, 0))
        inter_shape = jax.ShapeDtypeStruct((m * KV_ROWS, HEAD_DIM), f32)
        out_specs = [row(NSA_W, 0), inter, inter, inter,
                     pl.BlockSpec((tq // CMP_BLOCK * KV_ROWS, HEAD_DIM), lambda i: (i, 0)), row(KV_W, 0), row(KV_W, 0)]
        out_shape = [jax.ShapeDtypeStruct((m, NSA_W), bf16), inter_shape, inter_shape, inter_shape,
                     jax.ShapeDtypeStruct((m // CMP_BLOCK * KV_ROWS, HEAD_DIM), f32),
                     jax.ShapeDtypeStruct((m, KV_W), bf16), jax.ShapeDtypeStruct((m, KV_W), bf16)]
    else:
        out_specs = [row(NSA_W, 0), row(KV_W, 0), row(KV_W, 0)]
        out_shape = [jax.ShapeDtypeStruct((m, NSA_W), bf16), jax.ShapeDtypeStruct((m, KV_W), f32),
                     jax.ShapeDtypeStruct((m, KV_W), f32)]
    return pl.pallas_call(
        functools.partial(_nsa_prep_kernel, prompt=prompt),
        grid=(m // tq,),
        in_specs=[row(NSA_W, 0), row(KV_W, 4), row(KV_W, 5), row(KV_W, 6), tab, tab],
        out_specs=out_specs,
        out_shape=out_shape,
        compiler_params=_cparams(("parallel",)),
        name="nsa_prep",
    )(proj, proj, proj, proj, cos, sin)


def _select_blocks(imp, q_pos, n_cand):
    lane = lax.broadcasted_iota(jnp.int32, imp.shape, 1)
    pair = imp + pltpu.roll(imp, LANES - 1, axis=1)
    cur2 = lax.shift_left(lax.shift_right_logical(q_pos, 6), 1)
    valid = ((lane & 1) == 0) & (lane <= cur2)
    v = jnp.where(lane == cur2, jnp.inf, pair)
    v = jnp.where(valid, v, -jnp.inf)
    cnt = jnp.zeros(imp.shape, f32)
    for i in range(n_cand):
        vi = v[:, 2 * i:2 * i + 1]
        before = jnp.where(lane > 2 * i, 1.0, 0.0)
        cnt = cnt + jnp.where(vi > v, 1.0, 0.0) + jnp.where(vi == v, before, 0.0)
    return jnp.where(valid & (cnt < SEL_TOPK), 1.0, 0.0)


def _pad_rows(x, rows):
    return jnp.concatenate([x, jnp.zeros((rows - x.shape[0], x.shape[1]), x.dtype)], axis=0)


def _masked_softmax(s, mask):
    s = jnp.where(mask, s, -jnp.inf)
    m = jnp.max(s, axis=-1, keepdims=True)
    m = jnp.where(m > -jnp.inf, m, 0.0)
    p = jnp.exp(s - m)
    return p / jnp.maximum(jnp.sum(p, axis=-1, keepdims=True), 1e-30)


def _select_block_rows(imp, q_pos, n_cand):
    row = lax.broadcasted_iota(jnp.int32, imp.shape, 0)
    pair = imp + pltpu.roll(imp, imp.shape[0] - 1, axis=0)
    cur2 = lax.shift_left(lax.shift_right_logical(q_pos, 6), 1)
    valid = ((row & 1) == 0) & (row <= cur2)
    v = jnp.where(row == cur2, jnp.inf, pair)
    v = jnp.where(valid, v, -jnp.inf)
    cnt = jnp.zeros(imp.shape, f32)
    for i in range(n_cand):
        vi = v[2 * i:2 * i + 1, :]
        before = jnp.where(row > 2 * i, 1.0, 0.0)
        cnt = cnt + jnp.where(vi > v, 1.0, 0.0) + jnp.where(vi == v, before, 0.0)
    return jnp.where(valid & (cnt < SEL_TOPK), 1.0, 0.0)


def _cmp_prompt_kernel(q_ref, blk_ref, oc_ref, sel_ref, *, n_blocks):
    tq = q_ref.shape[0]
    t0 = pl.program_id(1) * tq
    row = lax.broadcasted_iota(jnp.int32, (LANES, tq), 0)
    q_pos = lax.broadcasted_iota(jnp.int32, (LANES, tq), 1) + t0
    vis = (row < n_blocks) & ((row + 1) * CMP_BLOCK - 1 <= q_pos)

    def block_rows(c):
        return _pad_rows(blk_ref[pl.ds(c, n_blocks, stride=KV_ROWS), :], LANES).astype(bf16)

    for g in range(N_KV):
        kg = block_rows(g)
        vg = block_rows(N_KV + g)
        imp = jnp.zeros((LANES, tq), f32)
        for r in range(Q_PER_KV):
            hs = slice((g * Q_PER_KV + r) * HEAD_DIM, (g * Q_PER_KV + r + 1) * HEAD_DIM)
            s = jnp.where(vis, _mm_nt(kg, q_ref[:, hs].astype(bf16)) * ATT_SCALE, -jnp.inf)
            m = jnp.max(s, axis=0, keepdims=True)
            m = jnp.where(m > -jnp.inf, m, 0.0)
            p = jnp.exp(s - m)
            p = p / jnp.maximum(jnp.sum(p, axis=0, keepdims=True), 1e-30)
            oc_ref[:, hs] = _mm(p.T.astype(bf16), vg)
            imp = imp + p
        flags = _select_block_rows(imp, q_pos, n_blocks // 2)
        sel_ref[:, g * LANES:(g + 1) * LANES] = flags.T.astype(bf16)


def cmp_prompt(proj, blocks, nb, t, tq):
    nq = t // tq
    n_blocks = t // CMP_BLOCK
    return pl.pallas_call(
        functools.partial(_cmp_prompt_kernel, n_blocks=n_blocks),
        grid=(nb, nq),
        in_specs=[pl.BlockSpec((tq, NSA_W), lambda b, i: (b * nq + i, 0)),
                  pl.BlockSpec((n_blocks * KV_ROWS, HEAD_DIM), lambda b, i: (b, 0))],
        out_specs=[pl.BlockSpec((tq, NSA_W), lambda b, i: (b * nq + i, 0)),
                   pl.BlockSpec((tq, N_KV * LANES), lambda b, i: (b * nq + i, 0))],
        out_shape=[jax.ShapeDtypeStruct((nb * t, NSA_W), f32),
                   jax.ShapeDtypeStruct((nb * t, N_KV * LANES), bf16)],
        compiler_params=_cparams(("parallel", "parallel")),
        name="cmp_prompt",
    )(proj, blocks)


def _attn_kernel(*refs, mode, tile, reach):
    if mode == "sel":
        q_ref, k_ref, v_ref, sel_ref, et_ref, o_ref, qp_sc, s_sc, mx_sc, mb_sc, acc_sc = refs
    else:
        q_ref, k_ref, v_ref, o_ref, qp_sc, s_sc, mx_sc, mb_sc, acc_sc = refs
    qi = pl.program_id(2)
    rows = Q_PER_KV * tile
    for r in range(Q_PER_KV):
        qh = q_ref[:, r * HEAD_DIM:(r + 1) * HEAD_DIM]
        if mode == "sel":
            bias = ((1.0 - sel_ref[...].astype(f32)) * MASK_BIAS).astype(bf16)
            qh = jnp.concatenate([qh, bias], axis=1)
        qp_sc[r * tile:(r + 1) * tile, :] = qh
    lo = 0 if mode == "sel" else jnp.maximum(qi - reach, 0)
    t_in = lax.broadcasted_iota(jnp.int32, (rows, tile), 0) & (tile - 1)
    col = lax.broadcasted_iota(jnp.int32, (rows, tile), 1)

    def keys(kj):
        start = pl.multiple_of(kj * tile, tile)
        kt = k_ref[pl.ds(start, tile), :]
        if mode == "sel":
            kt = jnp.concatenate([kt, et_ref[pl.ds(start, tile), :]], axis=1)
        return kt

    def store_scores(kj, s):
        s_sc[kj] = s
        mx = mx_sc[...]
        for c in range(tile // LANES):
            mx = jnp.maximum(mx, s[:, c * LANES:(c + 1) * LANES])
        mx_sc[...] = mx

    mx_sc[...] = jnp.full_like(mx_sc, MASK_BIAS)

    def pass1(kj, carry):
        s = _mm_nt(qp_sc[...], keys(kj))
        if mode == "win":
            s = jnp.where(t_in - col + (qi - kj) * tile <= WINDOW, s, MASK_BIAS)
        store_scores(kj, s)
        return carry

    lax.fori_loop(lo, qi, pass1, 0)
    store_scores(qi, jnp.where(col <= t_in, _mm_nt(qp_sc[...], keys(qi)), MASK_BIAS))

    m = jnp.max(mx_sc[...], axis=1, keepdims=True)
    mb_sc[...] = jnp.broadcast_to(m, mb_sc.shape)
    acc_sc[...] = jnp.zeros_like(acc_sc)
    ones = jnp.ones((tile, HEAD_DIM), bf16)

    def pass2(kj, carry):
        start = pl.multiple_of(kj * tile, tile)
        s = s_sc[kj]
        mb = mb_sc[...]
        p = jnp.concatenate([jnp.exp2((s[:, c * LANES:(c + 1) * LANES] - mb) * (ATT_SCALE * LOG2E))
                             for c in range(tile // LANES)], axis=1)
        vt = jnp.concatenate([v_ref[pl.ds(start, tile), :], ones], axis=1)
        acc_sc[...] += _mm(p.astype(bf16), vt)
        return carry

    lax.fori_loop(lo, qi + 1, pass2, 0)
    acc = acc_sc[...]
    o = acc[:, :HEAD_DIM] / acc[:, HEAD_DIM:]
    for r in range(Q_PER_KV):
        o_ref[:, r * HEAD_DIM:(r + 1) * HEAD_DIM] = o[r * tile:(r + 1) * tile, :]


def attn_prompt(q_rot, kv_b, sel, expand_t, nb, t, tile, mode):
    nq = t // tile
    reach = -(-WINDOW // tile)
    qw = Q_PER_KV * HEAD_DIM
    kdim = 2 * HEAD_DIM if mode == "sel" else HEAD_DIM
    rows = Q_PER_KV * tile
    in_specs = [pl.BlockSpec((tile, qw), lambda b, g, i: (b * nq + i, g)),
                pl.BlockSpec((t, HEAD_DIM), lambda b, g, i: (b, g)),
                pl.BlockSpec((t, HEAD_DIM), lambda b, g, i: (b, N_KV + g))]
    args = [q_rot, kv_b, kv_b]
    if mode == "sel":
        in_specs += [pl.BlockSpec((tile, LANES), lambda b, g, i: (b * nq + i, g)),
                     pl.BlockSpec((t, LANES), lambda b, g, i: (0, 0))]
        args += [sel, expand_t]
    return pl.pallas_call(
        functools.partial(_attn_kernel, mode=mode, tile=tile, reach=reach),
        grid=(nb, N_KV, nq),
        in_specs=in_specs,
        out_specs=pl.BlockSpec((tile, qw), lambda b, g, i: (b * nq + i, g)),
        out_shape=jax.ShapeDtypeStruct((nb * t, NSA_W), f32),
        scratch_shapes=[pltpu.VMEM((rows, kdim), bf16),
                        pltpu.VMEM((nq, rows, tile), f32),
                        pltpu.VMEM((rows, LANES), f32),
                        pltpu.VMEM((rows, LANES), f32),
                        pltpu.VMEM((rows, 2 * HEAD_DIM), f32)],
        compiler_params=_cparams(("parallel", "parallel", "arbitrary")),
        name="attn_" + mode,
    )(*args)


def _combine_kernel(oc_ref, os_ref, ow_ref, g_ref, b_ref, z_ref, o_ref):
    gate = _sigmoid(g_ref[...] + b_ref[...])
    for h in range(N_HEADS):
        hs = slice(h * HEAD_DIM, (h + 1) * HEAD_DIM)
        o = (gate[:, 3 * h:3 * h + 1] * oc_ref[:, hs] + gate[:, 3 * h + 1:3 * h + 2] * os_ref[:, hs]
             + gate[:, 3 * h + 2:3 * h + 3] * ow_ref[:, hs])
        o_ref[:, hs] = (o * _silu(z_ref[:, hs])).astype(bf16)


def combine(o_c, o_s, o_w, gates, bias, proj, tm):
    m = o_c.shape[0]
    row = lambda w, cb: pl.BlockSpec((tm, w), lambda i: (i, cb))
    return pl.pallas_call(
        _combine_kernel,
        grid=(m // tm,),
        in_specs=[row(NSA_W, 0), row(NSA_W, 0), row(NSA_W, 0), row(LANES, 0),
                  pl.BlockSpec((1, LANES), lambda i: (0, 0)),
                  pl.BlockSpec((tm, NSA_W), lambda i: (i, 1))],
        out_specs=row(NSA_W, 0),
        out_shape=jax.ShapeDtypeStruct((m, NSA_W), bf16),
        compiler_params=_cparams(("parallel",)),
        name="nsa_combine",
    )(o_c, o_s, o_w, gates, bias, proj)


def _head_group(shape):
    return lax.shift_right_logical(lax.broadcasted_iota(jnp.int32, shape, 0), 2)


def _cmp_step_kernel(pt_ref, q_ref, *refs, n_pages, q_pos):
    pages = refs[:n_pages]
    oc_ref, sel_ref, blk_sc = refs[n_pages:]
    per_page = PAGE_SIZE // CMP_BLOCK
    for p in range(n_pages):
        x = pages[p][0].reshape(per_page, CMP_BLOCK, KV_ROWS, HEAD_DIM)
        means = jnp.sum(x, axis=1) * (1.0 / CMP_BLOCK)
        blk_sc[p * per_page * KV_ROWS:(p + 1) * per_page * KV_ROWS, :] = means.reshape(per_page * KV_ROWS, HEAD_DIM)
    n_blocks = blk_sc.shape[0] // KV_ROWS
    qb = q_ref[0].astype(bf16)
    grp = _head_group((N_HEADS, LANES))
    lane = lax.broadcasted_iota(jnp.int32, (N_HEADS, LANES), 1)
    s = jnp.zeros((N_HEADS, LANES), f32)
    for g in range(N_KV):
        kg = _pad_rows(blk_sc[pl.ds(g, n_blocks, stride=KV_ROWS), :], LANES).astype(bf16)
        s = jnp.where(grp == g, _mm_nt(qb, kg), s)
    vis = (lane < n_blocks) & ((lane + 1) * CMP_BLOCK - 1 <= q_pos)
    p = _masked_softmax(s * ATT_SCALE, vis)
    pb = p.astype(bf16)
    o = jnp.zeros((N_HEADS, HEAD_DIM), f32)
    for g in range(N_KV):
        vg = _pad_rows(blk_sc[pl.ds(N_KV + g, n_blocks, stride=KV_ROWS), :], LANES).astype(bf16)
        o = jnp.where(grp == g, _mm(pb, vg), o)
    oc_ref[0] = o
    row8 = lax.broadcasted_iota(jnp.int32, (8, LANES), 0)
    imp = jnp.zeros((8, LANES), f32)
    for g in range(N_KV):
        imp_g = jnp.sum(p[g * Q_PER_KV:(g + 1) * Q_PER_KV, :], axis=0, keepdims=True)
        imp = jnp.where(row8 == g, imp_g, imp)
    sel_ref[0] = _select_blocks(imp, jnp.full((8, LANES), q_pos, jnp.int32), q_pos // SEL_BLOCK + 1)


def _page_specs(n_pages):
    return [pl.BlockSpec((1, PAGE_SIZE * KV_ROWS, HEAD_DIM), lambda b, pt, p=p: (pt[b * n_pages + p], 0, 0))
            for p in range(n_pages)]


def cmp_step(q3, cache, pt_flat, n_pages, q_pos):
    nb = q3.shape[0]
    head_blk = pl.BlockSpec((1, N_HEADS, HEAD_DIM), lambda b, pt: (b, 0, 0))
    return pl.pallas_call(
        functools.partial(_cmp_step_kernel, n_pages=n_pages, q_pos=q_pos),
        grid_spec=pltpu.PrefetchScalarGridSpec(
            num_scalar_prefetch=1,
            grid=(nb,),
            in_specs=[head_blk] + _page_specs(n_pages),
            out_specs=[head_blk, pl.BlockSpec((1, 8, LANES), lambda b, pt: (b, 0, 0))],
            scratch_shapes=[pltpu.VMEM((n_pages * PAGE_SIZE // CMP_BLOCK * KV_ROWS, HEAD_DIM), f32)]),
        out_shape=[jax.ShapeDtypeStruct((nb, N_HEADS, HEAD_DIM), f32),
                   jax.ShapeDtypeStruct((nb, 8, LANES), f32)],
        compiler_params=_cparams(("parallel",)),
        name="cmp_step",
    )(pt_flat, q3, *([cache] * n_pages))


def _decode_attend(qb, n_tiles, kv_tile, flags, new_row, s_sc):
    grp = _head_group((N_HEADS, LANES))
    lane = lax.broadcasted_iota(jnp.int32, (N_HEADS, LANES), 1)
    for p in range(n_tiles):
        sp = jnp.zeros((N_HEADS, LANES), f32)
        for g in range(N_KV):
            sg = _mm_nt(qb, kv_tile(p, g).astype(bf16)) * ATT_SCALE
            if flags is not None:
                f0 = flags[g:g + 1, 4 * p:4 * p + 1]
                f1 = flags[g:g + 1, 4 * p + 2:4 * p + 3]
                sg = jnp.where(jnp.where(lane < SEL_BLOCK, f0, f1) > 0.5, sg, NEG)
            sp = jnp.where(grp == g, sg, sp)
        s_sc[:, p * LANES:(p + 1) * LANES] = sp
    qf = qb.astype(f32)
    s_new = jnp.zeros((N_HEADS, 1), f32)
    grp1 = _head_group((N_HEADS, 1))
    for g in range(N_KV):
        kn = new_row[:, g * HEAD_DIM:(g + 1) * HEAD_DIM].astype(bf16).astype(f32)
        s_new = jnp.where(grp1 == g, jnp.sum(qf * kn, axis=1, keepdims=True) * ATT_SCALE, s_new)
    s_all = s_sc[...]
    m = jnp.maximum(jnp.max(s_all, axis=1, keepdims=True), s_new)
    p_all = jnp.exp(s_all - m)
    p_new = jnp.exp(s_new - m)
    den = jnp.sum(p_all, axis=1, keepdims=True) + p_new
    o = jnp.zeros((N_HEADS, HEAD_DIM), f32)
    for g in range(N_KV):
        vn = new_row[:, KV_W // 2 + g * HEAD_DIM:KV_W // 2 + (g + 1) * HEAD_DIM].astype(bf16).astype(f32)
        o = jnp.where(grp == g, p_new * vn, o)
    for p in range(n_tiles):
        pb = p_all[:, p * LANES:(p + 1) * LANES].astype(bf16)
        for g in range(N_KV):
            o = o + jnp.where(grp == g, _mm(pb, kv_tile(p, N_KV + g).astype(bf16)), 0.0)
    return o / den


def _sel_step_kernel(pt_ref, q_ref, sel_ref, new_ref, *refs, n_pages):
    pages = refs[:n_pages]
    o_ref, s_sc = refs[n_pages:]
    kv_tile = lambda p, c: pages[p][0, pl.ds(c, PAGE_SIZE, stride=KV_ROWS), :]
    o_ref[0] = _decode_attend(q_ref[0], n_pages, kv_tile, sel_ref[0], new_ref[0], s_sc)


def sel_step(q3, sel, new_rows, cache, pt_flat, n_pages):
    nb = q3.shape[0]
    head_blk = pl.BlockSpec((1, N_HEADS, HEAD_DIM), lambda b, pt: (b, 0, 0))
    return pl.pallas_call(
        functools.partial(_sel_step_kernel, n_pages=n_pages),
        grid_spec=pltpu.PrefetchScalarGridSpec(
            num_scalar_prefetch=1,
            grid=(nb,),
            in_specs=[head_blk, pl.BlockSpec((1, 8, LANES), lambda b, pt: (b, 0, 0)),
                      pl.BlockSpec((1, 1, KV_W), lambda b, pt: (b, 0, 0))] + _page_specs(n_pages),
            out_specs=head_blk,
            scratch_shapes=[pltpu.VMEM((N_HEADS, n_pages * PAGE_SIZE), f32)]),
        out_shape=jax.ShapeDtypeStruct((nb, N_HEADS, HEAD_DIM), f32),
        compiler_params=_cparams(("parallel",)),
        name="sel_step",
    )(pt_flat, q3, sel, new_rows, *([cache] * n_pages))


def _win_step_kernel(q_ref, new_ref, win_ref, o_ref, s_sc):
    n_tiles = win_ref.shape[1] // (LANES * KV_ROWS)
    kv_tile = lambda p, c: win_ref[0, pl.ds(p * LANES * KV_ROWS + c, LANES, stride=KV_ROWS), :]
    o_ref[0] = _decode_attend(q_ref[0], n_tiles, kv_tile, None, new_ref[0], s_sc)


def win_step(q3, new_rows, win):
    nb, wrows = win.shape[0], win.shape[1]
    head_blk = pl.BlockSpec((1, N_HEADS, HEAD_DIM), lambda b: (b, 0, 0))
    return pl.pallas_call(
        _win_step_kernel,
        grid=(nb,),
        in_specs=[head_blk, pl.BlockSpec((1, 1, KV_W), lambda b: (b, 0, 0)),
                  pl.BlockSpec((1, wrows, HEAD_DIM), lambda b: (b, 0, 0))],
        out_specs=head_blk,
        out_shape=jax.ShapeDtypeStruct((nb, N_HEADS, HEAD_DIM), f32),
        scratch_shapes=[pltpu.VMEM((N_HEADS, wrows // KV_ROWS), f32)],
        compiler_params=_cparams(("parallel",)),
        name="win_step",
    )(q3, new_rows, win)


def _rope_tables(pos):
    half = HEAD_DIM // 2
    inv = ROPE_THETA ** (-jnp.arange(half, dtype=f32) / half)
    ang = pos.astype(f32)[:, None] * inv[None, :]
    cos, sin = jnp.cos(ang), jnp.sin(ang)
    return jnp.concatenate([cos, cos], axis=1), jnp.concatenate([-sin, sin], axis=1)


def _pad_cols(a, width):
    return jnp.pad(a, ((0, 0), (0, width - a.shape[1])))


def kernel(x_prompt, x_sample, state_pool, state_mlstm_c, state_mlstm_n, state_mlstm_m, cache_kv_cmp, cache_kv_sel, cache_kv_win, page_table, norm0_g, w_in0, b_gate0, w_pool, pool_scale, mh_norm_g, w_out0, norm1_g, w_in1, b_gate1, w_out1, final_g):
    nbp, t, d = x_prompt.shape
    nbs = x_sample.shape[0]
    mp = nbp * t
    n_pages = page_table.shape[1]
    past_len = n_pages * PAGE_SIZE
    wbuf = cache_kv_win.shape[1]

    w0 = [w_in0[:, :MAIN_W].astype(bf16)]
    wg0 = jnp.concatenate([_pad_cols(w_in0[:, MAIN_W:MAIN_W + M_HEADS], LANES),
                           _pad_cols(w_in0[:, MAIN_W + M_HEADS:], LANES)], axis=1).astype(bf16)
    bias_i = _pad_cols(b_gate0[None, :M_HEADS], LANES)
    bias_f = _pad_cols(b_gate0[None, M_HEADS:], LANES)
    g_lo = NSA_W + 3 * KV_W
    g_hi = g_lo + 3 * N_HEADS
    w1 = [w_in1[:, :NSA_W].astype(bf16), w_in1[:, g_hi:].astype(bf16), w_in1[:, NSA_W:g_lo].astype(bf16)]
    wg1 = _pad_cols(w_in1[:, g_lo:g_hi], LANES).astype(bf16)
    bias1 = _pad_cols(b_gate1[None, :], LANES)
    w_pool_b = w_pool.astype(bf16)
    wo0_pool = w_out0[:POOL_W].astype(bf16)
    wo0_m = w_out0[POOL_W:].astype(bf16)
    wo1 = w_out1.astype(bf16)

    xp = x_prompt.reshape(mp, d)
    xs = x_sample.reshape(nbs, d)

    proj_p, gates_p = norm_proj(xp, norm0_g, w0, wg0, 1024, 512)
    proj_s, gates_s = norm_proj(xs, norm0_g, w0, wg0, nbs, 512)

    ypool_p = pool_prompt(proj_p, w_pool_b, pool_scale, nbp, t)
    ym_p, c_p, n_p, m_p = mlstm_prompt(proj_p, gates_p, bias_i, bias_f, mh_norm_g, nbp, t, 256)
    xp1 = out_proj([ypool_p, ym_p], [wo0_pool, wo0_m], xp, None, 512)

    ypool_s = pool_step(state_pool.reshape(nbs, POOL_STATE * POOL_W), proj_s, w_pool_b, pool_scale)
    m0_pad = _pad_cols(state_mlstm_m, LANES)
    ym_s, c_s, n_s, m_s = mlstm_step(proj_s, gates_s, bias_i, bias_f, mh_norm_g,
                                     state_mlstm_c, state_mlstm_n, m0_pad, 8)
    xs1 = out_proj([ypool_s, ym_s], [wo0_pool, wo0_m], xs, None, nbs)

    pool_p = proj_p.reshape(nbp, t, MAIN_W)[:, t - POOL_STATE:, :POOL_W]
    pool_s = jnp.concatenate([state_pool[:, 1:], proj_s[:, None, :POOL_W]], axis=1)

    proj1_p, gates1_p = norm_proj(xp1, norm1_g, w1, wg1, 1024, 512)
    proj1_s, gates1_s = norm_proj(xs1, norm1_g, w1, wg1, nbs, 512)

    tq = 256
    cos_p, sin_p = _rope_tables(jnp.arange(t))
    qrot_p, kvc_p, kvs_p, kvw_p, blocks_p, kvs_pb, kvw_pb = nsa_prep(proj1_p, cos_p, sin_p, tq, t // tq, True)
    oc_p, sel_p = cmp_prompt(proj1_p, blocks_p, nbp, t, tq)
    expand_t = (jnp.arange(LANES)[None, :] == 2 * (jnp.arange(t)[:, None] // SEL_BLOCK)).astype(bf16)
    os_p = attn_prompt(qrot_p, kvs_pb, sel_p, expand_t, nbp, t, tq, "sel")
    ow_p = attn_prompt(qrot_p, kvw_pb, None, None, nbp, t, tq, "win")
    a1_p = combine(oc_p, os_p, ow_p, gates1_p, bias1, proj1_p, 256)
    y_p = out_proj([a1_p], [wo1], xp1, final_g, 512)

    cos_s, sin_s = _rope_tables(jnp.full((nbs,), past_len))
    qrot_s, kvs_s, kvw_s = nsa_prep(proj1_s, cos_s, sin_s, nbs, 1, False)
    pt_flat = page_table.reshape(-1)
    q3_s = proj1_s[:, :NSA_W].reshape(nbs, N_HEADS, HEAD_DIM)
    qrot3_s = qrot_s.reshape(nbs, N_HEADS, HEAD_DIM)
    n_pool = cache_kv_cmp.shape[0]
    page_rows = PAGE_SIZE * KV_ROWS
    oc_s, sel_s = cmp_step(q3_s, cache_kv_cmp.reshape(n_pool, page_rows, HEAD_DIM), pt_flat, n_pages, past_len)
    os_s = sel_step(qrot3_s, sel_s, kvs_s.reshape(nbs, 1, KV_W),
                    cache_kv_sel.reshape(n_pool, page_rows, HEAD_DIM), pt_flat, n_pages)
    ow_s = win_step(qrot3_s, kvw_s.reshape(nbs, 1, KV_W), cache_kv_win.reshape(nbs, wbuf * KV_ROWS, HEAD_DIM))
    a1_s = combine(oc_s.reshape(nbs, NSA_W), os_s.reshape(nbs, NSA_W), ow_s.reshape(nbs, NSA_W),
                   gates1_s, bias1, proj1_s, nbs)
    y_s = out_proj([a1_s], [wo1], xs1, final_g, nbs)

    kv5 = lambda a, rows: a.reshape(-1, rows, 2, N_KV, HEAD_DIM)
    return (y_p.reshape(nbp, t, d), y_s.reshape(nbs, 1, d),
            pool_p, pool_s,
            c_p, c_s, n_p, n_s, m_p[:, 0, :M_HEADS], m_s[:, :M_HEADS],
            kv5(kvc_p, t), kv5(proj1_s[:, 2 * NSA_W:2 * NSA_W + KV_W], 1),
            kv5(kvs_p, t), kv5(kvs_s, 1),
            kv5(kvw_p, t)[:, t - wbuf:],
            jnp.concatenate([cache_kv_win, kv5(kvw_s, 1)], axis=1)[:, 1:])
```

```python
import functools

import jax
import jax.numpy as jnp
from jax import lax
from jax.experimental import pallas as pl
from jax.experimental.pallas import tpu as pltpu

f32 = jnp.float32
bf16 = jnp.bfloat16

D_MODEL = 2048
POOL_WINDOWS = (2, 4, 8, 16)
POOL_W = 1024
POOL_GROUP_W = 256
POOL_STATE = 15
M_HEADS = 4
M_W = 1024
M_HEAD_DIM = 256
N_HEADS = 16
HEAD_DIM = 128
N_KV = 4
Q_PER_KV = 4
NSA_W = 2048
KV_W = 1024
KV_ROWS = 2 * N_KV
CMP_BLOCK = 32
SEL_BLOCK = 64
SEL_TOPK = 16
WINDOW = 512
PAGE_SIZE = 128
ROPE_THETA = 10000.0
ATT_SCALE = HEAD_DIM ** -0.5
EPS = 1e-6
MAIN_W = 7168
LANES = 128
NEG = -1e30
MASK_BIAS = -(2.0 ** 100)
LOG2E = 1.4426950408889634
VMEM_LIMIT = 48 * 1024 * 1024

_NT = (((1,), (1,)), ((), ()))


def _cparams(sem):
    return pltpu.CompilerParams(dimension_semantics=sem, vmem_limit_bytes=VMEM_LIMIT)


def _sigmoid(x):
    return 1.0 / (1.0 + jnp.exp(-x))


def _silu(x):
    return x * _sigmoid(x)


def _log_sigmoid(x):
    return jnp.minimum(x, 0.0) - jnp.log1p(jnp.exp(-jnp.abs(x)))


def _mm(a, b):
    return jnp.dot(a, b, preferred_element_type=f32)


def _mm_nt(a, b):
    return lax.dot_general(a, b, _NT, preferred_element_type=f32)


def _norm_proj_kernel(*refs, starts, seg_ref):
    n_w = max(seg_ref) + 1
    x_ref, g_ref = refs[:2]
    w_refs = refs[2:2 + n_w]
    wg_ref, o_ref, og_ref, h_ref = refs[2 + n_w:]
    j = pl.program_id(1)

    @pl.when(j == 0)
    def _():
        x = x_ref[...]
        r = lax.rsqrt(jnp.mean(x * x, axis=-1, keepdims=True) + EPS)
        h = ((x * r) * g_ref[...]).astype(bf16)
        h_ref[...] = h
        og_ref[...] = _mm(h, wg_ref[...])

    for k, a in enumerate(seg_ref):
        @pl.when((j >= starts[k]) & (j < starts[k + 1]))
        def _(w_ref=w_refs[a]):
            o_ref[...] = _mm(h_ref[...], w_ref[...].astype(bf16))


def norm_proj(x, g, segs, wg, tm, tn):
    arrays, seg_list = segs
    m, d = x.shape
    ng = wg.shape[1]
    starts = [0]
    for _, _, cols in seg_list:
        starts.append(starts[-1] + cols // tn)
    n_tiles = starts[-1]

    def w_spec(a):
        def index(i, j):
            idx = None
            for k, (ak, col0, cols) in enumerate(seg_list):
                if ak != a:
                    continue
                here = col0 // tn + jnp.clip(j - starts[k], 0, cols // tn - 1)
                idx = here if idx is None else jnp.where(j >= starts[k], here, idx)
            return (0, idx)
        return pl.BlockSpec((d, tn), index)

    return pl.pallas_call(
        functools.partial(_norm_proj_kernel, starts=tuple(starts), seg_ref=tuple(a for a, _, _ in seg_list)),
        grid=(m // tm, n_tiles),
        in_specs=[pl.BlockSpec((tm, d), lambda i, j: (i, 0)),
                  pl.BlockSpec((1, d), lambda i, j: (0, 0))]
                 + [w_spec(a) for a in range(len(arrays))]
                 + [pl.BlockSpec((d, ng), lambda i, j: (0, 0))],
        out_specs=[pl.BlockSpec((tm, tn), lambda i, j: (i, j)),
                   pl.BlockSpec((tm, ng), lambda i, j: (i, 0))],
        out_shape=[jax.ShapeDtypeStruct((m, n_tiles * tn), f32), jax.ShapeDtypeStruct((m, ng), f32)],
        scratch_shapes=[pltpu.VMEM((tm, d), bf16)],
        compiler_params=_cparams(("parallel", "arbitrary")),
        name="norm_proj",
    )(x, g.reshape(1, d), *arrays, wg)


def _out_proj_kernel(*refs, n_parts):
    a_refs = refs[:n_parts]
    w_refs = refs[n_parts:2 * n_parts]
    x_ref, o_ref = refs[2 * n_parts:]
    acc = x_ref[...]
    for a_ref, w_ref in zip(a_refs, w_refs):
        acc = acc + _mm(a_ref[...], w_ref[...])
    o_ref[...] = acc


def out_proj(parts, weights, x, tm):
    m, d = x.shape
    n_parts = len(parts)
    in_specs = [pl.BlockSpec((tm, a.shape[1]), lambda i: (i, 0)) for a in parts]
    in_specs += [pl.BlockSpec(w.shape, lambda i: (0, 0)) for w in weights]
    in_specs += [pl.BlockSpec((tm, d), lambda i: (i, 0))]
    args = list(parts) + list(weights) + [x]
    return pl.pallas_call(
        functools.partial(_out_proj_kernel, n_parts=n_parts),
        grid=(m // tm,),
        in_specs=in_specs,
        out_specs=pl.BlockSpec((tm, d), lambda i: (i, 0)),
        out_shape=jax.ShapeDtypeStruct((m, d), f32),
        compiler_params=_cparams(("parallel",)),
        name="out_proj",
    )(*args)


def _pool_kernel(u_ref, z_ref, w_ref, sc_ref, o_ref):
    g = pl.program_id(1)
    x = u_ref[...]
    row = lax.broadcasted_iota(jnp.int32, x.shape, 0)

    def back(a, s):
        return jnp.where(row >= s, pltpu.roll(a, s, axis=0), 0.0)

    s2 = x + back(x, 1)
    s4 = s2 + back(s2, 2)
    s8 = s4 + back(s4, 4)
    s16 = s8 + back(s8, 8)
    win = jnp.where(g == 0, s2, jnp.where(g == 1, s4, jnp.where(g == 2, s8, s16)))
    wlen = lax.shift_left(jnp.int32(2), g)
    cnt = jnp.minimum(row + 1, wlen).astype(f32)
    pooled = win / cnt - x
    y = _mm(pooled.astype(bf16), w_ref[0]) * sc_ref[...]
    o_ref[...] = (y * _silu(z_ref[...])).astype(bf16)


def pool_prompt(proj, w_pool, pool_scale, nb, t):
    ng = len(POOL_WINDOWS)
    return pl.pallas_call(
        _pool_kernel,
        grid=(nb, ng),
        in_specs=[pl.BlockSpec((t, POOL_GROUP_W), lambda b, g: (b, g)),
                  pl.BlockSpec((t, POOL_GROUP_W), lambda b, g: (b, ng + g)),
                  pl.BlockSpec((1, POOL_GROUP_W, POOL_GROUP_W), lambda b, g: (g, 0, 0)),
                  pl.BlockSpec((1, POOL_GROUP_W), lambda b, g: (0, g))],
        out_specs=pl.BlockSpec((t, POOL_GROUP_W), lambda b, g: (b, g)),
        out_shape=jax.ShapeDtypeStruct((nb * t, POOL_W), bf16),
        compiler_params=_cparams(("parallel", "arbitrary")),
        name="pool_prompt",
    )(proj, proj, w_pool, pool_scale.reshape(1, POOL_W))


def _pool_step_kernel(st_ref, u_ref, z_ref, w_ref, sc_ref, o_ref):
    u = u_ref[...]
    for g, wlen in enumerate(POOL_WINDOWS):
        lo = g * POOL_GROUP_W
        ug = u[:, lo:lo + POOL_GROUP_W]
        acc = ug
        for r in range(POOL_STATE + 1 - wlen, POOL_STATE):
            acc = acc + st_ref[:, r * POOL_W + lo:r * POOL_W + lo + POOL_GROUP_W]
        pooled = acc / float(wlen) - ug
        y = _mm(pooled.astype(bf16), w_ref[g]) * sc_ref[:, lo:lo + POOL_GROUP_W]
        o_ref[:, lo:lo + POOL_GROUP_W] = (y * _silu(z_ref[:, lo:lo + POOL_GROUP_W])).astype(bf16)


def pool_step(state_flat, proj, w_pool, pool_scale):
    nb = proj.shape[0]
    return pl.pallas_call(
        _pool_step_kernel,
        grid=(1,),
        in_specs=[pl.BlockSpec(state_flat.shape, lambda i: (0, 0)),
                  pl.BlockSpec((nb, POOL_W), lambda i: (0, 0)),
                  pl.BlockSpec((nb, POOL_W), lambda i: (0, 1)),
                  pl.BlockSpec(w_pool.shape, lambda i: (0, 0, 0)),
                  pl.BlockSpec((1, POOL_W), lambda i: (0, 0))],
        out_specs=pl.BlockSpec((nb, POOL_W), lambda i: (0, 0)),
        out_shape=jax.ShapeDtypeStruct((nb, POOL_W), bf16),
        compiler_params=_cparams(("arbitrary",)),
        name="pool_step",
    )(state_flat, proj, proj, w_pool, pool_scale.reshape(1, POOL_W))


def _head_out(hc, o, z, g):
    hc = hc * _sigmoid(o)
    hc = hc * lax.rsqrt(jnp.mean(hc * hc, axis=-1, keepdims=True) + EPS)
    return ((hc * g) * _silu(z)).astype(bf16)


def _mlstm_kernel(q_ref, k_ref, v_ref, o_ref, z_ref, gi_ref, gf_ref, bi_ref, bf_ref, mhg_ref,
                  y_ref, c_ref, n_ref, m_ref):
    @pl.when(pl.program_id(1) == 0)
    def _():
        c_ref[...] = jnp.zeros_like(c_ref)
        n_ref[...] = jnp.zeros_like(n_ref)
        m_ref[...] = jnp.zeros_like(m_ref)

    ln = q_ref.shape[0]
    gi = gi_ref[...] + bi_ref[...]
    lf = _log_sigmoid(gf_ref[...] + bf_ref[...])
    row = lax.broadcasted_iota(jnp.int32, lf.shape, 0)
    b = lf
    s = 1
    while s < ln:
        b = b + jnp.where(row >= s, pltpu.roll(b, s, axis=0), 0.0)
        s *= 2
    r_t = (gi - b).T
    tt = lax.broadcasted_iota(jnp.int32, (ln, ln), 0)
    ss = lax.broadcasted_iota(jnp.int32, (ln, ln), 1)
    causal = ss <= tt
    lane = lax.broadcasted_iota(jnp.int32, (1, LANES), 1)
    m_vec = m_ref[0]
    for h in range(M_HEADS):
        hs = slice(h * M_HEAD_DIM, (h + 1) * M_HEAD_DIM)
        b_col = b[:, h:h + 1]
        ig_col = gi[:, h:h + 1]
        m_prev = m_vec[:, h:h + 1]
        inter = b_col + m_prev
        dmat = jnp.where(causal, b_col + r_t[h:h + 1, :], -jnp.inf)
        m_t = jnp.maximum(inter, jnp.max(dmat, axis=1, keepdims=True))
        dw = jnp.exp(dmat - m_t)
        iw = jnp.exp(inter - m_t)
        q = q_ref[:, hs]
        k = k_ref[:, hs] * (M_HEAD_DIM ** -0.5)
        v = v_ref[:, hs]
        qb, kb, vb = q.astype(bf16), k.astype(bf16), v.astype(bf16)
        c = c_ref[0, h]
        n = n_ref[0, h:h + 1, :]
        qk = _mm_nt(qb, kb) * dw
        num = iw * _mm(qb, c.astype(bf16)) + _mm(qk.astype(bf16), vb)
        den = iw * jnp.sum(q * n, axis=1, keepdims=True) + jnp.sum(qk, axis=1, keepdims=True)
        hc = num / jnp.maximum(jnp.abs(den), jnp.exp(-m_t))
        y_ref[:, hs] = _head_out(hc, o_ref[:, hs], z_ref[:, hs], mhg_ref[:, hs])
        m_last = m_t[ln - 1:ln, :]
        b_last = b_col[ln - 1:ln, :]
        ws = jnp.exp(b_last - b_col + ig_col - m_last)
        dec = jnp.exp(b_last + m_prev - m_last)
        kw = ws * k
        c_ref[0, h] = dec * c + _mm(kw.T.astype(bf16), vb)
        n_ref[0, h:h + 1, :] = dec * n + jnp.sum(kw, axis=0, keepdims=True)
        m_vec = jnp.where(lane == h, m_last, m_vec)
    m_ref[0] = m_vec


def mlstm_prompt(proj, gates, bias_i, bias_f, mh_norm_g, nb, t, ln):
    nc = t // ln
    col = lambda cb: pl.BlockSpec((ln, M_W), lambda b, c: (b * nc + c, cb))
    gcol = lambda cb: pl.BlockSpec((ln, LANES), lambda b, c: (b * nc + c, cb))
    vec = lambda w: pl.BlockSpec((1, w), lambda b, c: (0, 0))
    return pl.pallas_call(
        _mlstm_kernel,
        grid=(nb, nc),
        in_specs=[col(2), col(3), col(4), col(5), col(6), gcol(0), gcol(1), vec(LANES), vec(LANES), vec(M_W)],
        out_specs=[pl.BlockSpec((ln, M_W), lambda b, c: (b * nc + c, 0)),
                   pl.BlockSpec((1, M_HEADS, M_HEAD_DIM, M_HEAD_DIM), lambda b, c: (b, 0, 0, 0)),
                   pl.BlockSpec((1, M_HEADS, M_HEAD_DIM), lambda b, c: (b, 0, 0)),
                   pl.BlockSpec((1, 1, LANES), lambda b, c: (b, 0, 0))],
        out_shape=[jax.ShapeDtypeStruct((nb * t, M_W), bf16),
                   jax.ShapeDtypeStruct((nb, M_HEADS, M_HEAD_DIM, M_HEAD_DIM), f32),
                   jax.ShapeDtypeStruct((nb, M_HEADS, M_HEAD_DIM), f32),
                   jax.ShapeDtypeStruct((nb, 1, LANES), f32)],
        compiler_params=_cparams(("parallel", "arbitrary")),
        name="mlstm_prompt",
    )(proj, proj, proj, proj, proj, gates, gates, bias_i, bias_f, mh_norm_g.reshape(1, M_W))


def _mlstm_step_kernel(q_ref, k_ref, v_ref, o_ref, z_ref, gi_ref, gf_ref, bi_ref, bf_ref, mhg_ref,
                       c_ref, n_ref, m_ref, y_ref, co_ref, no_ref, mo_ref):
    nb = q_ref.shape[0]
    gi = gi_ref[...] + bi_ref[...]
    lf = _log_sigmoid(gf_ref[...] + bf_ref[...])
    inter = lf + m_ref[...]
    m_t = jnp.maximum(inter, gi)
    dw_all = jnp.exp(gi - m_t)
    iw_all = jnp.exp(inter - m_t)
    em_all = jnp.exp(-m_t)
    mo_ref[...] = m_t
    d0 = lax.broadcasted_iota(jnp.int32, (M_HEAD_DIM, M_HEAD_DIM), 0)
    d1 = lax.broadcasted_iota(jnp.int32, (M_HEAD_DIM, M_HEAD_DIM), 1)
    eye = d0 == d1
    for j in range(nb):
        for h in range(M_HEADS):
            hs = slice(h * M_HEAD_DIM, (h + 1) * M_HEAD_DIM)
            dw = dw_all[j:j + 1, h:h + 1]
            iw = iw_all[j:j + 1, h:h + 1]
            em = em_all[j:j + 1, h:h + 1]
            q = q_ref[j:j + 1, hs]
            k = k_ref[j:j + 1, hs] * (M_HEAD_DIM ** -0.5)
            v = v_ref[j:j + 1, hs]
            c = c_ref[j, h]
            n = n_ref[j, h:h + 1, :]
            qc = _mm(jnp.broadcast_to(q, (8, M_HEAD_DIM)).astype(bf16), c.astype(bf16))[0:1, :]
            qk = jnp.sum(q * k, axis=1, keepdims=True) * dw
            num = iw * qc + qk * v
            den = iw * jnp.sum(q * n, axis=1, keepdims=True) + qk
            hc = num / jnp.maximum(jnp.abs(den), em)
            y_ref[j:j + 1, hs] = _head_out(hc, o_ref[j:j + 1, hs], z_ref[j:j + 1, hs], mhg_ref[:, hs])
            kdiag = jnp.where(eye, jnp.broadcast_to(k, (M_HEAD_DIM, M_HEAD_DIM)), 0.0).astype(bf16)
            vrep = jnp.broadcast_to(v, (M_HEAD_DIM, M_HEAD_DIM)).astype(bf16)
            co_ref[j, h] = iw * c + dw * _mm(kdiag, vrep)
            no_ref[j, h:h + 1, :] = iw * n + dw * k


def mlstm_step(proj, gates, bias_i, bias_f, mh_norm_g, c0, n0, m0_pad, bb):
    nb = proj.shape[0]
    col = lambda cb: pl.BlockSpec((bb, M_W), lambda i: (i, cb))
    gcol = lambda cb: pl.BlockSpec((bb, LANES), lambda i: (i, cb))
    vec = lambda w: pl.BlockSpec((1, w), lambda i: (0, 0))
    cspec = pl.BlockSpec((bb, M_HEADS, M_HEAD_DIM, M_HEAD_DIM), lambda i: (i, 0, 0, 0))
    nspec = pl.BlockSpec((bb, M_HEADS, M_HEAD_DIM), lambda i: (i, 0, 0))
    return pl.pallas_call(
        _mlstm_step_kernel,
        grid=(nb // bb,),
        in_specs=[col(2), col(3), col(4), col(5), col(6), gcol(0), gcol(1), vec(LANES), vec(LANES), vec(M_W),
                  cspec, nspec, gcol(0)],
        out_specs=[pl.BlockSpec((bb, M_W), lambda i: (i, 0)), cspec, nspec, gcol(0)],
        out_shape=[jax.ShapeDtypeStruct((nb, M_W), bf16),
                   jax.ShapeDtypeStruct(c0.shape, f32),
                   jax.ShapeDtypeStruct(n0.shape, f32),
                   jax.ShapeDtypeStruct((nb, LANES), f32)],
        compiler_params=_cparams(("parallel",)),
        name="mlstm_step",
    )(proj, proj, proj, proj, proj, gates, gates, bias_i, bias_f, mh_norm_g.reshape(1, M_W), c0, n0, m0_pad)


def _rope(x, cos, sin_signed):
    return x * cos + pltpu.roll(x, HEAD_DIM // 2, axis=1) * sin_signed


def _nsa_prep_kernel(*refs, prompt):
    q_ref, kvc_ref, kvs_ref, kvw_ref, cos_ref, sin_ref, qr_ref = refs[:7]
    tq = q_ref.shape[0]
    cos, sin = cos_ref[...], sin_ref[...]
    for h in range(N_HEADS):
        hs = slice(h * HEAD_DIM, (h + 1) * HEAD_DIM)
        qr_ref[:, hs] = _rope(q_ref[:, hs], cos, sin).astype(bf16)

    def chunks(src, rotate):
        for c in range(KV_ROWS):
            x = src[:, c * HEAD_DIM:(c + 1) * HEAD_DIM]
            yield c, (_rope(x, cos, sin) if rotate and c < N_KV else x)

    if prompt:
        kvc_i, kvs_i, kvw_i, blk_ref, kvs_b, kvw_b = refs[7:13]
        for c, x in chunks(kvc_ref, False):
            kvc_i[pl.ds(c, tq, stride=KV_ROWS), :] = x
        for src, dst_i, dst_b in ((kvs_ref, kvs_i, kvs_b), (kvw_ref, kvw_i, kvw_b)):
            for c, x in chunks(src, True):
                dst_i[pl.ds(c, tq, stride=KV_ROWS), :] = x
                dst_b[:, c * HEAD_DIM:(c + 1) * HEAD_DIM] = x.astype(bf16)
        nblk = tq // CMP_BLOCK
        means = jnp.sum(kvc_ref[...].reshape(nblk, CMP_BLOCK, KV_W), axis=1) * (1.0 / CMP_BLOCK)
        for c in range(KV_ROWS):
            blk_ref[pl.ds(c, nblk, stride=KV_ROWS), :] = means[:, c * HEAD_DIM:(c + 1) * HEAD_DIM]
    else:
        for src, dst in ((kvs_ref, refs[7]), (kvw_ref, refs[8])):
            for c, x in chunks(src, True):
                dst[:, c * HEAD_DIM:(c + 1) * HEAD_DIM] = x


def nsa_prep(proj, cos, sin, tq, n_pos_blocks, prompt):
    m = proj.shape[0]
    row = lambda w, cb: pl.BlockSpec((tq, w), lambda i: (i, cb))
    tab = pl.BlockSpec((tq, HEAD_DIM), lambda i: (i % n_pos_blocks, 0))
    if prompt:
        inter = pl.BlockSpec((tq * KV_ROWS, HEAD_DIM), lambda i: (i, 0))
        inter_shape = jax.ShapeDtypeStruct((m * KV_ROWS, HEAD_DIM), f32)
        out_specs = [row(NSA_W, 0), inter, inter, inter,
                     pl.BlockSpec((tq // CMP_BLOCK * KV_ROWS, HEAD_DIM), lambda i: (i, 0)), row(KV_W, 0), row(KV_W, 0)]
        out_shape = [jax.ShapeDtypeStruct((m, NSA_W), bf16), inter_shape, inter_shape, inter_shape,
                     jax.ShapeDtypeStruct((m // CMP_BLOCK * KV_ROWS, HEAD_DIM), f32),
                     jax.ShapeDtypeStruct((m, KV_W), bf16), jax.ShapeDtypeStruct((m, KV_W), bf16)]
    else:
        out_specs = [row(NSA_W, 0), row(KV_W, 0), row(KV_W, 0)]
        out_shape = [jax.ShapeDtypeStruct((m, NSA_W), bf16), jax.ShapeDtypeStruct((m, KV_W), f32),
                     jax.ShapeDtypeStruct((m, KV_W), f32)]
    return pl.pallas_call(
        functools.partial(_nsa_prep_kernel, prompt=prompt),
        grid=(m // tq,),
        in_specs=[row(NSA_W, 0), row(KV_W, 4), row(KV_W, 5), row(KV_W, 6), tab, tab],
        out_specs=out_specs,
        out_shape=out_shape,
        compiler_params=_cparams(("parallel",)),
        name="nsa_prep",
    )(proj, proj, proj, proj, cos, sin)


def _select_blocks(imp, q_pos, n_cand):
    lane = lax.broadcasted_iota(jnp.int32, imp.shape, 1)
    pair = imp + pltpu.roll(imp, LANES - 1, axis=1)
    cur2 = lax.shift_left(lax.shift_right_logical(q_pos, 6), 1)
    valid = ((lane & 1) == 0) & (lane <= cur2)
    v = jnp.where(lane == cur2, jnp.inf, pair)
    v = jnp.where(valid, v, -jnp.inf)
    cnt = jnp.zeros(imp.shape, f32)
    for i in range(n_cand):
        vi = v[:, 2 * i:2 * i + 1]
        before = jnp.where(lane > 2 * i, 1.0, 0.0)
        cnt = cnt + jnp.where(vi > v, 1.0, 0.0) + jnp.where(vi == v, before, 0.0)
    return jnp.where(valid & (cnt < SEL_TOPK), 1.0, 0.0)


def _pad_rows(x, rows):
    return jnp.concatenate([x, jnp.zeros((rows - x.shape[0], x.shape[1]), x.dtype)], axis=0)


def _masked_softmax(s, mask):
    s = jnp.where(mask, s, -jnp.inf)
    m = jnp.max(s, axis=-1, keepdims=True)
    m = jnp.where(m > -jnp.inf, m, 0.0)
    p = jnp.exp(s - m)
    return p / jnp.maximum(jnp.sum(p, axis=-1, keepdims=True), 1e-30)


def _select_block_rows(imp, q_pos, n_cand):
    row = lax.broadcasted_iota(jnp.int32, imp.shape, 0)
    pair = imp + pltpu.roll(imp, imp.shape[0] - 1, axis=0)
    cur2 = lax.shift_left(lax.shift_right_logical(q_pos, 6), 1)
    valid = ((row & 1) == 0) & (row <= cur2)
    v = jnp.where(row == cur2, jnp.inf, pair)
    v = jnp.where(valid, v, -jnp.inf)
    cnt = jnp.zeros(imp.shape, f32)
    for i in range(n_cand):
        vi = v[2 * i:2 * i + 1, :]
        before = jnp.where(row > 2 * i, 1.0, 0.0)
        cnt = cnt + jnp.where(vi > v, 1.0, 0.0) + jnp.where(vi == v, before, 0.0)
    return jnp.where(valid & (cnt < SEL_TOPK), 1.0, 0.0)


def _cmp_prompt_kernel(q_ref, blk_ref, oc_ref, sel_ref, *, n_blocks):
    tq = q_ref.shape[0]
    t0 = pl.program_id(1) * tq
    row = lax.broadcasted_iota(jnp.int32, (LANES, tq), 0)
    q_pos = lax.broadcasted_iota(jnp.int32, (LANES, tq), 1) + t0
    vis = (row < n_blocks) & ((row + 1) * CMP_BLOCK - 1 <= q_pos)

    def block_rows(c):
        return _pad_rows(blk_ref[pl.ds(c, n_blocks, stride=KV_ROWS), :], LANES).astype(bf16)

    for g in range(N_KV):
        kg = block_rows(g)
        vg = block_rows(N_KV + g)
        imp = jnp.zeros((LANES, tq), f32)
        for r in range(Q_PER_KV):
            hs = slice((g * Q_PER_KV + r) * HEAD_DIM, (g * Q_PER_KV + r + 1) * HEAD_DIM)
            s = jnp.where(vis, _mm_nt(kg, q_ref[:, hs].astype(bf16)) * ATT_SCALE, -jnp.inf)
            m = jnp.max(s, axis=0, keepdims=True)
            m = jnp.where(m > -jnp.inf, m, 0.0)
            p = jnp.exp(s - m)
            p = p / jnp.maximum(jnp.sum(p, axis=0, keepdims=True), 1e-30)
            oc_ref[:, hs] = _mm(p.T.astype(bf16), vg).astype(bf16)
            imp = imp + p
        flags = _select_block_rows(imp, q_pos, n_blocks // 2)
        sel_ref[:, g * LANES:(g + 1) * LANES] = flags.T.astype(bf16)


def cmp_prompt(proj, blocks, nb, t, tq):
    nq = t // tq
    n_blocks = t // CMP_BLOCK
    return pl.pallas_call(
        functools.partial(_cmp_prompt_kernel, n_blocks=n_blocks),
        grid=(nb, nq),
        in_specs=[pl.BlockSpec((tq, NSA_W), lambda b, i: (b * nq + i, 0)),
                  pl.BlockSpec((n_blocks * KV_ROWS, HEAD_DIM), lambda b, i: (b, 0))],
        out_specs=[pl.BlockSpec((tq, NSA_W), lambda b, i: (b * nq + i, 0)),
                   pl.BlockSpec((tq, N_KV * LANES), lambda b, i: (b * nq + i, 0))],
        out_shape=[jax.ShapeDtypeStruct((nb * t, NSA_W), bf16),
                   jax.ShapeDtypeStruct((nb * t, N_KV * LANES), bf16)],
        compiler_params=_cparams(("parallel", "parallel")),
        name="cmp_prompt",
    )(proj, blocks)


def _attn_kernel(*refs, mode, tile, reach):
    if mode == "sel":
        q_ref, k_ref, v_ref, sel_ref, et_ref, o_ref, qp_sc, s_sc, mx_sc, mb_sc, acc_sc = refs
    else:
        q_ref, k_ref, v_ref, o_ref, qp_sc, s_sc, mx_sc, mb_sc, acc_sc = refs
    qi = pl.program_id(2)
    rows = Q_PER_KV * tile
    for r in range(Q_PER_KV):
        qh = q_ref[:, r * HEAD_DIM:(r + 1) * HEAD_DIM]
        if mode == "sel":
            bias = ((1.0 - sel_ref[...].astype(f32)) * MASK_BIAS).astype(bf16)
            qh = jnp.concatenate([qh, bias], axis=1)
        qp_sc[r * tile:(r + 1) * tile, :] = qh
    lo = 0 if mode == "sel" else jnp.maximum(qi - reach, 0)

    def scores(kj, n):
        start = pl.multiple_of(kj * tile, tile)
        kt = k_ref[pl.ds(start, n * tile), :]
        if mode == "sel":
            kt = jnp.concatenate([kt, et_ref[pl.ds(start, n * tile), :]], axis=1)
        s = _mm_nt(qp_sc[...], kt)
        if mode == "win" or n == 1:
            t_in = lax.broadcasted_iota(jnp.int32, s.shape, 0) & (tile - 1)
            dpos = t_in - lax.broadcasted_iota(jnp.int32, s.shape, 1) + (qi - kj) * tile
            ok = (dpos >= 0) & (dpos <= WINDOW) if mode == "win" else dpos >= 0
            s = jnp.where(ok, s, MASK_BIAS)
        return s

    def store_scores(kj, n, s):
        mx = mx_sc[...]
        for i in range(n):
            s_sc[kj + i] = s[:, i * tile:(i + 1) * tile]
        for c in range(n * tile // LANES):
            mx = jnp.maximum(mx, s[:, c * LANES:(c + 1) * LANES])
        mx_sc[...] = mx

    def weigh(kj, n):
        start = pl.multiple_of(kj * tile, tile)
        mb = mb_sc[...]
        p = jnp.concatenate([jnp.exp2((s_sc[kj + i][:, c * LANES:(c + 1) * LANES] - mb) * (ATT_SCALE * LOG2E))
                             for i in range(n) for c in range(tile // LANES)], axis=1)
        vt = jnp.concatenate([v_ref[pl.ds(start, n * tile), :], jnp.ones((n * tile, HEAD_DIM), bf16)], axis=1)
        acc_sc[...] += _mm(p.astype(bf16), vt)

    def in_pairs(first, count, fn):
        def trip(i, carry):
            fn(first + 2 * i, 2)
            return carry

        lax.fori_loop(0, count // 2, trip, 0)

        @pl.when(count % 2 == 1)
        def _():
            fn(first + count - 1, 1)

    mx_sc[...] = jnp.full_like(mx_sc, MASK_BIAS)
    if mode == "sel":
        in_pairs(lo, qi - lo, lambda kj, n: store_scores(kj, n, scores(kj, n)))
        store_scores(qi, 1, scores(qi, 1))
    else:
        in_pairs(lo, qi - lo + 1, lambda kj, n: store_scores(kj, n, scores(kj, n)))
    m = jnp.max(mx_sc[...], axis=1, keepdims=True)
    mb_sc[...] = jnp.broadcast_to(m, mb_sc.shape)
    acc_sc[...] = jnp.zeros_like(acc_sc)
    in_pairs(lo, qi - lo + 1, weigh)
    acc = acc_sc[...]
    o = (acc[:, :HEAD_DIM] / acc[:, HEAD_DIM:]).astype(bf16)
    for r in range(Q_PER_KV):
        o_ref[:, r * HEAD_DIM:(r + 1) * HEAD_DIM] = o[r * tile:(r + 1) * tile, :]


def attn_prompt(q_rot, kv_b, sel, expand_t, nb, t, tile, mode):
    nq = t // tile
    reach = -(-WINDOW // tile)
    qw = Q_PER_KV * HEAD_DIM
    kdim = 2 * HEAD_DIM if mode == "sel" else HEAD_DIM
    rows = Q_PER_KV * tile
    in_specs = [pl.BlockSpec((tile, qw), lambda b, g, i: (b * nq + i, g)),
                pl.BlockSpec((t, HEAD_DIM), lambda b, g, i: (b, g)),
                pl.BlockSpec((t, HEAD_DIM), lambda b, g, i: (b, N_KV + g))]
    args = [q_rot, kv_b, kv_b]
    if mode == "sel":
        in_specs += [pl.BlockSpec((tile, LANES), lambda b, g, i: (b * nq + i, g)),
                     pl.BlockSpec((t, LANES), lambda b, g, i: (0, 0))]
        args += [sel, expand_t]
    return pl.pallas_call(
        functools.partial(_attn_kernel, mode=mode, tile=tile, reach=reach),
        grid=(nb, N_KV, nq),
        in_specs=in_specs,
        out_specs=pl.BlockSpec((tile, qw), lambda b, g, i: (b * nq + i, g)),
        out_shape=jax.ShapeDtypeStruct((nb * t, NSA_W), bf16),
        scratch_shapes=[pltpu.VMEM((rows, kdim), bf16),
                        pltpu.VMEM((nq, rows, tile), f32),
                        pltpu.VMEM((rows, LANES), f32),
                        pltpu.VMEM((rows, LANES), f32),
                        pltpu.VMEM((rows, 2 * HEAD_DIM), f32)],
        compiler_params=_cparams(("parallel", "parallel", "arbitrary")),
        name="attn_" + mode,
    )(*args)


def _nsa_out_kernel(oc_ref, os_ref, ow_ref, g_ref, b_ref, z_ref, w_ref, x_ref, fg_ref, o_ref, a_sc):
    gate = _sigmoid(g_ref[...] + b_ref[...])
    for h in range(N_HEADS):
        hs = slice(h * HEAD_DIM, (h + 1) * HEAD_DIM)
        o = (gate[:, 3 * h:3 * h + 1] * oc_ref[:, hs].astype(f32)
             + gate[:, 3 * h + 1:3 * h + 2] * os_ref[:, hs].astype(f32)
             + gate[:, 3 * h + 2:3 * h + 3] * ow_ref[:, hs].astype(f32))
        a_sc[:, hs] = (o * _silu(z_ref[:, hs])).astype(bf16)
    acc = x_ref[...] + _mm(a_sc[...], w_ref[...])
    r = lax.rsqrt(jnp.mean(acc * acc, axis=-1, keepdims=True) + EPS)
    o_ref[...] = (acc * r) * fg_ref[...]


def nsa_out(o_c, o_s, o_w, gates, bias, proj, w_out, x, final_g, tm):
    m, d = x.shape
    row = lambda w, cb: pl.BlockSpec((tm, w), lambda i: (i, cb))
    return pl.pallas_call(
        _nsa_out_kernel,
        grid=(m // tm,),
        in_specs=[row(NSA_W, 0), row(NSA_W, 0), row(NSA_W, 0), row(LANES, 0),
                  pl.BlockSpec((1, LANES), lambda i: (0, 0)),
                  row(NSA_W, 1),
                  pl.BlockSpec(w_out.shape, lambda i: (0, 0)),
                  row(d, 0),
                  pl.BlockSpec((1, d), lambda i: (0, 0))],
        out_specs=row(d, 0),
        out_shape=jax.ShapeDtypeStruct((m, d), f32),
        scratch_shapes=[pltpu.VMEM((tm, NSA_W), bf16)],
        compiler_params=_cparams(("parallel",)),
        name="nsa_out",
    )(o_c, o_s, o_w, gates, bias, proj, w_out, x, final_g.reshape(1, d))


def _head_group(shape):
    return lax.shift_right_logical(lax.broadcasted_iota(jnp.int32, shape, 0), 2)


def _cmp_step_kernel(pt_ref, q_ref, *refs, n_pages, q_pos):
    pages = refs[:n_pages]
    oc_ref, sel_ref, blk_sc = refs[n_pages:]
    per_page = PAGE_SIZE // CMP_BLOCK
    for p in range(n_pages):
        x = pages[p][0].reshape(per_page, CMP_BLOCK, KV_ROWS, HEAD_DIM)
        means = jnp.sum(x, axis=1) * (1.0 / CMP_BLOCK)
        blk_sc[p * per_page * KV_ROWS:(p + 1) * per_page * KV_ROWS, :] = means.reshape(per_page * KV_ROWS, HEAD_DIM)
    n_blocks = blk_sc.shape[0] // KV_ROWS
    qb = q_ref[0].astype(bf16)
    grp = _head_group((N_HEADS, LANES))
    lane = lax.broadcasted_iota(jnp.int32, (N_HEADS, LANES), 1)
    s = jnp.zeros((N_HEADS, LANES), f32)
    for g in range(N_KV):
        kg = _pad_rows(blk_sc[pl.ds(g, n_blocks, stride=KV_ROWS), :], LANES).astype(bf16)
        s = jnp.where(grp == g, _mm_nt(qb, kg), s)
    vis = (lane < n_blocks) & ((lane + 1) * CMP_BLOCK - 1 <= q_pos)
    p = _masked_softmax(s * ATT_SCALE, vis)
    pb = p.astype(bf16)
    o = jnp.zeros((N_HEADS, HEAD_DIM), f32)
    for g in range(N_KV):
        vg = _pad_rows(blk_sc[pl.ds(N_KV + g, n_blocks, stride=KV_ROWS), :], LANES).astype(bf16)
        o = jnp.where(grp == g, _mm(pb, vg), o)
    oc_ref[0] = o
    row8 = lax.broadcasted_iota(jnp.int32, (8, LANES), 0)
    imp = jnp.zeros((8, LANES), f32)
    for g in range(N_KV):
        imp_g = jnp.sum(p[g * Q_PER_KV:(g + 1) * Q_PER_KV, :], axis=0, keepdims=True)
        imp = jnp.where(row8 == g, imp_g, imp)
    sel_ref[0] = _select_blocks(imp, jnp.full((8, LANES), q_pos, jnp.int32), q_pos // SEL_BLOCK + 1)


def _page_specs(n_pages):
    return [pl.BlockSpec((1, PAGE_SIZE * KV_ROWS, HEAD_DIM), lambda b, pt, p=p: (pt[b * n_pages + p], 0, 0))
            for p in range(n_pages)]


def cmp_step(q3, cache, pt_flat, n_pages, q_pos):
    nb = q3.shape[0]
    head_blk = pl.BlockSpec((1, N_HEADS, HEAD_DIM), lambda b, pt: (b, 0, 0))
    return pl.pallas_call(
        functools.partial(_cmp_step_kernel, n_pages=n_pages, q_pos=q_pos),
        grid_spec=pltpu.PrefetchScalarGridSpec(
            num_scalar_prefetch=1,
            grid=(nb,),
            in_specs=[head_blk] + _page_specs(n_pages),
            out_specs=[head_blk, pl.BlockSpec((1, 8, LANES), lambda b, pt: (b, 0, 0))],
            scratch_shapes=[pltpu.VMEM((n_pages * PAGE_SIZE // CMP_BLOCK * KV_ROWS, HEAD_DIM), f32)]),
        out_shape=[jax.ShapeDtypeStruct((nb, N_HEADS, HEAD_DIM), f32),
                   jax.ShapeDtypeStruct((nb, 8, LANES), f32)],
        compiler_params=_cparams(("parallel",)),
        name="cmp_step",
    )(pt_flat, q3, *([cache] * n_pages))


def _decode_attend(qb, n_tiles, kv_tile, flags, new_row, s_sc):
    grp = _head_group((N_HEADS, LANES))
    lane = lax.broadcasted_iota(jnp.int32, (N_HEADS, LANES), 1)
    for p in range(n_tiles):
        sp = jnp.zeros((N_HEADS, LANES), f32)
        for g in range(N_KV):
            sg = _mm_nt(qb, kv_tile(p, g).astype(bf16)) * ATT_SCALE
            if flags is not None:
                f0 = flags[g:g + 1, 4 * p:4 * p + 1]
                f1 = flags[g:g + 1, 4 * p + 2:4 * p + 3]
                sg = jnp.where(jnp.where(lane < SEL_BLOCK, f0, f1) > 0.5, sg, NEG)
            sp = jnp.where(grp == g, sg, sp)
        s_sc[:, p * LANES:(p + 1) * LANES] = sp
    qf = qb.astype(f32)
    s_new = jnp.zeros((N_HEADS, 1), f32)
    grp1 = _head_group((N_HEADS, 1))
    for g in range(N_KV):
        kn = new_row[:, g * HEAD_DIM:(g + 1) * HEAD_DIM].astype(bf16).astype(f32)
        s_new = jnp.where(grp1 == g, jnp.sum(qf * kn, axis=1, keepdims=True) * ATT_SCALE, s_new)
    s_all = s_sc[...]
    m = jnp.maximum(jnp.max(s_all, axis=1, keepdims=True), s_new)
    p_all = jnp.exp(s_all - m)
    p_new = jnp.exp(s_new - m)
    den = jnp.sum(p_all, axis=1, keepdims=True) + p_new
    o = jnp.zeros((N_HEADS, HEAD_DIM), f32)
    for g in range(N_KV):
        vn = new_row[:, KV_W // 2 + g * HEAD_DIM:KV_W // 2 + (g + 1) * HEAD_DIM].astype(bf16).astype(f32)
        o = jnp.where(grp == g, p_new * vn, o)
    for p in range(n_tiles):
        pb = p_all[:, p * LANES:(p + 1) * LANES].astype(bf16)
        for g in range(N_KV):
            o = o + jnp.where(grp == g, _mm(pb, kv_tile(p, N_KV + g).astype(bf16)), 0.0)
    return o / den


def _sel_step_kernel(pt_ref, q_ref, sel_ref, new_ref, *refs, n_pages):
    pages = refs[:n_pages]
    o_ref, s_sc = refs[n_pages:]
    kv_tile = lambda p, c: pages[p][0, pl.ds(c, PAGE_SIZE, stride=KV_ROWS), :]
    o_ref[0] = _decode_attend(q_ref[0], n_pages, kv_tile, sel_ref[0], new_ref[0], s_sc)


def sel_step(q3, sel, new_rows, cache, pt_flat, n_pages):
    nb = q3.shape[0]
    head_blk = pl.BlockSpec((1, N_HEADS, HEAD_DIM), lambda b, pt: (b, 0, 0))
    return pl.pallas_call(
        functools.partial(_sel_step_kernel, n_pages=n_pages),
        grid_spec=pltpu.PrefetchScalarGridSpec(
            num_scalar_prefetch=1,
            grid=(nb,),
            in_specs=[head_blk, pl.BlockSpec((1, 8, LANES), lambda b, pt: (b, 0, 0)),
                      pl.BlockSpec((1, 1, KV_W), lambda b, pt: (b, 0, 0))] + _page_specs(n_pages),
            out_specs=head_blk,
            scratch_shapes=[pltpu.VMEM((N_HEADS, n_pages * PAGE_SIZE), f32)]),
        out_shape=jax.ShapeDtypeStruct((nb, N_HEADS, HEAD_DIM), f32),
        compiler_params=_cparams(("parallel",)),
        name="sel_step",
    )(pt_flat, q3, sel, new_rows, *([cache] * n_pages))


def _win_step_kernel(q_ref, new_ref, new8_ref, win_ref, o_ref, wo_ref, s_sc):
    wrows = win_ref.shape[1]
    n_tiles = wrows // (LANES * KV_ROWS)
    kv_tile = lambda p, c: win_ref[0, pl.ds(p * LANES * KV_ROWS + c, LANES, stride=KV_ROWS), :]
    o_ref[0] = _decode_attend(q_ref[0], n_tiles, kv_tile, None, new_ref[0], s_sc)
    wo_ref[0, :wrows - KV_ROWS, :] = win_ref[0, KV_ROWS:, :]
    wo_ref[0, wrows - KV_ROWS:, :] = new8_ref[0]


def win_step(q3, new_rows, win):
    nb, wrows = win.shape[0], win.shape[1]
    head_blk = pl.BlockSpec((1, N_HEADS, HEAD_DIM), lambda b: (b, 0, 0))
    win_blk = pl.BlockSpec((1, wrows, HEAD_DIM), lambda b: (b, 0, 0))
    return pl.pallas_call(
        _win_step_kernel,
        grid=(nb,),
        in_specs=[head_blk, pl.BlockSpec((1, 1, KV_W), lambda b: (b, 0, 0)),
                  pl.BlockSpec((1, KV_ROWS, HEAD_DIM), lambda b: (b, 0, 0)), win_blk],
        out_specs=[head_blk, win_blk],
        out_shape=[jax.ShapeDtypeStruct((nb, N_HEADS, HEAD_DIM), f32), jax.ShapeDtypeStruct(win.shape, f32)],
        scratch_shapes=[pltpu.VMEM((N_HEADS, wrows // KV_ROWS), f32)],
        compiler_params=_cparams(("parallel",)),
        name="win_step",
    )(q3, new_rows, new_rows.reshape(nb, KV_ROWS, HEAD_DIM), win)


def _rope_tables(pos):
    half = HEAD_DIM // 2
    inv = ROPE_THETA ** (-jnp.arange(half, dtype=f32) / half)
    ang = pos.astype(f32)[:, None] * inv[None, :]
    cos, sin = jnp.cos(ang), jnp.sin(ang)
    return jnp.concatenate([cos, cos], axis=1), jnp.concatenate([-sin, sin], axis=1)


def _pad_cols(a, width):
    return jnp.pad(a, ((0, 0), (0, width - a.shape[1])))


def kernel(x_prompt, x_sample, state_pool, state_mlstm_c, state_mlstm_n, state_mlstm_m, cache_kv_cmp, cache_kv_sel, cache_kv_win, page_table, norm0_g, w_in0, b_gate0, w_pool, pool_scale, mh_norm_g, w_out0, norm1_g, w_in1, b_gate1, w_out1, final_g):
    nbp, t, d = x_prompt.shape
    nbs = x_sample.shape[0]
    mp = nbp * t
    n_pages = page_table.shape[1]
    past_len = n_pages * PAGE_SIZE
    wbuf = cache_kv_win.shape[1]

    w0 = ([w_in0], [(0, 0, MAIN_W)])
    wg0 = jnp.concatenate([_pad_cols(w_in0[:, MAIN_W:MAIN_W + M_HEADS], LANES),
                           _pad_cols(w_in0[:, MAIN_W + M_HEADS:], LANES)], axis=1).astype(bf16)
    bias_i = _pad_cols(b_gate0[None, :M_HEADS], LANES)
    bias_f = _pad_cols(b_gate0[None, M_HEADS:], LANES)
    g_lo = NSA_W + 3 * KV_W
    g_hi = g_lo + 3 * N_HEADS
    w1 = ([w_in1, w_in1[:, g_hi:]], [(0, 0, NSA_W), (1, 0, NSA_W), (0, NSA_W, 3 * KV_W)])
    wg1 = _pad_cols(w_in1[:, g_lo:g_hi], LANES).astype(bf16)
    bias1 = _pad_cols(b_gate1[None, :], LANES)
    w_pool_b = w_pool.astype(bf16)
    wo0_pool = w_out0[:POOL_W].astype(bf16)
    wo0_m = w_out0[POOL_W:].astype(bf16)
    wo1 = w_out1.astype(bf16)

    xp = x_prompt.reshape(mp, d)
    xs = x_sample.reshape(nbs, d)

    proj_p, gates_p = norm_proj(xp, norm0_g, w0, wg0, 1024, 512)
    proj_s, gates_s = norm_proj(xs, norm0_g, w0, wg0, nbs, 512)

    ypool_p = pool_prompt(proj_p, w_pool_b, pool_scale, nbp, t)
    ym_p, c_p, n_p, m_p = mlstm_prompt(proj_p, gates_p, bias_i, bias_f, mh_norm_g, nbp, t, 256)
    xp1 = out_proj([ypool_p, ym_p], [wo0_pool, wo0_m], xp, 512)

    ypool_s = pool_step(state_pool.reshape(nbs, POOL_STATE * POOL_W), proj_s, w_pool_b, pool_scale)
    m0_pad = _pad_cols(state_mlstm_m, LANES)
    ym_s, c_s, n_s, m_s = mlstm_step(proj_s, gates_s, bias_i, bias_f, mh_norm_g,
                                     state_mlstm_c, state_mlstm_n, m0_pad, 8)
    xs1 = out_proj([ypool_s, ym_s], [wo0_pool, wo0_m], xs, nbs)

    pool_p = proj_p.reshape(nbp, t, MAIN_W)[:, t - POOL_STATE:, :POOL_W]
    pool_s = jnp.concatenate([state_pool[:, 1:], proj_s[:, None, :POOL_W]], axis=1)

    proj1_p, gates1_p = norm_proj(xp1, norm1_g, w1, wg1, 1024, 512)
    proj1_s, gates1_s = norm_proj(xs1, norm1_g, w1, wg1, nbs, 512)

    tq = 256
    cos_p, sin_p = _rope_tables(jnp.arange(t))
    qrot_p, kvc_p, kvs_p, kvw_p, blocks_p, kvs_pb, kvw_pb = nsa_prep(proj1_p, cos_p, sin_p, tq, t // tq, True)
    oc_p, sel_p = cmp_prompt(proj1_p, blocks_p, nbp, t, tq)
    expand_t = (jnp.arange(LANES)[None, :] == 2 * (jnp.arange(t)[:, None] // SEL_BLOCK)).astype(bf16)
    os_p = attn_prompt(qrot_p, kvs_pb, sel_p, expand_t, nbp, t, tq, "sel")
    ow_p = attn_prompt(qrot_p, kvw_pb, None, None, nbp, t, tq, "win")
    y_p = nsa_out(oc_p, os_p, ow_p, gates1_p, bias1, proj1_p, wo1, xp1, final_g, 256)

    cos_s, sin_s = _rope_tables(jnp.full((nbs,), past_len))
    qrot_s, kvs_s, kvw_s = nsa_prep(proj1_s, cos_s, sin_s, nbs, 1, False)
    pt_flat = page_table.reshape(-1)
    q3_s = proj1_s[:, :NSA_W].reshape(nbs, N_HEADS, HEAD_DIM)
    qrot3_s = qrot_s.reshape(nbs, N_HEADS, HEAD_DIM)
    n_pool = cache_kv_cmp.shape[0]
    page_rows = PAGE_SIZE * KV_ROWS
    oc_s, sel_s = cmp_step(q3_s, cache_kv_cmp.reshape(n_pool, page_rows, HEAD_DIM), pt_flat, n_pages, past_len)
    os_s = sel_step(qrot3_s, sel_s, kvs_s.reshape(nbs, 1, KV_W),
                    cache_kv_sel.reshape(n_pool, page_rows, HEAD_DIM), pt_flat, n_pages)
    ow_s, win_new = win_step(qrot3_s, kvw_s.reshape(nbs, 1, KV_W),
                             cache_kv_win.reshape(nbs, wbuf * KV_ROWS, HEAD_DIM))
    y_s = nsa_out(oc_s.reshape(nbs, NSA_W), os_s.reshape(nbs, NSA_W), ow_s.reshape(nbs, NSA_W),
                  gates1_s, bias1, proj1_s, wo1, xs1, final_g, nbs)

    kv5 = lambda a, rows: a.reshape(-1, rows, 2, N_KV, HEAD_DIM)
    return (y_p.reshape(nbp, t, d), y_s.reshape(nbs, 1, d),
            pool_p, pool_s,
            c_p, c_s, n_p, n_s, m_p[:, 0, :M_HEADS], m_s[:, :M_HEADS],
            kv5(kvc_p, t), kv5(proj1_s[:, 2 * NSA_W:2 * NSA_W + KV_W], 1),
            kv5(kvs_p, t), kv5(kvs_s, 1),
            kv5(kvw_p, t)[:, t - wbuf:],
            kv5(win_new, wbuf))
```

```python
import functools

import jax
import jax.numpy as jnp
from jax import lax
from jax.experimental import pallas as pl
from jax.experimental.pallas import tpu as pltpu

f32 = jnp.float32
bf16 = jnp.bfloat16

D_MODEL = 2048
POOL_WINDOWS = (2, 4, 8, 16)
POOL_W = 1024
POOL_GROUP_W = 256
POOL_STATE = 15
M_HEADS = 4
M_W = 1024
M_HEAD_DIM = 256
N_HEADS = 16
HEAD_DIM = 128
N_KV = 4
Q_PER_KV = 4
NSA_W = 2048
KV_W = 1024
KV_ROWS = 2 * N_KV
CMP_BLOCK = 32
SEL_BLOCK = 64
SEL_TOPK = 16
WINDOW = 512
PAGE_SIZE = 128
ROPE_THETA = 10000.0
ATT_SCALE = HEAD_DIM ** -0.5
EPS = 1e-6
MAIN_W = 7168
LANES = 128
NEG = -1e30
MASK_BIAS = -(2.0 ** 100)
LOG2E = 1.4426950408889634
VMEM_LIMIT = 48 * 1024 * 1024

_NT = (((1,), (1,)), ((), ()))


def _cparams(sem):
    return pltpu.CompilerParams(dimension_semantics=sem, vmem_limit_bytes=VMEM_LIMIT)


def _sigmoid(x):
    return 1.0 / (1.0 + jnp.exp(-x))


def _silu(x):
    return x * _sigmoid(x)


def _log_sigmoid(x):
    return jnp.minimum(x, 0.0) - jnp.log1p(jnp.exp(-jnp.abs(x)))


def _mm(a, b):
    return jnp.dot(a, b, preferred_element_type=f32)


def _mm_nt(a, b):
    return lax.dot_general(a, b, _NT, preferred_element_type=f32)


def _norm_proj_kernel(*refs, starts, seg_ref):
    n_w = max(seg_ref) + 1
    x_ref, g_ref = refs[:2]
    w_refs = refs[2:2 + n_w]
    wg_ref, o_ref, og_ref, h_ref = refs[2 + n_w:]
    j = pl.program_id(1)

    @pl.when(j == 0)
    def _():
        x = x_ref[...]
        r = lax.rsqrt(jnp.mean(x * x, axis=-1, keepdims=True) + EPS)
        h = ((x * r) * g_ref[...]).astype(bf16)
        h_ref[...] = h
        og_ref[...] = _mm(h, wg_ref[...])

    for k, a in enumerate(seg_ref):
        @pl.when((j >= starts[k]) & (j < starts[k + 1]))
        def _(w_ref=w_refs[a]):
            o_ref[...] = _mm(h_ref[...], w_ref[...])


def norm_proj(x, g, segs, wg, tm, tn):
    arrays, seg_list = segs
    m, d = x.shape
    ng = wg.shape[1]
    starts = [0]
    for _, _, cols in seg_list:
        starts.append(starts[-1] + cols // tn)
    n_tiles = starts[-1]

    def w_spec(a):
        def index(i, j):
            idx = None
            for k, (ak, col0, cols) in enumerate(seg_list):
                if ak != a:
                    continue
                here = col0 // tn + jnp.clip(j - starts[k], 0, cols // tn - 1)
                idx = here if idx is None else jnp.where(j >= starts[k], here, idx)
            return (0, idx)
        return pl.BlockSpec((d, tn), index)

    return pl.pallas_call(
        functools.partial(_norm_proj_kernel, starts=tuple(starts), seg_ref=tuple(a for a, _, _ in seg_list)),
        grid=(m // tm, n_tiles),
        in_specs=[pl.BlockSpec((tm, d), lambda i, j: (i, 0)),
                  pl.BlockSpec((1, d), lambda i, j: (0, 0))]
                 + [w_spec(a) for a in range(len(arrays))]
                 + [pl.BlockSpec((d, ng), lambda i, j: (0, 0))],
        out_specs=[pl.BlockSpec((tm, tn), lambda i, j: (i, j)),
                   pl.BlockSpec((tm, ng), lambda i, j: (i, 0))],
        out_shape=[jax.ShapeDtypeStruct((m, n_tiles * tn), f32), jax.ShapeDtypeStruct((m, ng), f32)],
        scratch_shapes=[pltpu.VMEM((tm, d), bf16)],
        compiler_params=_cparams(("parallel", "arbitrary")),
        name="norm_proj",
    )(x, g.reshape(1, d), *arrays, wg)


def _out_proj_kernel(*refs, n_parts):
    a_refs = refs[:n_parts]
    w_refs = refs[n_parts:2 * n_parts]
    x_ref, o_ref = refs[2 * n_parts:]
    acc = x_ref[...]
    for a_ref, w_ref in zip(a_refs, w_refs):
        acc = acc + _mm(a_ref[...], w_ref[...])
    o_ref[...] = acc


def out_proj(parts, weights, x, tm):
    m, d = x.shape
    n_parts = len(parts)
    in_specs = [pl.BlockSpec((tm, a.shape[1]), lambda i: (i, 0)) for a in parts]
    in_specs += [pl.BlockSpec(w.shape, lambda i: (0, 0)) for w in weights]
    in_specs += [pl.BlockSpec((tm, d), lambda i: (i, 0))]
    args = list(parts) + list(weights) + [x]
    return pl.pallas_call(
        functools.partial(_out_proj_kernel, n_parts=n_parts),
        grid=(m // tm,),
        in_specs=in_specs,
        out_specs=pl.BlockSpec((tm, d), lambda i: (i, 0)),
        out_shape=jax.ShapeDtypeStruct((m, d), f32),
        compiler_params=_cparams(("parallel",)),
        name="out_proj",
    )(*args)


def _pool_kernel(u_ref, z_ref, w_ref, sc_ref, o_ref):
    g = pl.program_id(1)
    x = u_ref[...]
    row = lax.broadcasted_iota(jnp.int32, x.shape, 0)

    def back(a, s):
        return jnp.where(row >= s, pltpu.roll(a, s, axis=0), 0.0)

    s2 = x + back(x, 1)
    s4 = s2 + back(s2, 2)
    s8 = s4 + back(s4, 4)
    s16 = s8 + back(s8, 8)
    win = jnp.where(g == 0, s2, jnp.where(g == 1, s4, jnp.where(g == 2, s8, s16)))
    wlen = lax.shift_left(jnp.int32(2), g)
    cnt = jnp.minimum(row + 1, wlen).astype(f32)
    pooled = win / cnt - x
    y = _mm(pooled.astype(bf16), w_ref[0]) * sc_ref[...]
    o_ref[...] = (y * _silu(z_ref[...])).astype(bf16)


def pool_prompt(proj, w_pool, pool_scale, nb, t):
    ng = len(POOL_WINDOWS)
    return pl.pallas_call(
        _pool_kernel,
        grid=(nb, ng),
        in_specs=[pl.BlockSpec((t, POOL_GROUP_W), lambda b, g: (b, g)),
                  pl.BlockSpec((t, POOL_GROUP_W), lambda b, g: (b, ng + g)),
                  pl.BlockSpec((1, POOL_GROUP_W, POOL_GROUP_W), lambda b, g: (g, 0, 0)),
                  pl.BlockSpec((1, POOL_GROUP_W), lambda b, g: (0, g))],
        out_specs=pl.BlockSpec((t, POOL_GROUP_W), lambda b, g: (b, g)),
        out_shape=jax.ShapeDtypeStruct((nb * t, POOL_W), bf16),
        compiler_params=_cparams(("parallel", "arbitrary")),
        name="pool_prompt",
    )(proj, proj, w_pool, pool_scale.reshape(1, POOL_W))


def _pool_step_kernel(st_ref, u_ref, z_ref, w_ref, sc_ref, o_ref):
    u = u_ref[...]
    for g, wlen in enumerate(POOL_WINDOWS):
        lo = g * POOL_GROUP_W
        ug = u[:, lo:lo + POOL_GROUP_W]
        acc = ug
        for r in range(POOL_STATE + 1 - wlen, POOL_STATE):
            acc = acc + st_ref[:, r * POOL_W + lo:r * POOL_W + lo + POOL_GROUP_W]
        pooled = acc / float(wlen) - ug
        y = _mm(pooled.astype(bf16), w_ref[g]) * sc_ref[:, lo:lo + POOL_GROUP_W]
        o_ref[:, lo:lo + POOL_GROUP_W] = (y * _silu(z_ref[:, lo:lo + POOL_GROUP_W])).astype(bf16)


def pool_step(state_flat, proj, w_pool, pool_scale):
    nb = proj.shape[0]
    return pl.pallas_call(
        _pool_step_kernel,
        grid=(1,),
        in_specs=[pl.BlockSpec(state_flat.shape, lambda i: (0, 0)),
                  pl.BlockSpec((nb, POOL_W), lambda i: (0, 0)),
                  pl.BlockSpec((nb, POOL_W), lambda i: (0, 1)),
                  pl.BlockSpec(w_pool.shape, lambda i: (0, 0, 0)),
                  pl.BlockSpec((1, POOL_W), lambda i: (0, 0))],
        out_specs=pl.BlockSpec((nb, POOL_W), lambda i: (0, 0)),
        out_shape=jax.ShapeDtypeStruct((nb, POOL_W), bf16),
        compiler_params=_cparams(("arbitrary",)),
        name="pool_step",
    )(state_flat, proj, proj, w_pool, pool_scale.reshape(1, POOL_W))


def _head_out(hc, o, z, g):
    hc = hc * _sigmoid(o)
    hc = hc * lax.rsqrt(jnp.mean(hc * hc, axis=-1, keepdims=True) + EPS)
    return ((hc * g) * _silu(z)).astype(bf16)


def _mlstm_kernel(q_ref, k_ref, v_ref, o_ref, z_ref, gi_ref, gf_ref, bi_ref, bf_ref, mhg_ref,
                  y_ref, c_ref, n_ref, m_ref):
    @pl.when(pl.program_id(1) == 0)
    def _():
        c_ref[...] = jnp.zeros_like(c_ref)
        n_ref[...] = jnp.zeros_like(n_ref)
        m_ref[...] = jnp.zeros_like(m_ref)

    ln = q_ref.shape[0]
    gi = gi_ref[...] + bi_ref[...]
    lf = _log_sigmoid(gf_ref[...] + bf_ref[...])
    row = lax.broadcasted_iota(jnp.int32, lf.shape, 0)
    b = lf
    s = 1
    while s < ln:
        b = b + jnp.where(row >= s, pltpu.roll(b, s, axis=0), 0.0)
        s *= 2
    r_t = (gi - b).T
    tt = lax.broadcasted_iota(jnp.int32, (ln, ln), 0)
    ss = lax.broadcasted_iota(jnp.int32, (ln, ln), 1)
    causal = ss <= tt
    lane = lax.broadcasted_iota(jnp.int32, (1, LANES), 1)
    m_vec = m_ref[0]
    for h in range(M_HEADS):
        hs = slice(h * M_HEAD_DIM, (h + 1) * M_HEAD_DIM)
        b_col = b[:, h:h + 1]
        ig_col = gi[:, h:h + 1]
        m_prev = m_vec[:, h:h + 1]
        inter = b_col + m_prev
        dmat = jnp.where(causal, b_col + r_t[h:h + 1, :], -jnp.inf)
        m_t = jnp.maximum(inter, jnp.max(dmat, axis=1, keepdims=True))
        dw = jnp.exp(dmat - m_t)
        iw = jnp.exp(inter - m_t)
        q = q_ref[:, hs]
        k = k_ref[:, hs] * (M_HEAD_DIM ** -0.5)
        v = v_ref[:, hs]
        qb, kb, vb = q.astype(bf16), k.astype(bf16), v.astype(bf16)
        c = c_ref[0, h]
        n = n_ref[0, h:h + 1, :]
        qk = _mm_nt(qb, kb) * dw
        num = iw * _mm(qb, c.astype(bf16)) + _mm(qk.astype(bf16), vb)
        den = iw * jnp.sum(q * n, axis=1, keepdims=True) + jnp.sum(qk, axis=1, keepdims=True)
        hc = num / jnp.maximum(jnp.abs(den), jnp.exp(-m_t))
        y_ref[:, hs] = _head_out(hc, o_ref[:, hs], z_ref[:, hs], mhg_ref[:, hs])
        m_last = m_t[ln - 1:ln, :]
        b_last = b_col[ln - 1:ln, :]
        ws = jnp.exp(b_last - b_col + ig_col - m_last)
        dec = jnp.exp(b_last + m_prev - m_last)
        kw = ws * k
        c_ref[0, h] = dec * c + _mm(kw.T.astype(bf16), vb)
        n_ref[0, h:h + 1, :] = dec * n + jnp.sum(kw, axis=0, keepdims=True)
        m_vec = jnp.where(lane == h, m_last, m_vec)
    m_ref[0] = m_vec


def mlstm_prompt(proj, gates, bias_i, bias_f, mh_norm_g, nb, t, ln):
    nc = t // ln
    col = lambda cb: pl.BlockSpec((ln, M_W), lambda b, c: (b * nc + c, cb))
    gcol = lambda cb: pl.BlockSpec((ln, LANES), lambda b, c: (b * nc + c, cb))
    vec = lambda w: pl.BlockSpec((1, w), lambda b, c: (0, 0))
    return pl.pallas_call(
        _mlstm_kernel,
        grid=(nb, nc),
        in_specs=[col(2), col(3), col(4), col(5), col(6), gcol(0), gcol(1), vec(LANES), vec(LANES), vec(M_W)],
        out_specs=[pl.BlockSpec((ln, M_W), lambda b, c: (b * nc + c, 0)),
                   pl.BlockSpec((1, M_HEADS, M_HEAD_DIM, M_HEAD_DIM), lambda b, c: (b, 0, 0, 0)),
                   pl.BlockSpec((1, M_HEADS, M_HEAD_DIM), lambda b, c: (b, 0, 0)),
                   pl.BlockSpec((1, 1, LANES), lambda b, c: (b, 0, 0))],
        out_shape=[jax.ShapeDtypeStruct((nb * t, M_W), bf16),
                   jax.ShapeDtypeStruct((nb, M_HEADS, M_HEAD_DIM, M_HEAD_DIM), f32),
                   jax.ShapeDtypeStruct((nb, M_HEADS, M_HEAD_DIM), f32),
                   jax.ShapeDtypeStruct((nb, 1, LANES), f32)],
        compiler_params=_cparams(("parallel", "arbitrary")),
        name="mlstm_prompt",
    )(proj, proj, proj, proj, proj, gates, gates, bias_i, bias_f, mh_norm_g.reshape(1, M_W))


def _mlstm_step_kernel(q_ref, k_ref, v_ref, o_ref, z_ref, gi_ref, gf_ref, bi_ref, bf_ref, mhg_ref,
                       c_ref, n_ref, m_ref, y_ref, co_ref, no_ref, mo_ref):
    nb = q_ref.shape[0]
    gi = gi_ref[...] + bi_ref[...]
    lf = _log_sigmoid(gf_ref[...] + bf_ref[...])
    inter = lf + m_ref[...]
    m_t = jnp.maximum(inter, gi)
    dw_all = jnp.exp(gi - m_t)
    iw_all = jnp.exp(inter - m_t)
    em_all = jnp.exp(-m_t)
    mo_ref[...] = m_t
    d0 = lax.broadcasted_iota(jnp.int32, (M_HEAD_DIM, M_HEAD_DIM), 0)
    d1 = lax.broadcasted_iota(jnp.int32, (M_HEAD_DIM, M_HEAD_DIM), 1)
    eye = d0 == d1
    for j in range(nb):
        for h in range(M_HEADS):
            hs = slice(h * M_HEAD_DIM, (h + 1) * M_HEAD_DIM)
            dw = dw_all[j:j + 1, h:h + 1]
            iw = iw_all[j:j + 1, h:h + 1]
            em = em_all[j:j + 1, h:h + 1]
            q = q_ref[j:j + 1, hs]
            k = k_ref[j:j + 1, hs] * (M_HEAD_DIM ** -0.5)
            v = v_ref[j:j + 1, hs]
            c = c_ref[j, h]
            n = n_ref[j, h:h + 1, :]
            qc = _mm(jnp.broadcast_to(q, (8, M_HEAD_DIM)).astype(bf16), c.astype(bf16))[0:1, :]
            qk = jnp.sum(q * k, axis=1, keepdims=True) * dw
            num = iw * qc + qk * v
            den = iw * jnp.sum(q * n, axis=1, keepdims=True) + qk
            hc = num / jnp.maximum(jnp.abs(den), em)
            y_ref[j:j + 1, hs] = _head_out(hc, o_ref[j:j + 1, hs], z_ref[j:j + 1, hs], mhg_ref[:, hs])
            kdiag = jnp.where(eye, jnp.broadcast_to(k, (M_HEAD_DIM, M_HEAD_DIM)), 0.0).astype(bf16)
            vrep = jnp.broadcast_to(v, (M_HEAD_DIM, M_HEAD_DIM)).astype(bf16)
            co_ref[j, h] = iw * c + dw * _mm(kdiag, vrep)
            no_ref[j, h:h + 1, :] = iw * n + dw * k


def mlstm_step(proj, gates, bias_i, bias_f, mh_norm_g, c0, n0, m0_pad, bb):
    nb = proj.shape[0]
    col = lambda cb: pl.BlockSpec((bb, M_W), lambda i: (i, cb))
    gcol = lambda cb: pl.BlockSpec((bb, LANES), lambda i: (i, cb))
    vec = lambda w: pl.BlockSpec((1, w), lambda i: (0, 0))
    cspec = pl.BlockSpec((bb, M_HEADS, M_HEAD_DIM, M_HEAD_DIM), lambda i: (i, 0, 0, 0))
    nspec = pl.BlockSpec((bb, M_HEADS, M_HEAD_DIM), lambda i: (i, 0, 0))
    return pl.pallas_call(
        _mlstm_step_kernel,
        grid=(nb // bb,),
        in_specs=[col(2), col(3), col(4), col(5), col(6), gcol(0), gcol(1), vec(LANES), vec(LANES), vec(M_W),
                  cspec, nspec, gcol(0)],
        out_specs=[pl.BlockSpec((bb, M_W), lambda i: (i, 0)), cspec, nspec, gcol(0)],
        out_shape=[jax.ShapeDtypeStruct((nb, M_W), bf16),
                   jax.ShapeDtypeStruct(c0.shape, f32),
                   jax.ShapeDtypeStruct(n0.shape, f32),
                   jax.ShapeDtypeStruct((nb, LANES), f32)],
        compiler_params=_cparams(("parallel",)),
        name="mlstm_step",
    )(proj, proj, proj, proj, proj, gates, gates, bias_i, bias_f, mh_norm_g.reshape(1, M_W), c0, n0, m0_pad)


def _rope(x, cos, sin_signed):
    return x * cos + pltpu.roll(x, HEAD_DIM // 2, axis=1) * sin_signed


def _nsa_prep_kernel(*refs, prompt):
    q_ref, kvc_ref, kvs_ref, kvw_ref, cos_ref, sin_ref, qr_ref = refs[:7]
    tq = q_ref.shape[0]
    cos, sin = cos_ref[...], sin_ref[...]
    for h in range(N_HEADS):
        hs = slice(h * HEAD_DIM, (h + 1) * HEAD_DIM)
        qr_ref[:, hs] = _rope(q_ref[:, hs], cos, sin).astype(bf16)

    def chunks(src, rotate):
        for c in range(KV_ROWS):
            x = src[:, c * HEAD_DIM:(c + 1) * HEAD_DIM]
            yield c, (_rope(x, cos, sin) if rotate and c < N_KV else x)

    if prompt:
        kvc_i, kvs_i, kvw_i, blk_ref, kvs_b, kvw_b = refs[7:13]
        for c, x in chunks(kvc_ref, False):
            kvc_i[pl.ds(c, tq, stride=KV_ROWS), :] = x
        for src, dst_i, dst_b in ((kvs_ref, kvs_i, kvs_b), (kvw_ref, kvw_i, kvw_b)):
            for c, x in chunks(src, True):
                dst_i[pl.ds(c, tq, stride=KV_ROWS), :] = x
                dst_b[:, c * HEAD_DIM:(c + 1) * HEAD_DIM] = x.astype(bf16)
        nblk = tq // CMP_BLOCK
        means = jnp.sum(kvc_ref[...].reshape(nblk, CMP_BLOCK, KV_W), axis=1) * (1.0 / CMP_BLOCK)
        for c in range(KV_ROWS):
            blk_ref[pl.ds(c, nblk, stride=KV_ROWS), :] = means[:, c * HEAD_DIM:(c + 1) * HEAD_DIM]
    else:
        for src, dst in ((kvs_ref, refs[7]), (kvw_ref, refs[8])):
            for c, x in chunks(src, True):
                dst[:, c * HEAD_DIM:(c + 1) * HEAD_DIM] = x


def nsa_prep(proj, cos, sin, tq, n_pos_blocks, prompt):
    m = proj.shape[0]
    row = lambda w, cb: pl.BlockSpec((tq, w), lambda i: (i, cb))
    tab = pl.BlockSpec((tq, HEAD_DIM), lambda i: (i % n_pos_blocks, 0))
    if prompt:
        inter = pl.BlockSpec((tq * KV_ROWS, HEAD_DIM), lambda i: (i, 0))
        inter_shape = jax.ShapeDtypeStruct((m * KV_ROWS, HEAD_DIM), f32)
        out_specs = [row(NSA_W, 0), inter, inter, inter,
                     pl.BlockSpec((tq // CMP_BLOCK * KV_ROWS, HEAD_DIM), lambda i: (i, 0)), row(KV_W, 0), row(KV_W, 0)]
        out_shape = [jax.ShapeDtypeStruct((m, NSA_W), bf16), inter_shape, inter_shape, inter_shape,
                     jax.ShapeDtypeStruct((m // CMP_BLOCK * KV_ROWS, HEAD_DIM), f32),
                     jax.ShapeDtypeStruct((m, KV_W), bf16), jax.ShapeDtypeStruct((m, KV_W), bf16)]
    else:
        out_specs = [row(NSA_W, 0), row(KV_W, 0), row(KV_W, 0)]
        out_shape = [jax.ShapeDtypeStruct((m, NSA_W), bf16), jax.ShapeDtypeStruct((m, KV_W), f32),
                     jax.ShapeDtypeStruct((m, KV_W), f32)]
    return pl.pallas_call(
        functools.partial(_nsa_prep_kernel, prompt=prompt),
        grid=(m // tq,),
        in_specs=[row(NSA_W, 0), row(KV_W, 4), row(KV_W, 5), row(KV_W, 6), tab, tab],
        out_specs=out_specs,
        out_shape=out_shape,
        compiler_params=_cparams(("parallel",)),
        name="nsa_prep",
    )(proj, proj, proj, proj, cos, sin)


def _select_blocks(imp, q_pos, n_cand):
    lane = lax.broadcasted_iota(jnp.int32, imp.shape, 1)
    pair = imp + pltpu.roll(imp, LANES - 1, axis=1)
    cur2 = lax.shift_left(lax.shift_right_logical(q_pos, 6), 1)
    valid = ((lane & 1) == 0) & (lane <= cur2)
    v = jnp.where(lane == cur2, jnp.inf, pair)
    v = jnp.where(valid, v, -jnp.inf)
    cnt = jnp.zeros(imp.shape, f32)
    for i in range(n_cand):
        vi = v[:, 2 * i:2 * i + 1]
        before = jnp.where(lane > 2 * i, 1.0, 0.0)
        cnt = cnt + jnp.where(vi > v, 1.0, 0.0) + jnp.where(vi == v, before, 0.0)
    return jnp.where(valid & (cnt < SEL_TOPK), 1.0, 0.0)


def _pad_rows(x, rows):
    return jnp.concatenate([x, jnp.zeros((rows - x.shape[0], x.shape[1]), x.dtype)], axis=0)


def _masked_softmax(s, mask):
    s = jnp.where(mask, s, -jnp.inf)
    m = jnp.max(s, axis=-1, keepdims=True)
    m = jnp.where(m > -jnp.inf, m, 0.0)
    p = jnp.exp(s - m)
    return p / jnp.maximum(jnp.sum(p, axis=-1, keepdims=True), 1e-30)


def _select_block_rows(pair, q_pos):
    row = lax.broadcasted_iota(jnp.int32, pair.shape, 0)
    cur = lax.shift_right_logical(q_pos, 6)
    valid = row <= cur
    v = jnp.where(row == cur, jnp.inf, pair)
    v = jnp.where(valid, v, -jnp.inf)
    cnt = jnp.zeros(pair.shape, f32)
    for i in range(pair.shape[0]):
        vi = v[i:i + 1, :]
        before = jnp.where(row > i, 1.0, 0.0)
        cnt = cnt + jnp.where(vi > v, 1.0, 0.0) + jnp.where(vi == v, before, 0.0)
    return jnp.where(valid & (cnt < SEL_TOPK), 1.0, 0.0)


def _cmp_prompt_kernel(q_ref, blk_ref, oc_ref, sel_ref, pair_sc, flag_sc, *, n_blocks):
    tq = q_ref.shape[0]
    n_sel = n_blocks // 2
    t0 = pl.program_id(1) * tq
    row = lax.broadcasted_iota(jnp.int32, (LANES, tq), 0)
    q_pos = lax.broadcasted_iota(jnp.int32, (LANES, tq), 1) + t0
    vis = (row < n_blocks) & ((row + 1) * CMP_BLOCK - 1 <= q_pos)
    flag_sc[...] = jnp.zeros_like(flag_sc)

    def block_rows(c):
        return _pad_rows(blk_ref[pl.ds(c, n_blocks, stride=KV_ROWS), :], LANES).astype(bf16)

    for g in range(N_KV):
        kg = block_rows(g)
        vg = block_rows(N_KV + g)
        imp = jnp.zeros((LANES, tq), f32)
        for r in range(Q_PER_KV):
            hs = slice((g * Q_PER_KV + r) * HEAD_DIM, (g * Q_PER_KV + r + 1) * HEAD_DIM)
            s = jnp.where(vis, _mm_nt(kg, q_ref[:, hs].astype(bf16)) * ATT_SCALE, -jnp.inf)
            m = jnp.max(s, axis=0, keepdims=True)
            m = jnp.where(m > -jnp.inf, m, 0.0)
            p = jnp.exp(s - m)
            p = p / jnp.maximum(jnp.sum(p, axis=0, keepdims=True), 1e-30)
            oc_ref[:, hs] = _mm(p.T.astype(bf16), vg).astype(bf16)
            imp = imp + p
        pair = imp + pltpu.roll(imp, LANES - 1, axis=0)
        for h in range(tq // LANES):
            ts = slice(h * LANES, (h + 1) * LANES)
            pos = lax.broadcasted_iota(jnp.int32, (n_sel, LANES), 1) + (t0 + h * LANES)
            pair_sc[h] = pair[:, ts]
            flag_sc[:n_sel, :] = _select_block_rows(pair_sc[h, pl.ds(0, n_sel, stride=2), :], pos)
            sel_ref[ts, g * LANES:(g + 1) * LANES] = flag_sc[...].T.astype(bf16)


def cmp_prompt(proj, blocks, nb, t, tq):
    nq = t // tq
    n_blocks = t // CMP_BLOCK
    return pl.pallas_call(
        functools.partial(_cmp_prompt_kernel, n_blocks=n_blocks),
        grid=(nb, nq),
        in_specs=[pl.BlockSpec((tq, NSA_W), lambda b, i: (b * nq + i, 0)),
                  pl.BlockSpec((n_blocks * KV_ROWS, HEAD_DIM), lambda b, i: (b, 0))],
        out_specs=[pl.BlockSpec((tq, NSA_W), lambda b, i: (b * nq + i, 0)),
                   pl.BlockSpec((tq, N_KV * LANES), lambda b, i: (b * nq + i, 0))],
        out_shape=[jax.ShapeDtypeStruct((nb * t, NSA_W), bf16),
                   jax.ShapeDtypeStruct((nb * t, N_KV * LANES), bf16)],
        scratch_shapes=[pltpu.VMEM((tq // LANES, LANES, LANES), f32), pltpu.VMEM((LANES, LANES), f32)],
        compiler_params=_cparams(("parallel", "parallel")),
        name="cmp_prompt",
    )(proj, blocks)


def _attn_kernel(*refs, mode, tile, reach):
    if mode == "sel":
        q_ref, k_ref, v_ref, sel_ref, et_ref, o_ref, qp_sc, s_sc, mx_sc, mb_sc, acc_sc = refs
    else:
        q_ref, k_ref, v_ref, o_ref, qp_sc, s_sc, mx_sc, mb_sc, acc_sc = refs
    qi = pl.program_id(2)
    rows = Q_PER_KV * tile
    for r in range(Q_PER_KV):
        qh = q_ref[:, r * HEAD_DIM:(r + 1) * HEAD_DIM]
        if mode == "sel":
            bias = ((1.0 - sel_ref[...].astype(f32)) * MASK_BIAS).astype(bf16)
            qh = jnp.concatenate([qh, bias], axis=1)
        qp_sc[r * tile:(r + 1) * tile, :] = qh
    lo = 0 if mode == "sel" else jnp.maximum(qi - reach, 0)

    def scores(kj, n):
        start = pl.multiple_of(kj * tile, tile)
        kt = k_ref[pl.ds(start, n * tile), :]
        if mode == "sel":
            kt = jnp.concatenate([kt, et_ref[pl.ds(start, n * tile), :]], axis=1)
        s = _mm_nt(qp_sc[...], kt)
        if mode == "win" or n == 1:
            t_in = lax.broadcasted_iota(jnp.int32, s.shape, 0) & (tile - 1)
            dpos = t_in - lax.broadcasted_iota(jnp.int32, s.shape, 1) + (qi - kj) * tile
            ok = (dpos >= 0) & (dpos <= WINDOW) if mode == "win" else dpos >= 0
            s = jnp.where(ok, s, MASK_BIAS)
        return s

    def store_scores(kj, n, s):
        mx = mx_sc[...]
        for i in range(n):
            s_sc[kj + i] = s[:, i * tile:(i + 1) * tile]
        for c in range(n * tile // LANES):
            mx = jnp.maximum(mx, s[:, c * LANES:(c + 1) * LANES])
        mx_sc[...] = mx

    def weigh(kj, n):
        start = pl.multiple_of(kj * tile, tile)
        mb = mb_sc[...]
        p = jnp.concatenate([jnp.exp2((s_sc[kj + i][:, c * LANES:(c + 1) * LANES] - mb) * (ATT_SCALE * LOG2E))
                             for i in range(n) for c in range(tile // LANES)], axis=1)
        vt = jnp.concatenate([v_ref[pl.ds(start, n * tile), :], jnp.ones((n * tile, HEAD_DIM), bf16)], axis=1)
        acc_sc[...] += _mm(p.astype(bf16), vt)

    def in_pairs(first, count, fn):
        def trip(i, carry):
            fn(first + 2 * i, 2)
            return carry

        lax.fori_loop(0, count // 2, trip, 0)

        @pl.when(count % 2 == 1)
        def _():
            fn(first + count - 1, 1)

    mx_sc[...] = jnp.full_like(mx_sc, MASK_BIAS)
    if mode == "sel":
        in_pairs(lo, qi - lo, lambda kj, n: store_scores(kj, n, scores(kj, n)))
        store_scores(qi, 1, scores(qi, 1))
    else:
        in_pairs(lo, qi - lo + 1, lambda kj, n: store_scores(kj, n, scores(kj, n)))
    m = jnp.max(mx_sc[...], axis=1, keepdims=True)
    mb_sc[...] = jnp.broadcast_to(m, mb_sc.shape)
    acc_sc[...] = jnp.zeros_like(acc_sc)
    in_pairs(lo, qi - lo + 1, weigh)
    acc = acc_sc[...]
    o = (acc[:, :HEAD_DIM] / acc[:, HEAD_DIM:]).astype(bf16)
    for r in range(Q_PER_KV):
        o_ref[:, r * HEAD_DIM:(r + 1) * HEAD_DIM] = o[r * tile:(r + 1) * tile, :]


def attn_prompt(q_rot, kv_b, sel, expand_t, nb, t, tile, mode):
    nq = t // tile
    reach = -(-WINDOW // tile)
    qw = Q_PER_KV * HEAD_DIM
    kdim = 2 * HEAD_DIM if mode == "sel" else HEAD_DIM
    rows = Q_PER_KV * tile
    in_specs = [pl.BlockSpec((tile, qw), lambda b, g, i: (b * nq + i, g)),
                pl.BlockSpec((t, HEAD_DIM), lambda b, g, i: (b, g)),
                pl.BlockSpec((t, HEAD_DIM), lambda b, g, i: (b, N_KV + g))]
    args = [q_rot, kv_b, kv_b]
    if mode == "sel":
        in_specs += [pl.BlockSpec((tile, LANES), lambda b, g, i: (b * nq + i, g)),
                     pl.BlockSpec((t, LANES), lambda b, g, i: (0, 0))]
        args += [sel, expand_t]
    return pl.pallas_call(
        functools.partial(_attn_kernel, mode=mode, tile=tile, reach=reach),
        grid=(nb, N_KV, nq),
        in_specs=in_specs,
        out_specs=pl.BlockSpec((tile, qw), lambda b, g, i: (b * nq + i, g)),
        out_shape=jax.ShapeDtypeStruct((nb * t, NSA_W), bf16),
        scratch_shapes=[pltpu.VMEM((rows, kdim), bf16),
                        pltpu.VMEM((nq, rows, tile), f32),
                        pltpu.VMEM((rows, LANES), f32),
                        pltpu.VMEM((rows, LANES), f32),
                        pltpu.VMEM((rows, 2 * HEAD_DIM), f32)],
        compiler_params=_cparams(("parallel", "parallel", "arbitrary")),
        name="attn_" + mode,
    )(*args)


def _nsa_out_kernel(oc_ref, os_ref, ow_ref, g_ref, b_ref, z_ref, w_ref, x_ref, fg_ref, o_ref, a_sc):
    gate = _sigmoid(g_ref[...] + b_ref[...])
    for h in range(N_HEADS):
        hs = slice(h * HEAD_DIM, (h + 1) * HEAD_DIM)
        o = (gate[:, 3 * h:3 * h + 1] * oc_ref[:, hs].astype(f32)
             + gate[:, 3 * h + 1:3 * h + 2] * os_ref[:, hs].astype(f32)
             + gate[:, 3 * h + 2:3 * h + 3] * ow_ref[:, hs].astype(f32))
        a_sc[:, hs] = (o * _silu(z_ref[:, hs])).astype(bf16)
    acc = x_ref[...] + _mm(a_sc[...], w_ref[...])
    r = lax.rsqrt(jnp.mean(acc * acc, axis=-1, keepdims=True) + EPS)
    o_ref[...] = (acc * r) * fg_ref[...]


def nsa_out(o_c, o_s, o_w, gates, bias, proj, w_out, x, final_g, tm):
    m, d = x.shape
    row = lambda w, cb: pl.BlockSpec((tm, w), lambda i: (i, cb))
    return pl.pallas_call(
        _nsa_out_kernel,
        grid=(m // tm,),
        in_specs=[row(NSA_W, 0), row(NSA_W, 0), row(NSA_W, 0), row(LANES, 0),
                  pl.BlockSpec((1, LANES), lambda i: (0, 0)),
                  row(NSA_W, 1),
                  pl.BlockSpec(w_out.shape, lambda i: (0, 0)),
                  row(d, 0),
                  pl.BlockSpec((1, d), lambda i: (0, 0))],
        out_specs=row(d, 0),
        out_shape=jax.ShapeDtypeStruct((m, d), f32),
        scratch_shapes=[pltpu.VMEM((tm, NSA_W), bf16)],
        compiler_params=_cparams(("parallel",)),
        name="nsa_out",
    )(o_c, o_s, o_w, gates, bias, proj, w_out, x, final_g.reshape(1, d))


def _head_group(shape):
    return lax.shift_right_logical(lax.broadcasted_iota(jnp.int32, shape, 0), 2)


def _cmp_step_kernel(pt_ref, q_ref, *refs, n_pages, q_pos):
    pages = refs[:n_pages]
    oc_ref, sel_ref, blk_sc = refs[n_pages:]
    per_page = PAGE_SIZE // CMP_BLOCK
    for p in range(n_pages):
        x = pages[p][0].reshape(per_page, CMP_BLOCK, KV_ROWS, HEAD_DIM)
        means = jnp.sum(x, axis=1) * (1.0 / CMP_BLOCK)
        blk_sc[p * per_page * KV_ROWS:(p + 1) * per_page * KV_ROWS, :] = means.reshape(per_page * KV_ROWS, HEAD_DIM)
    n_blocks = blk_sc.shape[0] // KV_ROWS
    qb = q_ref[0].astype(bf16)
    grp = _head_group((N_HEADS, LANES))
    lane = lax.broadcasted_iota(jnp.int32, (N_HEADS, LANES), 1)
    s = jnp.zeros((N_HEADS, LANES), f32)
    for g in range(N_KV):
        kg = _pad_rows(blk_sc[pl.ds(g, n_blocks, stride=KV_ROWS), :], LANES).astype(bf16)
        s = jnp.where(grp == g, _mm_nt(qb, kg), s)
    vis = (lane < n_blocks) & ((lane + 1) * CMP_BLOCK - 1 <= q_pos)
    p = _masked_softmax(s * ATT_SCALE, vis)
    pb = p.astype(bf16)
    o = jnp.zeros((N_HEADS, HEAD_DIM), f32)
    for g in range(N_KV):
        vg = _pad_rows(blk_sc[pl.ds(N_KV + g, n_blocks, stride=KV_ROWS), :], LANES).astype(bf16)
        o = jnp.where(grp == g, _mm(pb, vg), o)
    oc_ref[0] = o
    row8 = lax.broadcasted_iota(jnp.int32, (8, LANES), 0)
    imp = jnp.zeros((8, LANES), f32)
    for g in range(N_KV):
        imp_g = jnp.sum(p[g * Q_PER_KV:(g + 1) * Q_PER_KV, :], axis=0, keepdims=True)
        imp = jnp.where(row8 == g, imp_g, imp)
    sel_ref[0] = _select_blocks(imp, jnp.full((8, LANES), q_pos, jnp.int32), q_pos // SEL_BLOCK + 1)


def _page_specs(n_pages):
    return [pl.BlockSpec((1, PAGE_SIZE * KV_ROWS, HEAD_DIM), lambda b, pt, p=p: (pt[b * n_pages + p], 0, 0))
            for p in range(n_pages)]


def cmp_step(q3, cache, pt_flat, n_pages, q_pos):
    nb = q3.shape[0]
    head_blk = pl.BlockSpec((1, N_HEADS, HEAD_DIM), lambda b, pt: (b, 0, 0))
    return pl.pallas_call(
        functools.partial(_cmp_step_kernel, n_pages=n_pages, q_pos=q_pos),
        grid_spec=pltpu.PrefetchScalarGridSpec(
            num_scalar_prefetch=1,
            grid=(nb,),
            in_specs=[head_blk] + _page_specs(n_pages),
            out_specs=[head_blk, pl.BlockSpec((1, 8, LANES), lambda b, pt: (b, 0, 0))],
            scratch_shapes=[pltpu.VMEM((n_pages * PAGE_SIZE // CMP_BLOCK * KV_ROWS, HEAD_DIM), f32)]),
        out_shape=[jax.ShapeDtypeStruct((nb, N_HEADS, HEAD_DIM), f32),
                   jax.ShapeDtypeStruct((nb, 8, LANES), f32)],
        compiler_params=_cparams(("parallel",)),
        name="cmp_step",
    )(pt_flat, q3, *([cache] * n_pages))


def _decode_attend(qb, n_tiles, kv_tile, flags, new_row, s_sc):
    grp = _head_group((N_HEADS, LANES))
    lane = lax.broadcasted_iota(jnp.int32, (N_HEADS, LANES), 1)
    for p in range(n_tiles):
        sp = jnp.zeros((N_HEADS, LANES), f32)
        for g in range(N_KV):
            sg = _mm_nt(qb, kv_tile(p, g).astype(bf16)) * ATT_SCALE
            if flags is not None:
                f0 = flags[g:g + 1, 4 * p:4 * p + 1]
                f1 = flags[g:g + 1, 4 * p + 2:4 * p + 3]
                sg = jnp.where(jnp.where(lane < SEL_BLOCK, f0, f1) > 0.5, sg, NEG)
            sp = jnp.where(grp == g, sg, sp)
        s_sc[:, p * LANES:(p + 1) * LANES] = sp
    qf = qb.astype(f32)
    s_new = jnp.zeros((N_HEADS, 1), f32)
    grp1 = _head_group((N_HEADS, 1))
    for g in range(N_KV):
        kn = new_row[:, g * HEAD_DIM:(g + 1) * HEAD_DIM].astype(bf16).astype(f32)
        s_new = jnp.where(grp1 == g, jnp.sum(qf * kn, axis=1, keepdims=True) * ATT_SCALE, s_new)
    s_all = s_sc[...]
    m = jnp.maximum(jnp.max(s_all, axis=1, keepdims=True), s_new)
    p_all = jnp.exp(s_all - m)
    p_new = jnp.exp(s_new - m)
    den = jnp.sum(p_all, axis=1, keepdims=True) + p_new
    o = jnp.zeros((N_HEADS, HEAD_DIM), f32)
    for g in range(N_KV):
        vn = new_row[:, KV_W // 2 + g * HEAD_DIM:KV_W // 2 + (g + 1) * HEAD_DIM].astype(bf16).astype(f32)
        o = jnp.where(grp == g, p_new * vn, o)
    for p in range(n_tiles):
        pb = p_all[:, p * LANES:(p + 1) * LANES].astype(bf16)
        for g in range(N_KV):
            o = o + jnp.where(grp == g, _mm(pb, kv_tile(p, N_KV + g).astype(bf16)), 0.0)
    return o / den


def _sel_step_kernel(pt_ref, q_ref, sel_ref, new_ref, *refs, n_pages):
    pages = refs[:n_pages]
    o_ref, s_sc = refs[n_pages:]
    kv_tile = lambda p, c: pages[p][0, pl.ds(c, PAGE_SIZE, stride=KV_ROWS), :]
    o_ref[0] = _decode_attend(q_ref[0], n_pages, kv_tile, sel_ref[0], new_ref[0], s_sc)


def sel_step(q3, sel, new_rows, cache, pt_flat, n_pages):
    nb = q3.shape[0]
    head_blk = pl.BlockSpec((1, N_HEADS, HEAD_DIM), lambda b, pt: (b, 0, 0))
    return pl.pallas_call(
        functools.partial(_sel_step_kernel, n_pages=n_pages),
        grid_spec=pltpu.PrefetchScalarGridSpec(
            num_scalar_prefetch=1,
            grid=(nb,),
            in_specs=[head_blk, pl.BlockSpec((1, 8, LANES), lambda b, pt: (b, 0, 0)),
                      pl.BlockSpec((1, 1, KV_W), lambda b, pt: (b, 0, 0))] + _page_specs(n_pages),
            out_specs=head_blk,
            scratch_shapes=[pltpu.VMEM((N_HEADS, n_pages * PAGE_SIZE), f32)]),
        out_shape=jax.ShapeDtypeStruct((nb, N_HEADS, HEAD_DIM), f32),
        compiler_params=_cparams(("parallel",)),
        name="sel_step",
    )(pt_flat, q3, sel, new_rows, *([cache] * n_pages))


def _win_step_kernel(q_ref, new_ref, new8_ref, win_ref, o_ref, wo_ref, s_sc):
    wrows = win_ref.shape[1]
    n_tiles = wrows // (LANES * KV_ROWS)
    kv_tile = lambda p, c: win_ref[0, pl.ds(p * LANES * KV_ROWS + c, LANES, stride=KV_ROWS), :]
    o_ref[0] = _decode_attend(q_ref[0], n_tiles, kv_tile, None, new_ref[0], s_sc)
    wo_ref[0, :wrows - KV_ROWS, :] = win_ref[0, KV_ROWS:, :]
    wo_ref[0, wrows - KV_ROWS:, :] = new8_ref[0]


def win_step(q3, new_rows, win):
    nb, wrows = win.shape[0], win.shape[1]
    head_blk = pl.BlockSpec((1, N_HEADS, HEAD_DIM), lambda b: (b, 0, 0))
    win_blk = pl.BlockSpec((1, wrows, HEAD_DIM), lambda b: (b, 0, 0))
    return pl.pallas_call(
        _win_step_kernel,
        grid=(nb,),
        in_specs=[head_blk, pl.BlockSpec((1, 1, KV_W), lambda b: (b, 0, 0)),
                  pl.BlockSpec((1, KV_ROWS, HEAD_DIM), lambda b: (b, 0, 0)), win_blk],
        out_specs=[head_blk, win_blk],
        out_shape=[jax.ShapeDtypeStruct((nb, N_HEADS, HEAD_DIM), f32), jax.ShapeDtypeStruct(win.shape, f32)],
        scratch_shapes=[pltpu.VMEM((N_HEADS, wrows // KV_ROWS), f32)],
        compiler_params=_cparams(("parallel",)),
        name="win_step",
    )(q3, new_rows, new_rows.reshape(nb, KV_ROWS, HEAD_DIM), win)


def _rope_tables(pos):
    half = HEAD_DIM // 2
    inv = ROPE_THETA ** (-jnp.arange(half, dtype=f32) / half)
    ang = pos.astype(f32)[:, None] * inv[None, :]
    cos, sin = jnp.cos(ang), jnp.sin(ang)
    return jnp.concatenate([cos, cos], axis=1), jnp.concatenate([-sin, sin], axis=1)


def _pad_cols(a, width):
    return jnp.pad(a, ((0, 0), (0, width - a.shape[1])))


def kernel(x_prompt, x_sample, state_pool, state_mlstm_c, state_mlstm_n, state_mlstm_m, cache_kv_cmp, cache_kv_sel, cache_kv_win, page_table, norm0_g, w_in0, b_gate0, w_pool, pool_scale, mh_norm_g, w_out0, norm1_g, w_in1, b_gate1, w_out1, final_g):
    nbp, t, d = x_prompt.shape
    nbs = x_sample.shape[0]
    mp = nbp * t
    n_pages = page_table.shape[1]
    past_len = n_pages * PAGE_SIZE
    wbuf = cache_kv_win.shape[1]

    w_in0 = w_in0.astype(bf16)
    w_in1 = w_in1.astype(bf16)
    w0 = ([w_in0], [(0, 0, MAIN_W)])
    wg0 = jnp.concatenate([_pad_cols(w_in0[:, MAIN_W:MAIN_W + M_HEADS], LANES),
                           _pad_cols(w_in0[:, MAIN_W + M_HEADS:], LANES)], axis=1)
    bias_i = _pad_cols(b_gate0[None, :M_HEADS], LANES)
    bias_f = _pad_cols(b_gate0[None, M_HEADS:], LANES)
    g_lo = NSA_W + 3 * KV_W
    g_hi = g_lo + 3 * N_HEADS
    w1 = ([w_in1, w_in1[:, g_hi:]], [(0, 0, NSA_W), (1, 0, NSA_W), (0, NSA_W, 3 * KV_W)])
    wg1 = _pad_cols(w_in1[:, g_lo:g_hi], LANES)
    bias1 = _pad_cols(b_gate1[None, :], LANES)
    w_pool_b = w_pool.astype(bf16)
    wo0_pool = w_out0[:POOL_W].astype(bf16)
    wo0_m = w_out0[POOL_W:].astype(bf16)
    wo1 = w_out1.astype(bf16)

    xp = x_prompt.reshape(mp, d)
    xs = x_sample.reshape(nbs, d)

    proj_p, gates_p = norm_proj(xp, norm0_g, w0, wg0, 1024, 1024)
    proj_s, gates_s = norm_proj(xs, norm0_g, w0, wg0, nbs, 1024)

    ypool_p = pool_prompt(proj_p, w_pool_b, pool_scale, nbp, t)
    ym_p, c_p, n_p, m_p = mlstm_prompt(proj_p, gates_p, bias_i, bias_f, mh_norm_g, nbp, t, 256)
    xp1 = out_proj([ypool_p, ym_p], [wo0_pool, wo0_m], xp, 512)

    ypool_s = pool_step(state_pool.reshape(nbs, POOL_STATE * POOL_W), proj_s, w_pool_b, pool_scale)
    m0_pad = _pad_cols(state_mlstm_m, LANES)
    ym_s, c_s, n_s, m_s = mlstm_step(proj_s, gates_s, bias_i, bias_f, mh_norm_g,
                                     state_mlstm_c, state_mlstm_n, m0_pad, 8)
    xs1 = out_proj([ypool_s, ym_s], [wo0_pool, wo0_m], xs, nbs)

    pool_p = proj_p.reshape(nbp, t, MAIN_W)[:, t - POOL_STATE:, :POOL_W]
    pool_s = jnp.concatenate([state_pool[:, 1:], proj_s[:, None, :POOL_W]], axis=1)

    proj1_p, gates1_p = norm_proj(xp1, norm1_g, w1, wg1, 1024, 1024)
    proj1_s, gates1_s = norm_proj(xs1, norm1_g, w1, wg1, nbs, 1024)

    tq = 256
    cos_p, sin_p = _rope_tables(jnp.arange(t))
    qrot_p, kvc_p, kvs_p, kvw_p, blocks_p, kvs_pb, kvw_pb = nsa_prep(proj1_p, cos_p, sin_p, tq, t // tq, True)
    oc_p, sel_p = cmp_prompt(proj1_p, blocks_p, nbp, t, tq)
    expand_t = (jnp.arange(LANES)[None, :] == jnp.arange(t)[:, None] // SEL_BLOCK).astype(bf16)
    os_p = attn_prompt(qrot_p, kvs_pb, sel_p, expand_t, nbp, t, tq, "sel")
    ow_p = attn_prompt(qrot_p, kvw_pb, None, None, nbp, t, tq, "win")
    y_p = nsa_out(oc_p, os_p, ow_p, gates1_p, bias1, proj1_p, wo1, xp1, final_g, 256)

    cos_s, sin_s = _rope_tables(jnp.full((nbs,), past_len))
    qrot_s, kvs_s, kvw_s = nsa_prep(proj1_s, cos_s, sin_s, nbs, 1, False)
    pt_flat = page_table.reshape(-1)
    q3_s = proj1_s[:, :NSA_W].reshape(nbs, N_HEADS, HEAD_DIM)
    qrot3_s = qrot_s.reshape(nbs, N_HEADS, HEAD_DIM)
    n_pool = cache_kv_cmp.shape[0]
    page_rows = PAGE_SIZE * KV_ROWS
    oc_s, sel_s = cmp_step(q3_s, cache_kv_cmp.reshape(n_pool, page_rows, HEAD_DIM), pt_flat, n_pages, past_len)
    os_s = sel_step(qrot3_s, sel_s, kvs_s.reshape(nbs, 1, KV_W),
                    cache_kv_sel.reshape(n_pool, page_rows, HEAD_DIM), pt_flat, n_pages)
    ow_s, win_new = win_step(qrot3_s, kvw_s.reshape(nbs, 1, KV_W),
                             cache_kv_win.reshape(nbs, wbuf * KV_ROWS, HEAD_DIM))
    y_s = nsa_out(oc_s.reshape(nbs, NSA_W), os_s.reshape(nbs, NSA_W), ow_s.reshape(nbs, NSA_W),
                  gates1_s, bias1, proj1_s, wo1, xs1, final_g, nbs)

    kv5 = lambda a, rows: a.reshape(-1, rows, 2, N_KV, HEAD_DIM)
    return (y_p.reshape(nbp, t, d), y_s.reshape(nbs, 1, d),
            pool_p, pool_s,
            c_p, c_s, n_p, n_s, m_p[:, 0, :M_HEADS], m_s[:, :M_HEADS],
            kv5(kvc_p, t), kv5(proj1_s[:, 2 * NSA_W:2 * NSA_W + KV_W], 1),
            kv5(kvs_p, t), kv5(kvs_s, 1),
            kv5(kvw_p, t)[:, t - wbuf:],
            kv5(win_new, wbuf))
```

```python
import functools

import jax
import jax.numpy as jnp
from jax import lax
from jax.experimental import pallas as pl
from jax.experimental.pallas import tpu as pltpu

f32 = jnp.float32
bf16 = jnp.bfloat16

D_MODEL = 2048
POOL_WINDOWS = (2, 4, 8, 16)
POOL_W = 1024
POOL_GROUP_W = 256
POOL_STATE = 15
M_HEADS = 4
M_W = 1024
M_HEAD_DIM = 256
N_HEADS = 16
HEAD_DIM = 128
N_KV = 4
Q_PER_KV = 4
NSA_W = 2048
KV_W = 1024
KV_ROWS = 2 * N_KV
CMP_BLOCK = 32
SEL_BLOCK = 64
SEL_TOPK = 16
WINDOW = 512
PAGE_SIZE = 128
ROPE_THETA = 10000.0
ATT_SCALE = HEAD_DIM ** -0.5
EPS = 1e-6
MAIN_W = 7168
LANES = 128
NEG = -1e30
MASK_BIAS = -(2.0 ** 100)
LOG2E = 1.4426950408889634
VMEM_LIMIT = 48 * 1024 * 1024

_NT = (((1,), (1,)), ((), ()))


def _cparams(sem):
    return pltpu.CompilerParams(dimension_semantics=sem, vmem_limit_bytes=VMEM_LIMIT)


def _sigmoid(x):
    return 1.0 / (1.0 + jnp.exp(-x))


def _silu(x):
    return x * _sigmoid(x)


def _log_sigmoid(x):
    return jnp.minimum(x, 0.0) - jnp.log1p(jnp.exp(-jnp.abs(x)))


def _mm(a, b):
    return jnp.dot(a, b, preferred_element_type=f32)


def _mm_nt(a, b):
    return lax.dot_general(a, b, _NT, preferred_element_type=f32)


def _norm_proj_kernel(*refs, starts, seg_ref):
    n_w = max(seg_ref) + 1
    x_ref, g_ref = refs[:2]
    w_refs = refs[2:2 + n_w]
    wg_ref, o_ref, og_ref, h_ref = refs[2 + n_w:]
    j = pl.program_id(1)

    @pl.when(j == 0)
    def _():
        x = x_ref[...]
        r = lax.rsqrt(jnp.mean(x * x, axis=-1, keepdims=True) + EPS)
        h = ((x * r) * g_ref[...]).astype(bf16)
        h_ref[...] = h
        og_ref[...] = _mm(h, wg_ref[...])

    for k, a in enumerate(seg_ref):
        @pl.when((j >= starts[k]) & (j < starts[k + 1]))
        def _(w_ref=w_refs[a]):
            o_ref[...] = _mm(h_ref[...], w_ref[...])


def norm_proj(x, g, segs, wg, tm, tn):
    arrays, seg_list = segs
    m, d = x.shape
    ng = wg.shape[1]
    starts = [0]
    for _, _, cols in seg_list:
        starts.append(starts[-1] + cols // tn)
    n_tiles = starts[-1]

    def w_spec(a):
        def index(i, j):
            idx = None
            for k, (ak, col0, cols) in enumerate(seg_list):
                if ak != a:
                    continue
                here = col0 // tn + jnp.clip(j - starts[k], 0, cols // tn - 1)
                idx = here if idx is None else jnp.where(j >= starts[k], here, idx)
            return (0, idx)
        return pl.BlockSpec((d, tn), index)

    return pl.pallas_call(
        functools.partial(_norm_proj_kernel, starts=tuple(starts), seg_ref=tuple(a for a, _, _ in seg_list)),
        grid=(m // tm, n_tiles),
        in_specs=[pl.BlockSpec((tm, d), lambda i, j: (i, 0)),
                  pl.BlockSpec((1, d), lambda i, j: (0, 0))]
                 + [w_spec(a) for a in range(len(arrays))]
                 + [pl.BlockSpec((d, ng), lambda i, j: (0, 0))],
        out_specs=[pl.BlockSpec((tm, tn), lambda i, j: (i, j)),
                   pl.BlockSpec((tm, ng), lambda i, j: (i, 0))],
        out_shape=[jax.ShapeDtypeStruct((m, n_tiles * tn), f32), jax.ShapeDtypeStruct((m, ng), f32)],
        scratch_shapes=[pltpu.VMEM((tm, d), bf16)],
        compiler_params=_cparams(("parallel", "arbitrary")),
        name="norm_proj",
    )(x, g.reshape(1, d), *arrays, wg)


def _out_proj_kernel(*refs, n_parts):
    a_refs = refs[:n_parts]
    w_refs = refs[n_parts:2 * n_parts]
    x_ref, o_ref = refs[2 * n_parts:]
    acc = x_ref[...]
    for a_ref, w_ref in zip(a_refs, w_refs):
        acc = acc + _mm(a_ref[...], w_ref[...])
    o_ref[...] = acc


def out_proj(parts, weights, x, tm):
    m, d = x.shape
    n_parts = len(parts)
    in_specs = [pl.BlockSpec((tm, a.shape[1]), lambda i: (i, 0)) for a in parts]
    in_specs += [pl.BlockSpec(w.shape, lambda i: (0, 0)) for w in weights]
    in_specs += [pl.BlockSpec((tm, d), lambda i: (i, 0))]
    args = list(parts) + list(weights) + [x]
    return pl.pallas_call(
        functools.partial(_out_proj_kernel, n_parts=n_parts),
        grid=(m // tm,),
        in_specs=in_specs,
        out_specs=pl.BlockSpec((tm, d), lambda i: (i, 0)),
        out_shape=jax.ShapeDtypeStruct((m, d), f32),
        compiler_params=_cparams(("parallel",)),
        name="out_proj",
    )(*args)


def _pool_kernel(u_ref, z_ref, w_ref, sc_ref, o_ref):
    g = pl.program_id(1)
    x = u_ref[...]
    row = lax.broadcasted_iota(jnp.int32, x.shape, 0)

    def back(a, s):
        return jnp.where(row >= s, pltpu.roll(a, s, axis=0), 0.0)

    s2 = x + back(x, 1)
    s4 = s2 + back(s2, 2)
    s8 = s4 + back(s4, 4)
    s16 = s8 + back(s8, 8)
    win = jnp.where(g == 0, s2, jnp.where(g == 1, s4, jnp.where(g == 2, s8, s16)))
    wlen = lax.shift_left(jnp.int32(2), g)
    cnt = jnp.minimum(row + 1, wlen).astype(f32)
    pooled = win / cnt - x
    y = _mm(pooled.astype(bf16), w_ref[0]) * sc_ref[...]
    o_ref[...] = (y * _silu(z_ref[...])).astype(bf16)


def pool_prompt(proj, w_pool, pool_scale, nb, t):
    ng = len(POOL_WINDOWS)
    return pl.pallas_call(
        _pool_kernel,
        grid=(nb, ng),
        in_specs=[pl.BlockSpec((t, POOL_GROUP_W), lambda b, g: (b, g)),
                  pl.BlockSpec((t, POOL_GROUP_W), lambda b, g: (b, ng + g)),
                  pl.BlockSpec((1, POOL_GROUP_W, POOL_GROUP_W), lambda b, g: (g, 0, 0)),
                  pl.BlockSpec((1, POOL_GROUP_W), lambda b, g: (0, g))],
        out_specs=pl.BlockSpec((t, POOL_GROUP_W), lambda b, g: (b, g)),
        out_shape=jax.ShapeDtypeStruct((nb * t, POOL_W), bf16),
        compiler_params=_cparams(("parallel", "arbitrary")),
        name="pool_prompt",
    )(proj, proj, w_pool, pool_scale.reshape(1, POOL_W))


def _pool_step_kernel(st_ref, u_ref, z_ref, w_ref, sc_ref, o_ref):
    u = u_ref[...]
    for g, wlen in enumerate(POOL_WINDOWS):
        lo = g * POOL_GROUP_W
        ug = u[:, lo:lo + POOL_GROUP_W]
        acc = ug
        for r in range(POOL_STATE + 1 - wlen, POOL_STATE):
            acc = acc + st_ref[:, r * POOL_W + lo:r * POOL_W + lo + POOL_GROUP_W]
        pooled = acc / float(wlen) - ug
        y = _mm(pooled.astype(bf16), w_ref[g]) * sc_ref[:, lo:lo + POOL_GROUP_W]
        o_ref[:, lo:lo + POOL_GROUP_W] = (y * _silu(z_ref[:, lo:lo + POOL_GROUP_W])).astype(bf16)


def pool_step(state_flat, proj, w_pool, pool_scale):
    nb = proj.shape[0]
    return pl.pallas_call(
        _pool_step_kernel,
        grid=(1,),
        in_specs=[pl.BlockSpec(state_flat.shape, lambda i: (0, 0)),
                  pl.BlockSpec((nb, POOL_W), lambda i: (0, 0)),
                  pl.BlockSpec((nb, POOL_W), lambda i: (0, 1)),
                  pl.BlockSpec(w_pool.shape, lambda i: (0, 0, 0)),
                  pl.BlockSpec((1, POOL_W), lambda i: (0, 0))],
        out_specs=pl.BlockSpec((nb, POOL_W), lambda i: (0, 0)),
        out_shape=jax.ShapeDtypeStruct((nb, POOL_W), bf16),
        compiler_params=_cparams(("arbitrary",)),
        name="pool_step",
    )(state_flat, proj, proj, w_pool, pool_scale.reshape(1, POOL_W))


def _head_out(hc, o, z, g):
    hc = hc * _sigmoid(o)
    hc = hc * lax.rsqrt(jnp.mean(hc * hc, axis=-1, keepdims=True) + EPS)
    return ((hc * g) * _silu(z)).astype(bf16)


def _mlstm_kernel(q_ref, k_ref, v_ref, o_ref, z_ref, gi_ref, gf_ref, bi_ref, bf_ref, mhg_ref,
                  y_ref, c_ref, n_ref, m_ref):
    @pl.when(pl.program_id(1) == 0)
    def _():
        c_ref[...] = jnp.zeros_like(c_ref)
        n_ref[...] = jnp.zeros_like(n_ref)
        m_ref[...] = jnp.zeros_like(m_ref)

    ln = q_ref.shape[0]
    gi = gi_ref[...] + bi_ref[...]
    lf = _log_sigmoid(gf_ref[...] + bf_ref[...])
    row = lax.broadcasted_iota(jnp.int32, lf.shape, 0)
    b = lf
    s = 1
    while s < ln:
        b = b + jnp.where(row >= s, pltpu.roll(b, s, axis=0), 0.0)
        s *= 2
    r_t = (gi - b).T
    tt = lax.broadcasted_iota(jnp.int32, (ln, ln), 0)
    ss = lax.broadcasted_iota(jnp.int32, (ln, ln), 1)
    causal = ss <= tt
    lane = lax.broadcasted_iota(jnp.int32, (1, LANES), 1)
    m_vec = m_ref[0]
    for h in range(M_HEADS):
        hs = slice(h * M_HEAD_DIM, (h + 1) * M_HEAD_DIM)
        b_col = b[:, h:h + 1]
        ig_col = gi[:, h:h + 1]
        m_prev = m_vec[:, h:h + 1]
        inter = b_col + m_prev
        dmat = jnp.where(causal, b_col + r_t[h:h + 1, :], -jnp.inf)
        m_t = jnp.maximum(inter, jnp.max(dmat, axis=1, keepdims=True))
        dw = jnp.exp(dmat - m_t)
        iw = jnp.exp(inter - m_t)
        q = q_ref[:, hs]
        k = k_ref[:, hs] * (M_HEAD_DIM ** -0.5)
        v = v_ref[:, hs]
        qb, kb, vb = q.astype(bf16), k.astype(bf16), v.astype(bf16)
        c = c_ref[0, h]
        n = n_ref[0, h:h + 1, :]
        qk = _mm_nt(qb, kb) * dw
        num = iw * _mm(qb, c.astype(bf16)) + _mm(qk.astype(bf16), vb)
        den = iw * jnp.sum(q * n, axis=1, keepdims=True) + jnp.sum(qk, axis=1, keepdims=True)
        hc = num / jnp.maximum(jnp.abs(den), jnp.exp(-m_t))
        y_ref[:, hs] = _head_out(hc, o_ref[:, hs], z_ref[:, hs], mhg_ref[:, hs])
        m_last = m_t[ln - 1:ln, :]
        b_last = b_col[ln - 1:ln, :]
        ws = jnp.exp(b_last - b_col + ig_col - m_last)
        dec = jnp.exp(b_last + m_prev - m_last)
        kw = ws * k
        c_ref[0, h] = dec * c + _mm(kw.T.astype(bf16), vb)
        n_ref[0, h:h + 1, :] = dec * n + jnp.sum(kw, axis=0, keepdims=True)
        m_vec = jnp.where(lane == h, m_last, m_vec)
    m_ref[0] = m_vec


def mlstm_prompt(proj, gates, bias_i, bias_f, mh_norm_g, nb, t, ln):
    nc = t // ln
    col = lambda cb: pl.BlockSpec((ln, M_W), lambda b, c: (b * nc + c, cb))
    gcol = lambda cb: pl.BlockSpec((ln, LANES), lambda b, c: (b * nc + c, cb))
    vec = lambda w: pl.BlockSpec((1, w), lambda b, c: (0, 0))
    return pl.pallas_call(
        _mlstm_kernel,
        grid=(nb, nc),
        in_specs=[col(2), col(3), col(4), col(5), col(6), gcol(0), gcol(1), vec(LANES), vec(LANES), vec(M_W)],
        out_specs=[pl.BlockSpec((ln, M_W), lambda b, c: (b * nc + c, 0)),
                   pl.BlockSpec((1, M_HEADS, M_HEAD_DIM, M_HEAD_DIM), lambda b, c: (b, 0, 0, 0)),
                   pl.BlockSpec((1, M_HEADS, M_HEAD_DIM), lambda b, c: (b, 0, 0)),
                   pl.BlockSpec((1, 1, LANES), lambda b, c: (b, 0, 0))],
        out_shape=[jax.ShapeDtypeStruct((nb * t, M_W), bf16),
                   jax.ShapeDtypeStruct((nb, M_HEADS, M_HEAD_DIM, M_HEAD_DIM), f32),
                   jax.ShapeDtypeStruct((nb, M_HEADS, M_HEAD_DIM), f32),
                   jax.ShapeDtypeStruct((nb, 1, LANES), f32)],
        compiler_params=_cparams(("parallel", "arbitrary")),
        name="mlstm_prompt",
    )(proj, proj, proj, proj, proj, gates, gates, bias_i, bias_f, mh_norm_g.reshape(1, M_W))


def _mlstm_step_kernel(q_ref, k_ref, v_ref, o_ref, z_ref, gi_ref, gf_ref, bi_ref, bf_ref, mhg_ref,
                       c_ref, n_ref, m_ref, y_ref, co_ref, no_ref, mo_ref):
    nb = q_ref.shape[0]
    gi = gi_ref[...] + bi_ref[...]
    lf = _log_sigmoid(gf_ref[...] + bf_ref[...])
    inter = lf + m_ref[...]
    m_t = jnp.maximum(inter, gi)
    dw_all = jnp.exp(gi - m_t)
    iw_all = jnp.exp(inter - m_t)
    em_all = jnp.exp(-m_t)
    mo_ref[...] = m_t
    d0 = lax.broadcasted_iota(jnp.int32, (M_HEAD_DIM, M_HEAD_DIM), 0)
    d1 = lax.broadcasted_iota(jnp.int32, (M_HEAD_DIM, M_HEAD_DIM), 1)
    eye = d0 == d1
    for j in range(nb):
        for h in range(M_HEADS):
            hs = slice(h * M_HEAD_DIM, (h + 1) * M_HEAD_DIM)
            dw = dw_all[j:j + 1, h:h + 1]
            iw = iw_all[j:j + 1, h:h + 1]
            em = em_all[j:j + 1, h:h + 1]
            q = q_ref[j:j + 1, hs]
            k = k_ref[j:j + 1, hs] * (M_HEAD_DIM ** -0.5)
            v = v_ref[j:j + 1, hs]
            c = c_ref[j, h]
            n = n_ref[j, h:h + 1, :]
            qc = _mm(jnp.broadcast_to(q, (8, M_HEAD_DIM)).astype(bf16), c.astype(bf16))[0:1, :]
            qk = jnp.sum(q * k, axis=1, keepdims=True) * dw
            num = iw * qc + qk * v
            den = iw * jnp.sum(q * n, axis=1, keepdims=True) + qk
            hc = num / jnp.maximum(jnp.abs(den), em)
            y_ref[j:j + 1, hs] = _head_out(hc, o_ref[j:j + 1, hs], z_ref[j:j + 1, hs], mhg_ref[:, hs])
            kdiag = jnp.where(eye, jnp.broadcast_to(k, (M_HEAD_DIM, M_HEAD_DIM)), 0.0).astype(bf16)
            vrep = jnp.broadcast_to(v, (M_HEAD_DIM, M_HEAD_DIM)).astype(bf16)
            co_ref[j, h] = iw * c + dw * _mm(kdiag, vrep)
            no_ref[j, h:h + 1, :] = iw * n + dw * k


def mlstm_step(proj, gates, bias_i, bias_f, mh_norm_g, c0, n0, m0_pad, bb):
    nb = proj.shape[0]
    col = lambda cb: pl.BlockSpec((bb, M_W), lambda i: (i, cb))
    gcol = lambda cb: pl.BlockSpec((bb, LANES), lambda i: (i, cb))
    vec = lambda w: pl.BlockSpec((1, w), lambda i: (0, 0))
    cspec = pl.BlockSpec((bb, M_HEADS, M_HEAD_DIM, M_HEAD_DIM), lambda i: (i, 0, 0, 0))
    nspec = pl.BlockSpec((bb, M_HEADS, M_HEAD_DIM), lambda i: (i, 0, 0))
    return pl.pallas_call(
        _mlstm_step_kernel,
        grid=(nb // bb,),
        in_specs=[col(2), col(3), col(4), col(5), col(6), gcol(0), gcol(1), vec(LANES), vec(LANES), vec(M_W),
                  cspec, nspec, gcol(0)],
        out_specs=[pl.BlockSpec((bb, M_W), lambda i: (i, 0)), cspec, nspec, gcol(0)],
        out_shape=[jax.ShapeDtypeStruct((nb, M_W), bf16),
                   jax.ShapeDtypeStruct(c0.shape, f32),
                   jax.ShapeDtypeStruct(n0.shape, f32),
                   jax.ShapeDtypeStruct((nb, LANES), f32)],
        compiler_params=_cparams(("parallel",)),
        name="mlstm_step",
    )(proj, proj, proj, proj, proj, gates, gates, bias_i, bias_f, mh_norm_g.reshape(1, M_W), c0, n0, m0_pad)


def _rope(x, cos, sin_signed):
    return x * cos + pltpu.roll(x, HEAD_DIM // 2, axis=1) * sin_signed


def _nsa_prep_kernel(*refs, prompt):
    q_ref, kvc_ref, kvs_ref, kvw_ref, cos_ref, sin_ref, qr_ref = refs[:7]
    tq = q_ref.shape[0]
    cos, sin = cos_ref[...], sin_ref[...]
    for h in range(N_HEADS):
        hs = slice(h * HEAD_DIM, (h + 1) * HEAD_DIM)
        qr_ref[:, hs] = _rope(q_ref[:, hs], cos, sin).astype(bf16)

    def chunks(src, rotate):
        for c in range(KV_ROWS):
            x = src[:, c * HEAD_DIM:(c + 1) * HEAD_DIM]
            yield c, (_rope(x, cos, sin) if rotate and c < N_KV else x)

    if prompt:
        kvc_i, kvs_i, kvw_i, blk_ref, kvs_b, kvw_b = refs[7:13]
        for c, x in chunks(kvc_ref, False):
            kvc_i[pl.ds(c, tq, stride=KV_ROWS), :] = x
        for src, dst_i, dst_b in ((kvs_ref, kvs_i, kvs_b), (kvw_ref, kvw_i, kvw_b)):
            for c, x in chunks(src, True):
                dst_i[pl.ds(c, tq, stride=KV_ROWS), :] = x
                dst_b[:, c * HEAD_DIM:(c + 1) * HEAD_DIM] = x.astype(bf16)
        nblk = tq // CMP_BLOCK
        means = jnp.sum(kvc_ref[...].reshape(nblk, CMP_BLOCK, KV_W), axis=1) * (1.0 / CMP_BLOCK)
        for c in range(KV_ROWS):
            blk_ref[pl.ds(c, nblk, stride=KV_ROWS), :] = means[:, c * HEAD_DIM:(c + 1) * HEAD_DIM]
    else:
        for src, dst in ((kvs_ref, refs[7]), (kvw_ref, refs[8])):
            for c, x in chunks(src, True):
                dst[:, c * HEAD_DIM:(c + 1) * HEAD_DIM] = x


def nsa_prep(proj, cos, sin, tq, n_pos_blocks, prompt):
    m = proj.shape[0]
    row = lambda w, cb: pl.BlockSpec((tq, w), lambda i: (i, cb))
    tab = pl.BlockSpec((tq, HEAD_DIM), lambda i: (i % n_pos_blocks, 0))
    if prompt:
        inter = pl.BlockSpec((tq * KV_ROWS, HEAD_DIM), lambda i: (i, 0))
        inter_shape = jax.ShapeDtypeStruct((m * KV_ROWS, HEAD_DIM), f32)
        out_specs = [row(NSA_W, 0), inter, inter, inter,
                     pl.BlockSpec((tq // CMP_BLOCK * KV_ROWS, HEAD_DIM), lambda i: (i, 0)), row(KV_W, 0), row(KV_W, 0)]
        out_shape = [jax.ShapeDtypeStruct((m, NSA_W), bf16), inter_shape, inter_shape, inter_shape,
                     jax.ShapeDtypeStruct((m // CMP_BLOCK * KV_ROWS, HEAD_DIM), f32),
                     jax.ShapeDtypeStruct((m, KV_W), bf16), jax.ShapeDtypeStruct((m, KV_W), bf16)]
    else:
        out_specs = [row(NSA_W, 0), row(KV_W, 0), row(KV_W, 0)]
        out_shape = [jax.ShapeDtypeStruct((m, NSA_W), bf16), jax.ShapeDtypeStruct((m, KV_W), f32),
                     jax.ShapeDtypeStruct((m, KV_W), f32)]
    return pl.pallas_call(
        functools.partial(_nsa_prep_kernel, prompt=prompt),
        grid=(m // tq,),
        in_specs=[row(NSA_W, 0), row(KV_W, 4), row(KV_W, 5), row(KV_W, 6), tab, tab],
        out_specs=out_specs,
        out_shape=out_shape,
        compiler_params=_cparams(("parallel",)),
        name="nsa_prep",
    )(proj, proj, proj, proj, cos, sin)


def _select_blocks(imp, q_pos, n_cand):
    lane = lax.broadcasted_iota(jnp.int32, imp.shape, 1)
    pair = imp + pltpu.roll(imp, LANES - 1, axis=1)
    cur2 = lax.shift_left(lax.shift_right_logical(q_pos, 6), 1)
    valid = ((lane & 1) == 0) & (lane <= cur2)
    v = jnp.where(lane == cur2, jnp.inf, pair)
    v = jnp.where(valid, v, -jnp.inf)
    cnt = jnp.zeros(imp.shape, f32)
    for i in range(n_cand):
        vi = v[:, 2 * i:2 * i + 1]
        before = jnp.where(lane > 2 * i, 1.0, 0.0)
        cnt = cnt + jnp.where(vi > v, 1.0, 0.0) + jnp.where(vi == v, before, 0.0)
    return jnp.where(valid & (cnt < SEL_TOPK), 1.0, 0.0)


def _pad_rows(x, rows):
    return jnp.concatenate([x, jnp.zeros((rows - x.shape[0], x.shape[1]), x.dtype)], axis=0)


def _masked_softmax(s, mask):
    s = jnp.where(mask, s, -jnp.inf)
    m = jnp.max(s, axis=-1, keepdims=True)
    m = jnp.where(m > -jnp.inf, m, 0.0)
    p = jnp.exp(s - m)
    return p / jnp.maximum(jnp.sum(p, axis=-1, keepdims=True), 1e-30)


def _select_block_rows(pair, q_pos):
    row = lax.broadcasted_iota(jnp.int32, pair.shape, 0)
    cur = lax.shift_right_logical(q_pos, 6)
    valid = row <= cur
    v = jnp.where(row == cur, jnp.inf, pair)
    v = jnp.where(valid, v, -jnp.inf)
    cnt = jnp.zeros(pair.shape, f32)
    for i in range(pair.shape[0]):
        vi = v[i:i + 1, :]
        before = jnp.where(row > i, 1.0, 0.0)
        cnt = cnt + jnp.where(vi > v, 1.0, 0.0) + jnp.where(vi == v, before, 0.0)
    return jnp.where(valid & (cnt < SEL_TOPK), 1.0, 0.0)


def _cmp_prompt_kernel(q_ref, blk_ref, oc_ref, sel_ref, pair_sc, flag_sc, *, n_blocks):
    tq = q_ref.shape[0]
    n_sel = n_blocks // 2
    t0 = pl.program_id(1) * tq
    row = lax.broadcasted_iota(jnp.int32, (LANES, tq), 0)
    q_pos = lax.broadcasted_iota(jnp.int32, (LANES, tq), 1) + t0
    vis = (row < n_blocks) & ((row + 1) * CMP_BLOCK - 1 <= q_pos)
    flag_sc[...] = jnp.zeros_like(flag_sc)

    def block_rows(c):
        return _pad_rows(blk_ref[pl.ds(c, n_blocks, stride=KV_ROWS), :], LANES).astype(bf16)

    for g in range(N_KV):
        kg = block_rows(g)
        vg = block_rows(N_KV + g)
        imp = jnp.zeros((LANES, tq), f32)
        for r in range(Q_PER_KV):
            hs = slice((g * Q_PER_KV + r) * HEAD_DIM, (g * Q_PER_KV + r + 1) * HEAD_DIM)
            s = jnp.where(vis, _mm_nt(kg, q_ref[:, hs].astype(bf16)) * ATT_SCALE, -jnp.inf)
            m = jnp.max(s, axis=0, keepdims=True)
            m = jnp.where(m > -jnp.inf, m, 0.0)
            p = jnp.exp(s - m)
            p = p / jnp.maximum(jnp.sum(p, axis=0, keepdims=True), 1e-30)
            oc_ref[:, hs] = _mm(p.T.astype(bf16), vg).astype(bf16)
            imp = imp + p
        pair = imp + pltpu.roll(imp, LANES - 1, axis=0)
        for h in range(tq // LANES):
            ts = slice(h * LANES, (h + 1) * LANES)
            pos = lax.broadcasted_iota(jnp.int32, (n_sel, LANES), 1) + (t0 + h * LANES)
            pair_sc[h] = pair[:, ts]
            flag_sc[:n_sel, :] = _select_block_rows(pair_sc[h, pl.ds(0, n_sel, stride=2), :], pos)
            sel_ref[ts, g * LANES:(g + 1) * LANES] = flag_sc[...].T.astype(bf16)


def cmp_prompt(proj, blocks, nb, t, tq):
    nq = t // tq
    n_blocks = t // CMP_BLOCK
    return pl.pallas_call(
        functools.partial(_cmp_prompt_kernel, n_blocks=n_blocks),
        grid=(nb, nq),
        in_specs=[pl.BlockSpec((tq, NSA_W), lambda b, i: (b * nq + i, 0)),
                  pl.BlockSpec((n_blocks * KV_ROWS, HEAD_DIM), lambda b, i: (b, 0))],
        out_specs=[pl.BlockSpec((tq, NSA_W), lambda b, i: (b * nq + i, 0)),
                   pl.BlockSpec((tq, N_KV * LANES), lambda b, i: (b * nq + i, 0))],
        out_shape=[jax.ShapeDtypeStruct((nb * t, NSA_W), bf16),
                   jax.ShapeDtypeStruct((nb * t, N_KV * LANES), bf16)],
        scratch_shapes=[pltpu.VMEM((tq // LANES, LANES, LANES), f32), pltpu.VMEM((LANES, LANES), f32)],
        compiler_params=_cparams(("parallel", "parallel")),
        name="cmp_prompt",
    )(proj, blocks)


def _attn_kernel(*refs, mode, tile, reach):
    if mode == "sel":
        q_ref, k_ref, v_ref, sel_ref, et_ref, o_ref, qp_sc, s_sc, mx_sc, mb_sc, acc_sc = refs
    else:
        q_ref, k_ref, v_ref, o_ref, qp_sc, s_sc, mx_sc, mb_sc, acc_sc = refs
    qi = pl.program_id(2)
    rows = Q_PER_KV * tile
    for r in range(Q_PER_KV):
        qh = q_ref[:, r * HEAD_DIM:(r + 1) * HEAD_DIM]
        if mode == "sel":
            bias = ((1.0 - sel_ref[...].astype(f32)) * MASK_BIAS).astype(bf16)
            qh = jnp.concatenate([qh, bias], axis=1)
        qp_sc[r * tile:(r + 1) * tile, :] = qh
    lo = 0 if mode == "sel" else jnp.maximum(qi - reach, 0)

    def scores(kj, n):
        start = pl.multiple_of(kj * tile, tile)
        kt = k_ref[pl.ds(start, n * tile), :]
        if mode == "sel":
            kt = jnp.concatenate([kt, et_ref[pl.ds(start, n * tile), :]], axis=1)
        s = _mm_nt(qp_sc[...], kt)
        if mode == "win" or n == 1:
            t_in = lax.broadcasted_iota(jnp.int32, s.shape, 0) & (tile - 1)
            dpos = t_in - lax.broadcasted_iota(jnp.int32, s.shape, 1) + (qi - kj) * tile
            ok = (dpos >= 0) & (dpos <= WINDOW) if mode == "win" else dpos >= 0
            s = jnp.where(ok, s, MASK_BIAS)
        return s

    def store_scores(kj, n, s):
        mx = mx_sc[...]
        for i in range(n):
            s_sc[kj + i] = s[:, i * tile:(i + 1) * tile]
        for c in range(n * tile // LANES):
            mx = jnp.maximum(mx, s[:, c * LANES:(c + 1) * LANES])
        mx_sc[...] = mx

    def weigh(kj, n):
        start = pl.multiple_of(kj * tile, tile)
        mb = mb_sc[...]
        p = jnp.concatenate([jnp.exp2((s_sc[kj + i][:, c * LANES:(c + 1) * LANES] - mb) * (ATT_SCALE * LOG2E))
                             for i in range(n) for c in range(tile // LANES)], axis=1)
        vt = jnp.concatenate([v_ref[pl.ds(start, n * tile), :], jnp.ones((n * tile, HEAD_DIM), bf16)], axis=1)
        acc_sc[...] += _mm(p.astype(bf16), vt)

    def in_pairs(first, count, fn):
        def trip(i, carry):
            fn(first + 2 * i, 2)
            return carry

        lax.fori_loop(0, count // 2, trip, 0)

        @pl.when(count % 2 == 1)
        def _():
            fn(first + count - 1, 1)

    mx_sc[...] = jnp.full_like(mx_sc, MASK_BIAS)
    if mode == "sel":
        in_pairs(lo, qi - lo, lambda kj, n: store_scores(kj, n, scores(kj, n)))
        store_scores(qi, 1, scores(qi, 1))
    else:
        in_pairs(lo, qi - lo + 1, lambda kj, n: store_scores(kj, n, scores(kj, n)))
    m = jnp.max(mx_sc[...], axis=1, keepdims=True)
    mb_sc[...] = jnp.broadcast_to(m, mb_sc.shape)
    acc_sc[...] = jnp.zeros_like(acc_sc)
    in_pairs(lo, qi - lo + 1, weigh)
    acc = acc_sc[...]
    o = (acc[:, :HEAD_DIM] / acc[:, HEAD_DIM:]).astype(bf16)
    for r in range(Q_PER_KV):
        o_ref[:, r * HEAD_DIM:(r + 1) * HEAD_DIM] = o[r * tile:(r + 1) * tile, :]


def _attn_step_kernel(pt_ref, *refs, counts, attn_body, step_body):
    na_in, ns_in, na_out, ns_out, na_sc = counts
    i0 = na_in + ns_in
    o0 = i0 + na_out + ns_out
    attn_body(*refs[:na_in], *refs[i0:i0 + na_out], *refs[o0:o0 + na_sc])
    step_body(pt_ref, *refs[na_in:i0], *refs[i0 + na_out:o0], *refs[o0 + na_sc:])


def attn_prompt_with_step(q_rot, kv_b, sel, expand_t, nb, t, tile, mode, step):
    nq = t // tile
    reach = -(-WINDOW // tile)
    qw = Q_PER_KV * HEAD_DIM
    kdim = 2 * HEAD_DIM if mode == "sel" else HEAD_DIM
    rows = Q_PER_KV * tile
    in_specs = [pl.BlockSpec((tile, qw), lambda b, g, i, pt: (b * nq + i, g)),
                pl.BlockSpec((t, HEAD_DIM), lambda b, g, i, pt: (b, g)),
                pl.BlockSpec((t, HEAD_DIM), lambda b, g, i, pt: (b, N_KV + g))]
    args = [q_rot, kv_b, kv_b]
    if mode == "sel":
        in_specs += [pl.BlockSpec((tile, LANES), lambda b, g, i, pt: (b * nq + i, g)),
                     pl.BlockSpec((t, LANES), lambda b, g, i, pt: (0, 0))]
        args += [sel, expand_t]
    scratch = [pltpu.VMEM((rows, kdim), bf16), pltpu.VMEM((nq, rows, tile), f32), pltpu.VMEM((rows, LANES), f32),
               pltpu.VMEM((rows, LANES), f32), pltpu.VMEM((rows, 2 * HEAD_DIM), f32)]
    step_body, pt_flat, s_args, s_in, s_out, s_shape, s_scratch = step(lambda b, g, i: (b * N_KV + g) * nq + i)
    assert s_shape[0].shape[0] == nb * N_KV * nq, "one sample per attention grid step"
    counts = (len(in_specs), len(s_in), 1, len(s_out), len(scratch))
    return pl.pallas_call(
        functools.partial(_attn_step_kernel, counts=counts, step_body=step_body,
                          attn_body=functools.partial(_attn_kernel, mode=mode, tile=tile, reach=reach)),
        grid_spec=pltpu.PrefetchScalarGridSpec(
            num_scalar_prefetch=1,
            grid=(nb, N_KV, nq),
            in_specs=in_specs + s_in,
            out_specs=[pl.BlockSpec((tile, qw), lambda b, g, i, pt: (b * nq + i, g))] + s_out,
            scratch_shapes=scratch + s_scratch),
        out_shape=[jax.ShapeDtypeStruct((nb * t, NSA_W), bf16)] + s_shape,
        compiler_params=_cparams(("parallel", "parallel", "arbitrary")),
        name="attn_" + mode + "_step",
    )(pt_flat, *args, *s_args)


def _nsa_out_kernel(oc_ref, os_ref, ow_ref, g_ref, b_ref, z_ref, w_ref, x_ref, fg_ref, o_ref, a_sc):
    gate = _sigmoid(g_ref[...] + b_ref[...])
    for h in range(N_HEADS):
        hs = slice(h * HEAD_DIM, (h + 1) * HEAD_DIM)
        o = (gate[:, 3 * h:3 * h + 1] * oc_ref[:, hs].astype(f32)
             + gate[:, 3 * h + 1:3 * h + 2] * os_ref[:, hs].astype(f32)
             + gate[:, 3 * h + 2:3 * h + 3] * ow_ref[:, hs].astype(f32))
        a_sc[:, hs] = (o * _silu(z_ref[:, hs])).astype(bf16)
    acc = x_ref[...] + _mm(a_sc[...], w_ref[...])
    r = lax.rsqrt(jnp.mean(acc * acc, axis=-1, keepdims=True) + EPS)
    o_ref[...] = (acc * r) * fg_ref[...]


def nsa_out(o_c, o_s, o_w, gates, bias, proj, w_out, x, final_g, tm):
    m, d = x.shape
    row = lambda w, cb: pl.BlockSpec((tm, w), lambda i: (i, cb))
    return pl.pallas_call(
        _nsa_out_kernel,
        grid=(m // tm,),
        in_specs=[row(NSA_W, 0), row(NSA_W, 0), row(NSA_W, 0), row(LANES, 0),
                  pl.BlockSpec((1, LANES), lambda i: (0, 0)),
                  row(NSA_W, 1),
                  pl.BlockSpec(w_out.shape, lambda i: (0, 0)),
                  row(d, 0),
                  pl.BlockSpec((1, d), lambda i: (0, 0))],
        out_specs=row(d, 0),
        out_shape=jax.ShapeDtypeStruct((m, d), f32),
        scratch_shapes=[pltpu.VMEM((tm, NSA_W), bf16)],
        compiler_params=_cparams(("parallel",)),
        name="nsa_out",
    )(o_c, o_s, o_w, gates, bias, proj, w_out, x, final_g.reshape(1, d))


def _head_group(shape):
    return lax.shift_right_logical(lax.broadcasted_iota(jnp.int32, shape, 0), 2)


def _cmp_step_kernel(pt_ref, q_ref, *refs, n_pages, q_pos):
    pages = refs[:n_pages]
    oc_ref, sel_ref, blk_sc = refs[n_pages:]
    per_page = PAGE_SIZE // CMP_BLOCK
    for p in range(n_pages):
        x = pages[p][0].reshape(per_page, CMP_BLOCK, KV_ROWS, HEAD_DIM)
        means = jnp.sum(x, axis=1) * (1.0 / CMP_BLOCK)
        blk_sc[p * per_page * KV_ROWS:(p + 1) * per_page * KV_ROWS, :] = means.reshape(per_page * KV_ROWS, HEAD_DIM)
    n_blocks = blk_sc.shape[0] // KV_ROWS
    qb = q_ref[0].astype(bf16)
    grp = _head_group((N_HEADS, LANES))
    lane = lax.broadcasted_iota(jnp.int32, (N_HEADS, LANES), 1)
    s = jnp.zeros((N_HEADS, LANES), f32)
    for g in range(N_KV):
        kg = _pad_rows(blk_sc[pl.ds(g, n_blocks, stride=KV_ROWS), :], LANES).astype(bf16)
        s = jnp.where(grp == g, _mm_nt(qb, kg), s)
    vis = (lane < n_blocks) & ((lane + 1) * CMP_BLOCK - 1 <= q_pos)
    p = _masked_softmax(s * ATT_SCALE, vis)
    pb = p.astype(bf16)
    o = jnp.zeros((N_HEADS, HEAD_DIM), f32)
    for g in range(N_KV):
        vg = _pad_rows(blk_sc[pl.ds(N_KV + g, n_blocks, stride=KV_ROWS), :], LANES).astype(bf16)
        o = jnp.where(grp == g, _mm(pb, vg), o)
    oc_ref[0] = o
    row8 = lax.broadcasted_iota(jnp.int32, (8, LANES), 0)
    imp = jnp.zeros((8, LANES), f32)
    for g in range(N_KV):
        imp_g = jnp.sum(p[g * Q_PER_KV:(g + 1) * Q_PER_KV, :], axis=0, keepdims=True)
        imp = jnp.where(row8 == g, imp_g, imp)
    sel_ref[0] = _select_blocks(imp, jnp.full((8, LANES), q_pos, jnp.int32), q_pos // SEL_BLOCK + 1)


def _sample_specs(sample, n_pages):
    per = lambda *shape: pl.BlockSpec((1,) + shape, lambda *a: (sample(*a[:-1]), 0, 0))
    pages = [pl.BlockSpec((1, PAGE_SIZE * KV_ROWS, HEAD_DIM),
                          lambda *a, p=p: (a[-1][sample(*a[:-1]) * n_pages + p], 0, 0)) for p in range(n_pages)]
    return per, pages


def cmp_step(q3, cache, pt_flat, n_pages, q_pos):
    def build(sample):
        nb = q3.shape[0]
        per, pages = _sample_specs(sample, n_pages)
        return (functools.partial(_cmp_step_kernel, n_pages=n_pages, q_pos=q_pos), pt_flat,
                [q3] + [cache] * n_pages,
                [per(N_HEADS, HEAD_DIM)] + pages,
                [per(N_HEADS, HEAD_DIM), per(8, LANES)],
                [jax.ShapeDtypeStruct((nb, N_HEADS, HEAD_DIM), f32), jax.ShapeDtypeStruct((nb, 8, LANES), f32)],
                [pltpu.VMEM((n_pages * PAGE_SIZE // CMP_BLOCK * KV_ROWS, HEAD_DIM), f32)])
    return build


def _decode_attend(qb, n_tiles, kv_tile, flags, new_row, s_sc):
    grp = _head_group((N_HEADS, LANES))
    lane = lax.broadcasted_iota(jnp.int32, (N_HEADS, LANES), 1)
    for p in range(n_tiles):
        sp = jnp.zeros((N_HEADS, LANES), f32)
        for g in range(N_KV):
            sg = _mm_nt(qb, kv_tile(p, g).astype(bf16)) * ATT_SCALE
            if flags is not None:
                f0 = flags[g:g + 1, 4 * p:4 * p + 1]
                f1 = flags[g:g + 1, 4 * p + 2:4 * p + 3]
                sg = jnp.where(jnp.where(lane < SEL_BLOCK, f0, f1) > 0.5, sg, NEG)
            sp = jnp.where(grp == g, sg, sp)
        s_sc[:, p * LANES:(p + 1) * LANES] = sp
    qf = qb.astype(f32)
    s_new = jnp.zeros((N_HEADS, 1), f32)
    grp1 = _head_group((N_HEADS, 1))
    for g in range(N_KV):
        kn = new_row[:, g * HEAD_DIM:(g + 1) * HEAD_DIM].astype(bf16).astype(f32)
        s_new = jnp.where(grp1 == g, jnp.sum(qf * kn, axis=1, keepdims=True) * ATT_SCALE, s_new)
    s_all = s_sc[...]
    m = jnp.maximum(jnp.max(s_all, axis=1, keepdims=True), s_new)
    p_all = jnp.exp(s_all - m)
    p_new = jnp.exp(s_new - m)
    den = jnp.sum(p_all, axis=1, keepdims=True) + p_new
    o = jnp.zeros((N_HEADS, HEAD_DIM), f32)
    for g in range(N_KV):
        vn = new_row[:, KV_W // 2 + g * HEAD_DIM:KV_W // 2 + (g + 1) * HEAD_DIM].astype(bf16).astype(f32)
        o = jnp.where(grp == g, p_new * vn, o)
    for p in range(n_tiles):
        pb = p_all[:, p * LANES:(p + 1) * LANES].astype(bf16)
        for g in range(N_KV):
            o = o + jnp.where(grp == g, _mm(pb, kv_tile(p, N_KV + g).astype(bf16)), 0.0)
    return o / den


def _sel_step_kernel(pt_ref, q_ref, sel_ref, new_ref, *refs, n_pages):
    pages = refs[:n_pages]
    o_ref, s_sc = refs[n_pages:]
    kv_tile = lambda p, c: pages[p][0, pl.ds(c, PAGE_SIZE, stride=KV_ROWS), :]
    o_ref[0] = _decode_attend(q_ref[0], n_pages, kv_tile, sel_ref[0], new_ref[0], s_sc)


def sel_step(q3, sel, new_rows, cache, pt_flat, n_pages):
    def build(sample):
        nb = q3.shape[0]
        per, pages = _sample_specs(sample, n_pages)
        return (functools.partial(_sel_step_kernel, n_pages=n_pages), pt_flat,
                [q3, sel, new_rows] + [cache] * n_pages,
                [per(N_HEADS, HEAD_DIM), per(8, LANES), per(1, KV_W)] + pages,
                [per(N_HEADS, HEAD_DIM)],
                [jax.ShapeDtypeStruct((nb, N_HEADS, HEAD_DIM), f32)],
                [pltpu.VMEM((N_HEADS, n_pages * PAGE_SIZE), f32)])
    return build


def _win_step_kernel(q_ref, new_ref, new8_ref, win_ref, o_ref, wo_ref, s_sc):
    wrows = win_ref.shape[1]
    n_tiles = wrows // (LANES * KV_ROWS)
    kv_tile = lambda p, c: win_ref[0, pl.ds(p * LANES * KV_ROWS + c, LANES, stride=KV_ROWS), :]
    o_ref[0] = _decode_attend(q_ref[0], n_tiles, kv_tile, None, new_ref[0], s_sc)
    wo_ref[0, :wrows - KV_ROWS, :] = win_ref[0, KV_ROWS:, :]
    wo_ref[0, wrows - KV_ROWS:, :] = new8_ref[0]


def win_step(q3, new_rows, win):
    nb, wrows = win.shape[0], win.shape[1]
    head_blk = pl.BlockSpec((1, N_HEADS, HEAD_DIM), lambda b: (b, 0, 0))
    win_blk = pl.BlockSpec((1, wrows, HEAD_DIM), lambda b: (b, 0, 0))
    return pl.pallas_call(
        _win_step_kernel,
        grid=(nb,),
        in_specs=[head_blk, pl.BlockSpec((1, 1, KV_W), lambda b: (b, 0, 0)),
                  pl.BlockSpec((1, KV_ROWS, HEAD_DIM), lambda b: (b, 0, 0)), win_blk],
        out_specs=[head_blk, win_blk],
        out_shape=[jax.ShapeDtypeStruct((nb, N_HEADS, HEAD_DIM), f32), jax.ShapeDtypeStruct(win.shape, f32)],
        scratch_shapes=[pltpu.VMEM((N_HEADS, wrows // KV_ROWS), f32)],
        compiler_params=_cparams(("parallel",)),
        name="win_step",
    )(q3, new_rows, new_rows.reshape(nb, KV_ROWS, HEAD_DIM), win)


def _rope_tables(pos):
    half = HEAD_DIM // 2
    inv = ROPE_THETA ** (-jnp.arange(half, dtype=f32) / half)
    ang = pos.astype(f32)[:, None] * inv[None, :]
    cos, sin = jnp.cos(ang), jnp.sin(ang)
    return jnp.concatenate([cos, cos], axis=1), jnp.concatenate([-sin, sin], axis=1)


def _pad_cols(a, width):
    return jnp.pad(a, ((0, 0), (0, width - a.shape[1])))


def kernel(x_prompt, x_sample, state_pool, state_mlstm_c, state_mlstm_n, state_mlstm_m, cache_kv_cmp, cache_kv_sel, cache_kv_win, page_table, norm0_g, w_in0, b_gate0, w_pool, pool_scale, mh_norm_g, w_out0, norm1_g, w_in1, b_gate1, w_out1, final_g):
    nbp, t, d = x_prompt.shape
    nbs = x_sample.shape[0]
    mp = nbp * t
    n_pages = page_table.shape[1]
    past_len = n_pages * PAGE_SIZE
    wbuf = cache_kv_win.shape[1]

    w_in0 = w_in0.astype(bf16)
    w_in1 = w_in1.astype(bf16)
    w0 = ([w_in0], [(0, 0, MAIN_W)])
    wg0 = jnp.concatenate([_pad_cols(w_in0[:, MAIN_W:MAIN_W + M_HEADS], LANES),
                           _pad_cols(w_in0[:, MAIN_W + M_HEADS:], LANES)], axis=1)
    bias_i = _pad_cols(b_gate0[None, :M_HEADS], LANES)
    bias_f = _pad_cols(b_gate0[None, M_HEADS:], LANES)
    g_lo = NSA_W + 3 * KV_W
    g_hi = g_lo + 3 * N_HEADS
    w1 = ([w_in1, w_in1[:, g_hi:]], [(0, 0, NSA_W), (1, 0, NSA_W), (0, NSA_W, 3 * KV_W)])
    wg1 = _pad_cols(w_in1[:, g_lo:g_hi], LANES)
    bias1 = _pad_cols(b_gate1[None, :], LANES)
    w_pool_b = w_pool.astype(bf16)
    wo0_pool = w_out0[:POOL_W].astype(bf16)
    wo0_m = w_out0[POOL_W:].astype(bf16)
    wo1 = w_out1.astype(bf16)

    xp = x_prompt.reshape(mp, d)
    xs = x_sample.reshape(nbs, d)

    proj_p, gates_p = norm_proj(xp, norm0_g, w0, wg0, 1024, 1024)
    proj_s, gates_s = norm_proj(xs, norm0_g, w0, wg0, nbs, 1024)

    ypool_p = pool_prompt(proj_p, w_pool_b, pool_scale, nbp, t)
    ym_p, c_p, n_p, m_p = mlstm_prompt(proj_p, gates_p, bias_i, bias_f, mh_norm_g, nbp, t, 256)
    xp1 = out_proj([ypool_p, ym_p], [wo0_pool, wo0_m], xp, 512)

    ypool_s = pool_step(state_pool.reshape(nbs, POOL_STATE * POOL_W), proj_s, w_pool_b, pool_scale)
    m0_pad = _pad_cols(state_mlstm_m, LANES)
    ym_s, c_s, n_s, m_s = mlstm_step(proj_s, gates_s, bias_i, bias_f, mh_norm_g,
                                     state_mlstm_c, state_mlstm_n, m0_pad, 8)
    xs1 = out_proj([ypool_s, ym_s], [wo0_pool, wo0_m], xs, nbs)

    pool_p = proj_p.reshape(nbp, t, MAIN_W)[:, t - POOL_STATE:, :POOL_W]
    pool_s = jnp.concatenate([state_pool[:, 1:], proj_s[:, None, :POOL_W]], axis=1)

    proj1_p, gates1_p = norm_proj(xp1, norm1_g, w1, wg1, 1024, 1024)
    proj1_s, gates1_s = norm_proj(xs1, norm1_g, w1, wg1, nbs, 1024)

    tq = 256
    cos_p, sin_p = _rope_tables(jnp.arange(t))
    qrot_p, kvc_p, kvs_p, kvw_p, blocks_p, kvs_pb, kvw_pb = nsa_prep(proj1_p, cos_p, sin_p, tq, t // tq, True)
    oc_p, sel_p = cmp_prompt(proj1_p, blocks_p, nbp, t, tq)
    expand_t = (jnp.arange(LANES)[None, :] == jnp.arange(t)[:, None] // SEL_BLOCK).astype(bf16)

    cos_s, sin_s = _rope_tables(jnp.full((nbs,), past_len))
    qrot_s, kvs_s, kvw_s = nsa_prep(proj1_s, cos_s, sin_s, nbs, 1, False)
    pt_flat = page_table.reshape(-1)
    q3_s = proj1_s[:, :NSA_W].reshape(nbs, N_HEADS, HEAD_DIM)
    qrot3_s = qrot_s.reshape(nbs, N_HEADS, HEAD_DIM)
    n_pool = cache_kv_cmp.shape[0]
    page_rows = PAGE_SIZE * KV_ROWS

    ow_p, oc_s, sel_s = attn_prompt_with_step(
        qrot_p, kvw_pb, None, None, nbp, t, tq, "win",
        cmp_step(q3_s, cache_kv_cmp.reshape(n_pool, page_rows, HEAD_DIM), pt_flat, n_pages, past_len))
    os_p, os_s = attn_prompt_with_step(
        qrot_p, kvs_pb, sel_p, expand_t, nbp, t, tq, "sel",
        sel_step(qrot3_s, sel_s, kvs_s.reshape(nbs, 1, KV_W),
                 cache_kv_sel.reshape(n_pool, page_rows, HEAD_DIM), pt_flat, n_pages))
    y_p = nsa_out(oc_p, os_p, ow_p, gates1_p, bias1, proj1_p, wo1, xp1, final_g, 256)

    ow_s, win_new = win_step(qrot3_s, kvw_s.reshape(nbs, 1, KV_W),
                             cache_kv_win.reshape(nbs, wbuf * KV_ROWS, HEAD_DIM))
    y_s = nsa_out(oc_s.reshape(nbs, NSA_W), os_s.reshape(nbs, NSA_W), ow_s.reshape(nbs, NSA_W),
                  gates1_s, bias1, proj1_s, wo1, xs1, final_g, nbs)

    kv5 = lambda a, rows: a.reshape(-1, rows, 2, N_KV, HEAD_DIM)
    return (y_p.reshape(nbp, t, d), y_s.reshape(nbs, 1, d),
            pool_p, pool_s,
            c_p, c_s, n_p, n_s, m_p[:, 0, :M_HEADS], m_s[:, :M_HEADS],
            kv5(kvc_p, t), kv5(proj1_s[:, 2 * NSA_W:2 * NSA_W + KV_W], 1),
            kv5(kvs_p, t), kv5(kvs_s, 1),
            kv5(kvw_p, t)[:, t - wbuf:],
            kv5(win_new, wbuf))
```

```python
import functools

import jax
import jax.numpy as jnp
from jax import lax
from jax.experimental import pallas as pl
from jax.experimental.pallas import tpu as pltpu

f32 = jnp.float32
bf16 = jnp.bfloat16

D_MODEL = 2048
POOL_WINDOWS = (2, 4, 8, 16)
POOL_W = 1024
POOL_GROUP_W = 256
POOL_STATE = 15
M_HEADS = 4
M_W = 1024
M_HEAD_DIM = 256
N_HEADS = 16
HEAD_DIM = 128
N_KV = 4
Q_PER_KV = 4
NSA_W = 2048
KV_W = 1024
KV_ROWS = 2 * N_KV
CMP_BLOCK = 32
SEL_BLOCK = 64
SEL_TOPK = 16
WINDOW = 512
PAGE_SIZE = 128
ROPE_THETA = 10000.0
ATT_SCALE = HEAD_DIM ** -0.5
EPS = 1e-6
MAIN_W = 7168
LANES = 128
NEG = -1e30
MASK_BIAS = -(2.0 ** 100)
LOG2E = 1.4426950408889634
VMEM_LIMIT = 48 * 1024 * 1024

_NT = (((1,), (1,)), ((), ()))


def _cparams(sem):
    return pltpu.CompilerParams(dimension_semantics=sem, vmem_limit_bytes=VMEM_LIMIT)


def _sigmoid(x):
    return 1.0 / (1.0 + jnp.exp(-x))


def _silu(x):
    return x * _sigmoid(x)


def _log_sigmoid(x):
    return jnp.minimum(x, 0.0) - jnp.log1p(jnp.exp(-jnp.abs(x)))


def _mm(a, b):
    return jnp.dot(a, b, preferred_element_type=f32)


def _mm_nt(a, b):
    return lax.dot_general(a, b, _NT, preferred_element_type=f32)


def _norm_proj_kernel(*refs, starts, seg_ref):
    n_w = max(seg_ref) + 1
    x_ref, g_ref = refs[:2]
    w_refs = refs[2:2 + n_w]
    wg_ref, o_ref, og_ref, h_ref = refs[2 + n_w:]
    j = pl.program_id(1)

    @pl.when(j == 0)
    def _():
        x = x_ref[...]
        r = lax.rsqrt(jnp.mean(x * x, axis=-1, keepdims=True) + EPS)
        h = ((x * r) * g_ref[...]).astype(bf16)
        h_ref[...] = h
        og_ref[...] = _mm(h, wg_ref[...])

    for k, a in enumerate(seg_ref):
        @pl.when((j >= starts[k]) & (j < starts[k + 1]))
        def _(w_ref=w_refs[a]):
            o_ref[...] = _mm(h_ref[...], w_ref[...])


def norm_proj(x, g, segs, wg, tm, tn):
    arrays, seg_list = segs
    m, d = x.shape
    ng = wg.shape[1]
    starts = [0]
    for _, _, cols in seg_list:
        starts.append(starts[-1] + cols // tn)
    n_tiles = starts[-1]

    def w_spec(a):
        def index(i, j):
            idx = None
            for k, (ak, col0, cols) in enumerate(seg_list):
                if ak != a:
                    continue
                here = col0 // tn + jnp.clip(j - starts[k], 0, cols // tn - 1)
                idx = here if idx is None else jnp.where(j >= starts[k], here, idx)
            return (0, idx)
        return pl.BlockSpec((d, tn), index)

    return pl.pallas_call(
        functools.partial(_norm_proj_kernel, starts=tuple(starts), seg_ref=tuple(a for a, _, _ in seg_list)),
        grid=(m // tm, n_tiles),
        in_specs=[pl.BlockSpec((tm, d), lambda i, j: (i, 0)),
                  pl.BlockSpec((1, d), lambda i, j: (0, 0))]
                 + [w_spec(a) for a in range(len(arrays))]
                 + [pl.BlockSpec((d, ng), lambda i, j: (0, 0))],
        out_specs=[pl.BlockSpec((tm, tn), lambda i, j: (i, j)),
                   pl.BlockSpec((tm, ng), lambda i, j: (i, 0))],
        out_shape=[jax.ShapeDtypeStruct((m, n_tiles * tn), f32), jax.ShapeDtypeStruct((m, ng), f32)],
        scratch_shapes=[pltpu.VMEM((tm, d), bf16)],
        compiler_params=_cparams(("parallel", "arbitrary")),
        name="norm_proj",
    )(x, g.reshape(1, d), *arrays, wg)


def _out_proj_kernel(*refs, n_parts):
    a_refs = refs[:n_parts]
    w_refs = refs[n_parts:2 * n_parts]
    x_ref, o_ref = refs[2 * n_parts:]
    acc = x_ref[...]
    for a_ref, w_ref in zip(a_refs, w_refs):
        acc = acc + _mm(a_ref[...], w_ref[...])
    o_ref[...] = acc


def out_proj(parts, weights, x, tm):
    m, d = x.shape
    n_parts = len(parts)
    in_specs = [pl.BlockSpec((tm, a.shape[1]), lambda i: (i, 0)) for a in parts]
    in_specs += [pl.BlockSpec(w.shape, lambda i: (0, 0)) for w in weights]
    in_specs += [pl.BlockSpec((tm, d), lambda i: (i, 0))]
    args = list(parts) + list(weights) + [x]
    return pl.pallas_call(
        functools.partial(_out_proj_kernel, n_parts=n_parts),
        grid=(m // tm,),
        in_specs=in_specs,
        out_specs=pl.BlockSpec((tm, d), lambda i: (i, 0)),
        out_shape=jax.ShapeDtypeStruct((m, d), f32),
        compiler_params=_cparams(("parallel",)),
        name="out_proj",
    )(*args)


def _pool_kernel(u_ref, z_ref, w_ref, sc_ref, o_ref):
    g = pl.program_id(1)
    x = u_ref[...]
    row = lax.broadcasted_iota(jnp.int32, x.shape, 0)

    def back(a, s):
        return jnp.where(row >= s, pltpu.roll(a, s, axis=0), 0.0)

    s2 = x + back(x, 1)
    s4 = s2 + back(s2, 2)
    s8 = s4 + back(s4, 4)
    s16 = s8 + back(s8, 8)
    win = jnp.where(g == 0, s2, jnp.where(g == 1, s4, jnp.where(g == 2, s8, s16)))
    wlen = lax.shift_left(jnp.int32(2), g)
    cnt = jnp.minimum(row + 1, wlen).astype(f32)
    pooled = win / cnt - x
    y = _mm(pooled.astype(bf16), w_ref[0]) * sc_ref[...]
    o_ref[...] = (y * _silu(z_ref[...])).astype(bf16)


def pool_prompt(proj, w_pool, pool_scale, nb, t):
    ng = len(POOL_WINDOWS)
    return pl.pallas_call(
        _pool_kernel,
        grid=(nb, ng),
        in_specs=[pl.BlockSpec((t, POOL_GROUP_W), lambda b, g: (b, g)),
                  pl.BlockSpec((t, POOL_GROUP_W), lambda b, g: (b, ng + g)),
                  pl.BlockSpec((1, POOL_GROUP_W, POOL_GROUP_W), lambda b, g: (g, 0, 0)),
                  pl.BlockSpec((1, POOL_GROUP_W), lambda b, g: (0, g))],
        out_specs=pl.BlockSpec((t, POOL_GROUP_W), lambda b, g: (b, g)),
        out_shape=jax.ShapeDtypeStruct((nb * t, POOL_W), bf16),
        compiler_params=_cparams(("parallel", "arbitrary")),
        name="pool_prompt",
    )(proj, proj, w_pool, pool_scale.reshape(1, POOL_W))


def _pool_step_kernel(st_ref, u_ref, z_ref, w_ref, sc_ref, o_ref):
    u = u_ref[...]
    for g, wlen in enumerate(POOL_WINDOWS):
        lo = g * POOL_GROUP_W
        ug = u[:, lo:lo + POOL_GROUP_W]
        acc = ug
        for r in range(POOL_STATE + 1 - wlen, POOL_STATE):
            acc = acc + st_ref[:, r * POOL_W + lo:r * POOL_W + lo + POOL_GROUP_W]
        pooled = acc / float(wlen) - ug
        y = _mm(pooled.astype(bf16), w_ref[g]) * sc_ref[:, lo:lo + POOL_GROUP_W]
        o_ref[:, lo:lo + POOL_GROUP_W] = (y * _silu(z_ref[:, lo:lo + POOL_GROUP_W])).astype(bf16)


def pool_step(state_flat, proj, w_pool, pool_scale):
    nb = proj.shape[0]
    return pl.pallas_call(
        _pool_step_kernel,
        grid=(1,),
        in_specs=[pl.BlockSpec(state_flat.shape, lambda i: (0, 0)),
                  pl.BlockSpec((nb, POOL_W), lambda i: (0, 0)),
                  pl.BlockSpec((nb, POOL_W), lambda i: (0, 1)),
                  pl.BlockSpec(w_pool.shape, lambda i: (0, 0, 0)),
                  pl.BlockSpec((1, POOL_W), lambda i: (0, 0))],
        out_specs=pl.BlockSpec((nb, POOL_W), lambda i: (0, 0)),
        out_shape=jax.ShapeDtypeStruct((nb, POOL_W), bf16),
        compiler_params=_cparams(("arbitrary",)),
        name="pool_step",
    )(state_flat, proj, proj, w_pool, pool_scale.reshape(1, POOL_W))


def _head_out(hc, o, z, g):
    hc = hc * _sigmoid(o)
    hc = hc * lax.rsqrt(jnp.mean(hc * hc, axis=-1, keepdims=True) + EPS)
    return ((hc * g) * _silu(z)).astype(bf16)


def _mlstm_kernel(q_ref, k_ref, v_ref, o_ref, z_ref, gi_ref, gf_ref, bi_ref, bf_ref, mhg_ref,
                  y_ref, c_ref, n_ref, m_ref):
    @pl.when(pl.program_id(1) == 0)
    def _():
        c_ref[...] = jnp.zeros_like(c_ref)
        n_ref[...] = jnp.zeros_like(n_ref)
        m_ref[...] = jnp.zeros_like(m_ref)

    ln = q_ref.shape[0]
    gi = gi_ref[...] + bi_ref[...]
    lf = _log_sigmoid(gf_ref[...] + bf_ref[...])
    row = lax.broadcasted_iota(jnp.int32, lf.shape, 0)
    b = lf
    s = 1
    while s < ln:
        b = b + jnp.where(row >= s, pltpu.roll(b, s, axis=0), 0.0)
        s *= 2
    r_t = (gi - b).T
    tt = lax.broadcasted_iota(jnp.int32, (ln, ln), 0)
    ss = lax.broadcasted_iota(jnp.int32, (ln, ln), 1)
    causal = ss <= tt
    lane = lax.broadcasted_iota(jnp.int32, (1, LANES), 1)
    m_vec = m_ref[0]
    for h in range(M_HEADS):
        hs = slice(h * M_HEAD_DIM, (h + 1) * M_HEAD_DIM)
        b_col = b[:, h:h + 1]
        ig_col = gi[:, h:h + 1]
        m_prev = m_vec[:, h:h + 1]
        inter = b_col + m_prev
        dmat = jnp.where(causal, b_col + r_t[h:h + 1, :], -jnp.inf)
        m_t = jnp.maximum(inter, jnp.max(dmat, axis=1, keepdims=True))
        dw = jnp.exp(dmat - m_t)
        iw = jnp.exp(inter - m_t)
        q = q_ref[:, hs]
        k = k_ref[:, hs] * (M_HEAD_DIM ** -0.5)
        v = v_ref[:, hs]
        qb, kb, vb = q.astype(bf16), k.astype(bf16), v.astype(bf16)
        c = c_ref[0, h]
        n = n_ref[0, h:h + 1, :]
        qk = _mm_nt(qb, kb) * dw
        num = iw * _mm(qb, c.astype(bf16)) + _mm(qk.astype(bf16), vb)
        den = iw * jnp.sum(q * n, axis=1, keepdims=True) + jnp.sum(qk, axis=1, keepdims=True)
        hc = num / jnp.maximum(jnp.abs(den), jnp.exp(-m_t))
        y_ref[:, hs] = _head_out(hc, o_ref[:, hs], z_ref[:, hs], mhg_ref[:, hs])
        m_last = m_t[ln - 1:ln, :]
        b_last = b_col[ln - 1:ln, :]
        ws = jnp.exp(b_last - b_col + ig_col - m_last)
        dec = jnp.exp(b_last + m_prev - m_last)
        kw = ws * k
        c_ref[0, h] = dec * c + _mm(kw.T.astype(bf16), vb)
        n_ref[0, h:h + 1, :] = dec * n + jnp.sum(kw, axis=0, keepdims=True)
        m_vec = jnp.where(lane == h, m_last, m_vec)
    m_ref[0] = m_vec


def mlstm_prompt(proj, gates, bias_i, bias_f, mh_norm_g, nb, t, ln):
    nc = t // ln
    col = lambda cb: pl.BlockSpec((ln, M_W), lambda b, c: (b * nc + c, cb))
    gcol = lambda cb: pl.BlockSpec((ln, LANES), lambda b, c: (b * nc + c, cb))
    vec = lambda w: pl.BlockSpec((1, w), lambda b, c: (0, 0))
    return pl.pallas_call(
        _mlstm_kernel,
        grid=(nb, nc),
        in_specs=[col(2), col(3), col(4), col(5), col(6), gcol(0), gcol(1), vec(LANES), vec(LANES), vec(M_W)],
        out_specs=[pl.BlockSpec((ln, M_W), lambda b, c: (b * nc + c, 0)),
                   pl.BlockSpec((1, M_HEADS, M_HEAD_DIM, M_HEAD_DIM), lambda b, c: (b, 0, 0, 0)),
                   pl.BlockSpec((1, M_HEADS, M_HEAD_DIM), lambda b, c: (b, 0, 0)),
                   pl.BlockSpec((1, 1, LANES), lambda b, c: (b, 0, 0))],
        out_shape=[jax.ShapeDtypeStruct((nb * t, M_W), bf16),
                   jax.ShapeDtypeStruct((nb, M_HEADS, M_HEAD_DIM, M_HEAD_DIM), f32),
                   jax.ShapeDtypeStruct((nb, M_HEADS, M_HEAD_DIM), f32),
                   jax.ShapeDtypeStruct((nb, 1, LANES), f32)],
        compiler_params=_cparams(("parallel", "arbitrary")),
        name="mlstm_prompt",
    )(proj, proj, proj, proj, proj, gates, gates, bias_i, bias_f, mh_norm_g.reshape(1, M_W))


def _mlstm_step_kernel(q_ref, k_ref, v_ref, o_ref, z_ref, gi_ref, gf_ref, bi_ref, bf_ref, mhg_ref,
                       c_ref, n_ref, m_ref, y_ref, co_ref, no_ref, mo_ref):
    nb = q_ref.shape[0]
    gi = gi_ref[...] + bi_ref[...]
    lf = _log_sigmoid(gf_ref[...] + bf_ref[...])
    inter = lf + m_ref[...]
    m_t = jnp.maximum(inter, gi)
    dw_all = jnp.exp(gi - m_t)
    iw_all = jnp.exp(inter - m_t)
    em_all = jnp.exp(-m_t)
    mo_ref[...] = m_t
    d0 = lax.broadcasted_iota(jnp.int32, (M_HEAD_DIM, M_HEAD_DIM), 0)
    d1 = lax.broadcasted_iota(jnp.int32, (M_HEAD_DIM, M_HEAD_DIM), 1)
    eye = d0 == d1
    for j in range(nb):
        for h in range(M_HEADS):
            hs = slice(h * M_HEAD_DIM, (h + 1) * M_HEAD_DIM)
            dw = dw_all[j:j + 1, h:h + 1]
            iw = iw_all[j:j + 1, h:h + 1]
            em = em_all[j:j + 1, h:h + 1]
            q = q_ref[j:j + 1, hs]
            k = k_ref[j:j + 1, hs] * (M_HEAD_DIM ** -0.5)
            v = v_ref[j:j + 1, hs]
            c = c_ref[j, h]
            n = n_ref[j, h:h + 1, :]
            qc = _mm(jnp.broadcast_to(q, (8, M_HEAD_DIM)).astype(bf16), c.astype(bf16))[0:1, :]
            qk = jnp.sum(q * k, axis=1, keepdims=True) * dw
            num = iw * qc + qk * v
            den = iw * jnp.sum(q * n, axis=1, keepdims=True) + qk
            hc = num / jnp.maximum(jnp.abs(den), em)
            y_ref[j:j + 1, hs] = _head_out(hc, o_ref[j:j + 1, hs], z_ref[j:j + 1, hs], mhg_ref[:, hs])
            kdiag = jnp.where(eye, jnp.broadcast_to(k, (M_HEAD_DIM, M_HEAD_DIM)), 0.0).astype(bf16)
            vrep = jnp.broadcast_to(v, (M_HEAD_DIM, M_HEAD_DIM)).astype(bf16)
            co_ref[j, h] = iw * c + dw * _mm(kdiag, vrep)
            no_ref[j, h:h + 1, :] = iw * n + dw * k


def mlstm_step(proj, gates, bias_i, bias_f, mh_norm_g, c0, n0, m0_pad, bb):
    nb = proj.shape[0]
    col = lambda cb: pl.BlockSpec((bb, M_W), lambda i: (i, cb))
    gcol = lambda cb: pl.BlockSpec((bb, LANES), lambda i: (i, cb))
    vec = lambda w: pl.BlockSpec((1, w), lambda i: (0, 0))
    cspec = pl.BlockSpec((bb, M_HEADS, M_HEAD_DIM, M_HEAD_DIM), lambda i: (i, 0, 0, 0))
    nspec = pl.BlockSpec((bb, M_HEADS, M_HEAD_DIM), lambda i: (i, 0, 0))
    return pl.pallas_call(
        _mlstm_step_kernel,
        grid=(nb // bb,),
        in_specs=[col(2), col(3), col(4), col(5), col(6), gcol(0), gcol(1), vec(LANES), vec(LANES), vec(M_W),
                  cspec, nspec, gcol(0)],
        out_specs=[pl.BlockSpec((bb, M_W), lambda i: (i, 0)), cspec, nspec, gcol(0)],
        out_shape=[jax.ShapeDtypeStruct((nb, M_W), bf16),
                   jax.ShapeDtypeStruct(c0.shape, f32),
                   jax.ShapeDtypeStruct(n0.shape, f32),
                   jax.ShapeDtypeStruct((nb, LANES), f32)],
        compiler_params=_cparams(("parallel",)),
        name="mlstm_step",
    )(proj, proj, proj, proj, proj, gates, gates, bias_i, bias_f, mh_norm_g.reshape(1, M_W), c0, n0, m0_pad)


def _rope(x, cos, sin_signed):
    return x * cos + pltpu.roll(x, HEAD_DIM // 2, axis=1) * sin_signed


def _nsa_prep_kernel(*refs, prompt):
    q_ref, kvc_ref, kvs_ref, kvw_ref, cos_ref, sin_ref, qr_ref = refs[:7]
    tq = q_ref.shape[0]
    cos, sin = cos_ref[...], sin_ref[...]
    for h in range(N_HEADS):
        hs = slice(h * HEAD_DIM, (h + 1) * HEAD_DIM)
        qr_ref[:, hs] = _rope(q_ref[:, hs], cos, sin).astype(bf16)

    def chunks(src, rotate):
        for c in range(KV_ROWS):
            x = src[:, c * HEAD_DIM:(c + 1) * HEAD_DIM]
            yield c, (_rope(x, cos, sin) if rotate and c < N_KV else x)

    if prompt:
        kvc_i, kvs_i, kvw_i, blk_ref, kvs_b, kvw_b = refs[7:13]
        for c, x in chunks(kvc_ref, False):
            kvc_i[pl.ds(c, tq, stride=KV_ROWS), :] = x
        for src, dst_i, dst_b in ((kvs_ref, kvs_i, kvs_b), (kvw_ref, kvw_i, kvw_b)):
            for c, x in chunks(src, True):
                dst_i[pl.ds(c, tq, stride=KV_ROWS), :] = x
                dst_b[:, c * HEAD_DIM:(c + 1) * HEAD_DIM] = x.astype(bf16)
        nblk = tq // CMP_BLOCK
        means = jnp.sum(kvc_ref[...].reshape(nblk, CMP_BLOCK, KV_W), axis=1) * (1.0 / CMP_BLOCK)
        for c in range(KV_ROWS):
            blk_ref[pl.ds(c, nblk, stride=KV_ROWS), :] = means[:, c * HEAD_DIM:(c + 1) * HEAD_DIM]
    else:
        for src, dst in ((kvs_ref, refs[7]), (kvw_ref, refs[8])):
            for c, x in chunks(src, True):
                dst[:, c * HEAD_DIM:(c + 1) * HEAD_DIM] = x


def nsa_prep(proj, cos, sin, tq, n_pos_blocks, prompt):
    m = proj.shape[0]
    row = lambda w, cb: pl.BlockSpec((tq, w), lambda i: (i, cb))
    tab = pl.BlockSpec((tq, HEAD_DIM), lambda i: (i % n_pos_blocks, 0))
    if prompt:
        inter = pl.BlockSpec((tq * KV_ROWS, HEAD_DIM), lambda i: (i, 0))
        inter_shape = jax.ShapeDtypeStruct((m * KV_ROWS, HEAD_DIM), f32)
        out_specs = [row(NSA_W, 0), inter, inter, inter,
                     pl.BlockSpec((tq // CMP_BLOCK * KV_ROWS, HEAD_DIM), lambda i: (i, 0)), row(KV_W, 0), row(KV_W, 0)]
        out_shape = [jax.ShapeDtypeStruct((m, NSA_W), bf16), inter_shape, inter_shape, inter_shape,
                     jax.ShapeDtypeStruct((m // CMP_BLOCK * KV_ROWS, HEAD_DIM), f32),
                     jax.ShapeDtypeStruct((m, KV_W), bf16), jax.ShapeDtypeStruct((m, KV_W), bf16)]
    else:
        out_specs = [row(NSA_W, 0), row(KV_W, 0), row(KV_W, 0)]
        out_shape = [jax.ShapeDtypeStruct((m, NSA_W), bf16), jax.ShapeDtypeStruct((m, KV_W), f32),
                     jax.ShapeDtypeStruct((m, KV_W), f32)]
    return pl.pallas_call(
        functools.partial(_nsa_prep_kernel, prompt=prompt),
        grid=(m // tq,),
        in_specs=[row(NSA_W, 0), row(KV_W, 4), row(KV_W, 5), row(KV_W, 6), tab, tab],
        out_specs=out_specs,
        out_shape=out_shape,
        compiler_params=_cparams(("parallel",)),
        name="nsa_prep",
    )(proj, proj, proj, proj, cos, sin)


def _select_blocks(imp, q_pos, n_cand):
    lane = lax.broadcasted_iota(jnp.int32, imp.shape, 1)
    pair = imp + pltpu.roll(imp, LANES - 1, axis=1)
    cur2 = lax.shift_left(lax.shift_right_logical(q_pos, 6), 1)
    valid = ((lane & 1) == 0) & (lane <= cur2)
    v = jnp.where(lane == cur2, jnp.inf, pair)
    v = jnp.where(valid, v, -jnp.inf)
    cnt = jnp.zeros(imp.shape, f32)
    for i in range(n_cand):
        vi = v[:, 2 * i:2 * i + 1]
        before = jnp.where(lane > 2 * i, 1.0, 0.0)
        cnt = cnt + jnp.where(vi > v, 1.0, 0.0) + jnp.where(vi == v, before, 0.0)
    return jnp.where(valid & (cnt < SEL_TOPK), 1.0, 0.0)


def _pad_rows(x, rows):
    return jnp.concatenate([x, jnp.zeros((rows - x.shape[0], x.shape[1]), x.dtype)], axis=0)


def _masked_softmax(s, mask):
    s = jnp.where(mask, s, -jnp.inf)
    m = jnp.max(s, axis=-1, keepdims=True)
    m = jnp.where(m > -jnp.inf, m, 0.0)
    p = jnp.exp(s - m)
    return p / jnp.maximum(jnp.sum(p, axis=-1, keepdims=True), 1e-30)


def _select_block_rows(pair, q_pos):
    row = lax.broadcasted_iota(jnp.int32, pair.shape, 0)
    cur = lax.shift_right_logical(q_pos, 6)
    valid = row <= cur
    v = jnp.where(row == cur, jnp.inf, pair)
    v = jnp.where(valid, v, -jnp.inf)
    cnt = jnp.zeros(pair.shape, f32)
    for i in range(pair.shape[0]):
        vi = v[i:i + 1, :]
        before = jnp.where(row > i, 1.0, 0.0)
        cnt = cnt + jnp.where(vi > v, 1.0, 0.0) + jnp.where(vi == v, before, 0.0)
    return jnp.where(valid & (cnt < SEL_TOPK), 1.0, 0.0)


def _cmp_prompt_kernel(q_ref, blk_ref, oc_ref, sel_ref, pair_sc, flag_sc, *, n_blocks):
    tq = q_ref.shape[0]
    n_sel = n_blocks // 2
    t0 = pl.program_id(1) * tq
    row = lax.broadcasted_iota(jnp.int32, (LANES, tq), 0)
    q_pos = lax.broadcasted_iota(jnp.int32, (LANES, tq), 1) + t0
    vis = (row < n_blocks) & ((row + 1) * CMP_BLOCK - 1 <= q_pos)
    flag_sc[...] = jnp.zeros_like(flag_sc)

    def block_rows(c):
        return _pad_rows(blk_ref[pl.ds(c, n_blocks, stride=KV_ROWS), :], LANES).astype(bf16)

    for g in range(N_KV):
        kg = block_rows(g)
        vg = block_rows(N_KV + g)
        imp = jnp.zeros((LANES, tq), f32)
        for r in range(Q_PER_KV):
            hs = slice((g * Q_PER_KV + r) * HEAD_DIM, (g * Q_PER_KV + r + 1) * HEAD_DIM)
            s = jnp.where(vis, _mm_nt(kg, q_ref[:, hs].astype(bf16)) * ATT_SCALE, -jnp.inf)
            m = jnp.max(s, axis=0, keepdims=True)
            m = jnp.where(m > -jnp.inf, m, 0.0)
            p = jnp.exp(s - m)
            p = p / jnp.maximum(jnp.sum(p, axis=0, keepdims=True), 1e-30)
            oc_ref[:, hs] = _mm(p.T.astype(bf16), vg).astype(bf16)
            imp = imp + p
        pair = imp + pltpu.roll(imp, LANES - 1, axis=0)
        for h in range(tq // LANES):
            ts = slice(h * LANES, (h + 1) * LANES)
            pos = lax.broadcasted_iota(jnp.int32, (n_sel, LANES), 1) + (t0 + h * LANES)
            pair_sc[h] = pair[:, ts]
            flag_sc[:n_sel, :] = _select_block_rows(pair_sc[h, pl.ds(0, n_sel, stride=2), :], pos)
            sel_ref[ts, g * LANES:(g + 1) * LANES] = flag_sc[...].T.astype(bf16)


def cmp_prompt(proj, blocks, nb, t, tq):
    nq = t // tq
    n_blocks = t // CMP_BLOCK
    return pl.pallas_call(
        functools.partial(_cmp_prompt_kernel, n_blocks=n_blocks),
        grid=(nb, nq),
        in_specs=[pl.BlockSpec((tq, NSA_W), lambda b, i: (b * nq + i, 0)),
                  pl.BlockSpec((n_blocks * KV_ROWS, HEAD_DIM), lambda b, i: (b, 0))],
        out_specs=[pl.BlockSpec((tq, NSA_W), lambda b, i: (b * nq + i, 0)),
                   pl.BlockSpec((tq, N_KV * LANES), lambda b, i: (b * nq + i, 0))],
        out_shape=[jax.ShapeDtypeStruct((nb * t, NSA_W), bf16),
                   jax.ShapeDtypeStruct((nb * t, N_KV * LANES), bf16)],
        scratch_shapes=[pltpu.VMEM((tq // LANES, LANES, LANES), f32), pltpu.VMEM((LANES, LANES), f32)],
        compiler_params=_cparams(("parallel", "parallel")),
        name="cmp_prompt",
    )(proj, blocks)


def _attn_kernel(*refs, mode, tile, reach):
    if mode == "sel":
        q_ref, k_ref, v_ref, sel_ref, et_ref, o_ref, qp_sc, s_sc, mx_sc, mb_sc, acc_sc, rel_sc = refs
    else:
        q_ref, k_ref, v_ref, o_ref, qp_sc, s_sc, mx_sc, mb_sc, acc_sc, rel_sc = refs
    qi = pl.program_id(2)
    rows = Q_PER_KV * tile

    @pl.when(qi == 0)
    def _():
        rel_sc[...] = ((lax.broadcasted_iota(jnp.int32, rel_sc.shape, 0) & (tile - 1))
                       - lax.broadcasted_iota(jnp.int32, rel_sc.shape, 1))

    for r in range(Q_PER_KV):
        qh = q_ref[:, r * HEAD_DIM:(r + 1) * HEAD_DIM]
        if mode == "sel":
            bias = ((1.0 - sel_ref[...].astype(f32)) * MASK_BIAS).astype(bf16)
            qh = jnp.concatenate([qh, bias], axis=1)
        qp_sc[r * tile:(r + 1) * tile, :] = qh
    lo = 0 if mode == "sel" else jnp.maximum(qi - reach, 0)

    def scores(kj, n, masked):
        start = pl.multiple_of(kj * tile, tile)
        kt = k_ref[pl.ds(start, n * tile), :]
        if mode == "sel":
            kt = jnp.concatenate([kt, et_ref[pl.ds(start, n * tile), :]], axis=1)
        s = _mm_nt(qp_sc[...], kt)
        if not masked:
            return s
        dpos = rel_sc[:, :n * tile] + (qi - kj) * tile
        if mode == "win":
            ok = pltpu.bitcast(dpos, jnp.uint32) <= WINDOW
        else:
            ok = dpos >= 0
        return jnp.where(ok, s, MASK_BIAS)

    def store_scores(kj, n, s):
        mx = mx_sc[...]
        for i in range(n):
            s_sc[kj + i] = s[:, i * tile:(i + 1) * tile]
        for c in range(n * tile // LANES):
            mx = jnp.maximum(mx, s[:, c * LANES:(c + 1) * LANES])
        mx_sc[...] = mx

    def weigh(kj, n):
        start = pl.multiple_of(kj * tile, tile)
        mb = mb_sc[...]
        p = jnp.concatenate([jnp.exp2((s_sc[kj + i][:, c * LANES:(c + 1) * LANES] - mb) * (ATT_SCALE * LOG2E))
                             for i in range(n) for c in range(tile // LANES)], axis=1)
        vt = jnp.concatenate([v_ref[pl.ds(start, n * tile), :], jnp.ones((n * tile, HEAD_DIM), bf16)], axis=1)
        acc_sc[...] += _mm(p.astype(bf16), vt)

    def in_pairs(first, count, fn):
        def trip(i, carry):
            fn(first + 2 * i, 2)
            return carry

        lax.fori_loop(0, count // 2, trip, 0)

        @pl.when(count % 2 == 1)
        def _():
            fn(first + count - 1, 1)

    mx_sc[...] = jnp.full_like(mx_sc, MASK_BIAS)
    if mode == "sel":
        in_pairs(lo, qi - lo, lambda kj, n: store_scores(kj, n, scores(kj, n, False)))
        store_scores(qi, 1, scores(qi, 1, True))
    else:
        in_pairs(lo, qi - lo + 1, lambda kj, n: store_scores(kj, n, scores(kj, n, True)))
    m = jnp.max(mx_sc[...], axis=1, keepdims=True)
    mb_sc[...] = jnp.broadcast_to(m, mb_sc.shape)
    acc_sc[...] = jnp.zeros_like(acc_sc)
    in_pairs(lo, qi - lo + 1, weigh)
    acc = acc_sc[...]
    o = (acc[:, :HEAD_DIM] / acc[:, HEAD_DIM:]).astype(bf16)
    for r in range(Q_PER_KV):
        o_ref[:, r * HEAD_DIM:(r + 1) * HEAD_DIM] = o[r * tile:(r + 1) * tile, :]


def _attn_step_kernel(pt_ref, *refs, counts, bodies):
    n_in = sum(c[0] for c in counts)
    n_out = sum(c[1] for c in counts)
    i, o, s = 0, n_in, n_in + n_out
    for k, (body, (ci, co, cs)) in enumerate(zip(bodies, counts)):
        part = (*refs[i:i + ci], *refs[o:o + co], *refs[s:s + cs])
        body(*part) if k == 0 else body(pt_ref, *part)
        i, o, s = i + ci, o + co, s + cs


def attn_prompt_with_step(q_rot, kv_b, sel, expand_t, nb, t, tile, mode, steps):
    nq = t // tile
    reach = -(-WINDOW // tile)
    qw = Q_PER_KV * HEAD_DIM
    kdim = 2 * HEAD_DIM if mode == "sel" else HEAD_DIM
    rows = Q_PER_KV * tile
    in_specs = [pl.BlockSpec((tile, qw), lambda b, g, i, pt: (b * nq + i, g)),
                pl.BlockSpec((t, HEAD_DIM), lambda b, g, i, pt: (b, g)),
                pl.BlockSpec((t, HEAD_DIM), lambda b, g, i, pt: (b, N_KV + g))]
    args = [q_rot, kv_b, kv_b]
    if mode == "sel":
        in_specs += [pl.BlockSpec((tile, LANES), lambda b, g, i, pt: (b * nq + i, g)),
                     pl.BlockSpec((t, LANES), lambda b, g, i, pt: (0, 0))]
        args += [sel, expand_t]
    scratch = [pltpu.VMEM((rows, kdim), bf16), pltpu.VMEM((nq, rows, tile), f32), pltpu.VMEM((rows, LANES), f32),
               pltpu.VMEM((rows, LANES), f32), pltpu.VMEM((rows, 2 * HEAD_DIM), f32),
               pltpu.VMEM((rows, 2 * tile), jnp.int32)]
    out_specs = [pl.BlockSpec((tile, qw), lambda b, g, i, pt: (b * nq + i, g))]
    out_shape = [jax.ShapeDtypeStruct((nb * t, NSA_W), bf16)]
    bodies = [functools.partial(_attn_kernel, mode=mode, tile=tile, reach=reach)]
    counts = [(len(in_specs), 1, len(scratch))]
    pt_flat = None
    for step in steps:
        body, pt_flat, s_args, s_in, s_out, s_shape, s_scratch = step(lambda b, g, i: (b * N_KV + g) * nq + i)
        assert s_shape[0].shape[0] == nb * N_KV * nq, "one sample per attention grid step"
        bodies.append(body)
        counts.append((len(s_in), len(s_out), len(s_scratch)))
        args, in_specs, out_specs = args + s_args, in_specs + s_in, out_specs + s_out
        out_shape, scratch = out_shape + s_shape, scratch + s_scratch
    return pl.pallas_call(
        functools.partial(_attn_step_kernel, counts=tuple(counts), bodies=tuple(bodies)),
        grid_spec=pltpu.PrefetchScalarGridSpec(
            num_scalar_prefetch=1,
            grid=(nb, N_KV, nq),
            in_specs=in_specs,
            out_specs=out_specs,
            scratch_shapes=scratch),
        out_shape=out_shape,
        compiler_params=_cparams(("parallel", "parallel", "arbitrary")),
        name="attn_" + mode + "_step",
    )(pt_flat, *args)


def _nsa_out_kernel(oc_ref, os_ref, ow_ref, g_ref, b_ref, z_ref, w_ref, x_ref, fg_ref, o_ref, a_sc):
    gate = _sigmoid(g_ref[...] + b_ref[...])
    for h in range(N_HEADS):
        hs = slice(h * HEAD_DIM, (h + 1) * HEAD_DIM)
        o = (gate[:, 3 * h:3 * h + 1] * oc_ref[:, hs].astype(f32)
             + gate[:, 3 * h + 1:3 * h + 2] * os_ref[:, hs].astype(f32)
             + gate[:, 3 * h + 2:3 * h + 3] * ow_ref[:, hs].astype(f32))
        a_sc[:, hs] = (o * _silu(z_ref[:, hs])).astype(bf16)
    acc = x_ref[...] + _mm(a_sc[...], w_ref[...])
    r = lax.rsqrt(jnp.mean(acc * acc, axis=-1, keepdims=True) + EPS)
    o_ref[...] = (acc * r) * fg_ref[...]


def nsa_out(o_c, o_s, o_w, gates, bias, proj, w_out, x, final_g, tm):
    m, d = x.shape
    row = lambda w, cb: pl.BlockSpec((tm, w), lambda i: (i, cb))
    return pl.pallas_call(
        _nsa_out_kernel,
        grid=(m // tm,),
        in_specs=[row(NSA_W, 0), row(NSA_W, 0), row(NSA_W, 0), row(LANES, 0),
                  pl.BlockSpec((1, LANES), lambda i: (0, 0)),
                  row(NSA_W, 1),
                  pl.BlockSpec(w_out.shape, lambda i: (0, 0)),
                  row(d, 0),
                  pl.BlockSpec((1, d), lambda i: (0, 0))],
        out_specs=row(d, 0),
        out_shape=jax.ShapeDtypeStruct((m, d), f32),
        scratch_shapes=[pltpu.VMEM((tm, NSA_W), bf16)],
        compiler_params=_cparams(("parallel",)),
        name="nsa_out",
    )(o_c, o_s, o_w, gates, bias, proj, w_out, x, final_g.reshape(1, d))


def _head_group(shape):
    return lax.shift_right_logical(lax.broadcasted_iota(jnp.int32, shape, 0), 2)


def _cmp_step_kernel(pt_ref, q_ref, *refs, n_pages, q_pos):
    pages = refs[:n_pages]
    oc_ref, sel_ref, blk_sc = refs[n_pages:]
    per_page = PAGE_SIZE // CMP_BLOCK
    for p in range(n_pages):
        x = pages[p][0].reshape(per_page, CMP_BLOCK, KV_ROWS, HEAD_DIM)
        means = jnp.sum(x, axis=1) * (1.0 / CMP_BLOCK)
        blk_sc[p * per_page * KV_ROWS:(p + 1) * per_page * KV_ROWS, :] = means.reshape(per_page * KV_ROWS, HEAD_DIM)
    n_blocks = blk_sc.shape[0] // KV_ROWS
    qb = q_ref[0].astype(bf16)
    grp = _head_group((N_HEADS, LANES))
    lane = lax.broadcasted_iota(jnp.int32, (N_HEADS, LANES), 1)
    s = jnp.zeros((N_HEADS, LANES), f32)
    for g in range(N_KV):
        kg = _pad_rows(blk_sc[pl.ds(g, n_blocks, stride=KV_ROWS), :], LANES).astype(bf16)
        s = jnp.where(grp == g, _mm_nt(qb, kg), s)
    vis = (lane < n_blocks) & ((lane + 1) * CMP_BLOCK - 1 <= q_pos)
    p = _masked_softmax(s * ATT_SCALE, vis)
    pb = p.astype(bf16)
    o = jnp.zeros((N_HEADS, HEAD_DIM), f32)
    for g in range(N_KV):
        vg = _pad_rows(blk_sc[pl.ds(N_KV + g, n_blocks, stride=KV_ROWS), :], LANES).astype(bf16)
        o = jnp.where(grp == g, _mm(pb, vg), o)
    oc_ref[0] = o
    row8 = lax.broadcasted_iota(jnp.int32, (8, LANES), 0)
    imp = jnp.zeros((8, LANES), f32)
    for g in range(N_KV):
        imp_g = jnp.sum(p[g * Q_PER_KV:(g + 1) * Q_PER_KV, :], axis=0, keepdims=True)
        imp = jnp.where(row8 == g, imp_g, imp)
    sel_ref[0] = _select_blocks(imp, jnp.full((8, LANES), q_pos, jnp.int32), q_pos // SEL_BLOCK + 1)


def _sample_specs(sample, n_pages):
    per = lambda *shape: pl.BlockSpec((1,) + shape, lambda *a: (sample(*a[:-1]), 0, 0))
    pages = [pl.BlockSpec((1, PAGE_SIZE * KV_ROWS, HEAD_DIM),
                          lambda *a, p=p: (a[-1][sample(*a[:-1]) * n_pages + p], 0, 0)) for p in range(n_pages)]
    return per, pages


def cmp_step(q3, cache, pt_flat, n_pages, q_pos):
    def build(sample):
        nb = q3.shape[0]
        per, pages = _sample_specs(sample, n_pages)
        return (functools.partial(_cmp_step_kernel, n_pages=n_pages, q_pos=q_pos), pt_flat,
                [q3] + [cache] * n_pages,
                [per(N_HEADS, HEAD_DIM)] + pages,
                [per(N_HEADS, HEAD_DIM), per(8, LANES)],
                [jax.ShapeDtypeStruct((nb, N_HEADS, HEAD_DIM), f32), jax.ShapeDtypeStruct((nb, 8, LANES), f32)],
                [pltpu.VMEM((n_pages * PAGE_SIZE // CMP_BLOCK * KV_ROWS, HEAD_DIM), f32)])
    return build


def _decode_attend(qb, n_tiles, kv_tile, flags, new_row, s_sc):
    grp = _head_group((N_HEADS, LANES))
    lane = lax.broadcasted_iota(jnp.int32, (N_HEADS, LANES), 1)
    for p in range(n_tiles):
        sp = jnp.zeros((N_HEADS, LANES), f32)
        for g in range(N_KV):
            sg = _mm_nt(qb, kv_tile(p, g).astype(bf16)) * ATT_SCALE
            if flags is not None:
                f0 = flags[g:g + 1, 4 * p:4 * p + 1]
                f1 = flags[g:g + 1, 4 * p + 2:4 * p + 3]
                sg = jnp.where(jnp.where(lane < SEL_BLOCK, f0, f1) > 0.5, sg, NEG)
            sp = jnp.where(grp == g, sg, sp)
        s_sc[:, p * LANES:(p + 1) * LANES] = sp
    qf = qb.astype(f32)
    s_new = jnp.zeros((N_HEADS, 1), f32)
    grp1 = _head_group((N_HEADS, 1))
    for g in range(N_KV):
        kn = new_row[:, g * HEAD_DIM:(g + 1) * HEAD_DIM].astype(bf16).astype(f32)
        s_new = jnp.where(grp1 == g, jnp.sum(qf * kn, axis=1, keepdims=True) * ATT_SCALE, s_new)
    s_all = s_sc[...]
    m = jnp.maximum(jnp.max(s_all, axis=1, keepdims=True), s_new)
    p_all = jnp.exp(s_all - m)
    p_new = jnp.exp(s_new - m)
    den = jnp.sum(p_all, axis=1, keepdims=True) + p_new
    o = jnp.zeros((N_HEADS, HEAD_DIM), f32)
    for g in range(N_KV):
        vn = new_row[:, KV_W // 2 + g * HEAD_DIM:KV_W // 2 + (g + 1) * HEAD_DIM].astype(bf16).astype(f32)
        o = jnp.where(grp == g, p_new * vn, o)
    for p in range(n_tiles):
        pb = p_all[:, p * LANES:(p + 1) * LANES].astype(bf16)
        for g in range(N_KV):
            o = o + jnp.where(grp == g, _mm(pb, kv_tile(p, N_KV + g).astype(bf16)), 0.0)
    return o / den


def _sel_step_kernel(pt_ref, q_ref, sel_ref, new_ref, *refs, n_pages):
    pages = refs[:n_pages]
    o_ref, s_sc = refs[n_pages:]
    kv_tile = lambda p, c: pages[p][0, pl.ds(c, PAGE_SIZE, stride=KV_ROWS), :]
    o_ref[0] = _decode_attend(q_ref[0], n_pages, kv_tile, sel_ref[0], new_ref[0], s_sc)


def sel_step(q3, sel, new_rows, cache, pt_flat, n_pages):
    def build(sample):
        nb = q3.shape[0]
        per, pages = _sample_specs(sample, n_pages)
        return (functools.partial(_sel_step_kernel, n_pages=n_pages), pt_flat,
                [q3, sel, new_rows] + [cache] * n_pages,
                [per(N_HEADS, HEAD_DIM), per(8, LANES), per(1, KV_W)] + pages,
                [per(N_HEADS, HEAD_DIM)],
                [jax.ShapeDtypeStruct((nb, N_HEADS, HEAD_DIM), f32)],
                [pltpu.VMEM((N_HEADS, n_pages * PAGE_SIZE), f32)])
    return build


def _win_step_kernel(pt_ref, q_ref, new_ref, new8_ref, win_ref, o_ref, wo_ref, s_sc):
    wrows = win_ref.shape[1]
    n_tiles = wrows // (LANES * KV_ROWS)
    kv_tile = lambda p, c: win_ref[0, pl.ds(p * LANES * KV_ROWS + c, LANES, stride=KV_ROWS), :]
    o_ref[0] = _decode_attend(q_ref[0], n_tiles, kv_tile, None, new_ref[0], s_sc)
    wo_ref[0, :wrows - KV_ROWS, :] = win_ref[0, KV_ROWS:, :]
    wo_ref[0, wrows - KV_ROWS:, :] = new8_ref[0]


def win_step(q3, new_rows, win, pt_flat):
    def build(sample):
        nb, wrows = win.shape[0], win.shape[1]
        per, _ = _sample_specs(sample, 0)
        return (_win_step_kernel, pt_flat,
                [q3, new_rows, new_rows.reshape(nb, KV_ROWS, HEAD_DIM), win],
                [per(N_HEADS, HEAD_DIM), per(1, KV_W), per(KV_ROWS, HEAD_DIM), per(wrows, HEAD_DIM)],
                [per(N_HEADS, HEAD_DIM), per(wrows, HEAD_DIM)],
                [jax.ShapeDtypeStruct((nb, N_HEADS, HEAD_DIM), f32), jax.ShapeDtypeStruct(win.shape, f32)],
                [pltpu.VMEM((N_HEADS, wrows // KV_ROWS), f32)])
    return build


def _rope_tables(pos):
    half = HEAD_DIM // 2
    inv = ROPE_THETA ** (-jnp.arange(half, dtype=f32) / half)
    ang = pos.astype(f32)[:, None] * inv[None, :]
    cos, sin = jnp.cos(ang), jnp.sin(ang)
    return jnp.concatenate([cos, cos], axis=1), jnp.concatenate([-sin, sin], axis=1)


def _pad_cols(a, width):
    return jnp.pad(a, ((0, 0), (0, width - a.shape[1])))


def kernel(x_prompt, x_sample, state_pool, state_mlstm_c, state_mlstm_n, state_mlstm_m, cache_kv_cmp, cache_kv_sel, cache_kv_win, page_table, norm0_g, w_in0, b_gate0, w_pool, pool_scale, mh_norm_g, w_out0, norm1_g, w_in1, b_gate1, w_out1, final_g):
    nbp, t, d = x_prompt.shape
    nbs = x_sample.shape[0]
    mp = nbp * t
    n_pages = page_table.shape[1]
    past_len = n_pages * PAGE_SIZE
    wbuf = cache_kv_win.shape[1]

    w_in0 = w_in0.astype(bf16)
    w_in1 = w_in1.astype(bf16)
    w0 = ([w_in0], [(0, 0, MAIN_W)])
    wg0 = jnp.concatenate([_pad_cols(w_in0[:, MAIN_W:MAIN_W + M_HEADS], LANES),
                           _pad_cols(w_in0[:, MAIN_W + M_HEADS:], LANES)], axis=1)
    bias_i = _pad_cols(b_gate0[None, :M_HEADS], LANES)
    bias_f = _pad_cols(b_gate0[None, M_HEADS:], LANES)
    g_lo = NSA_W + 3 * KV_W
    g_hi = g_lo + 3 * N_HEADS
    w1 = ([w_in1, w_in1[:, g_hi:]], [(0, 0, NSA_W), (1, 0, NSA_W), (0, NSA_W, 3 * KV_W)])
    wg1 = _pad_cols(w_in1[:, g_lo:g_hi], LANES)
    bias1 = _pad_cols(b_gate1[None, :], LANES)
    w_pool_b = w_pool.astype(bf16)
    wo0_pool = w_out0[:POOL_W].astype(bf16)
    wo0_m = w_out0[POOL_W:].astype(bf16)
    wo1 = w_out1.astype(bf16)

    xp = x_prompt.reshape(mp, d)
    xs = x_sample.reshape(nbs, d)

    proj_p, gates_p = norm_proj(xp, norm0_g, w0, wg0, 1024, 1024)
    proj_s, gates_s = norm_proj(xs, norm0_g, w0, wg0, nbs, 1024)

    ypool_p = pool_prompt(proj_p, w_pool_b, pool_scale, nbp, t)
    ym_p, c_p, n_p, m_p = mlstm_prompt(proj_p, gates_p, bias_i, bias_f, mh_norm_g, nbp, t, 256)
    xp1 = out_proj([ypool_p, ym_p], [wo0_pool, wo0_m], xp, 512)

    ypool_s = pool_step(state_pool.reshape(nbs, POOL_STATE * POOL_W), proj_s, w_pool_b, pool_scale)
    m0_pad = _pad_cols(state_mlstm_m, LANES)
    ym_s, c_s, n_s, m_s = mlstm_step(proj_s, gates_s, bias_i, bias_f, mh_norm_g,
                                     state_mlstm_c, state_mlstm_n, m0_pad, 8)
    xs1 = out_proj([ypool_s, ym_s], [wo0_pool, wo0_m], xs, nbs)

    pool_p = proj_p.reshape(nbp, t, MAIN_W)[:, t - POOL_STATE:, :POOL_W]
    pool_s = jnp.concatenate([state_pool[:, 1:], proj_s[:, None, :POOL_W]], axis=1)

    proj1_p, gates1_p = norm_proj(xp1, norm1_g, w1, wg1, 1024, 1024)
    proj1_s, gates1_s = norm_proj(xs1, norm1_g, w1, wg1, nbs, 1024)

    tq = 256
    cos_p, sin_p = _rope_tables(jnp.arange(t))
    qrot_p, kvc_p, kvs_p, kvw_p, blocks_p, kvs_pb, kvw_pb = nsa_prep(proj1_p, cos_p, sin_p, tq, t // tq, True)
    oc_p, sel_p = cmp_prompt(proj1_p, blocks_p, nbp, t, tq)
    expand_t = (jnp.arange(LANES)[None, :] == jnp.arange(t)[:, None] // SEL_BLOCK).astype(bf16)

    cos_s, sin_s = _rope_tables(jnp.full((nbs,), past_len))
    qrot_s, kvs_s, kvw_s = nsa_prep(proj1_s, cos_s, sin_s, nbs, 1, False)
    pt_flat = page_table.reshape(-1)
    q3_s = proj1_s[:, :NSA_W].reshape(nbs, N_HEADS, HEAD_DIM)
    qrot3_s = qrot_s.reshape(nbs, N_HEADS, HEAD_DIM)
    n_pool = cache_kv_cmp.shape[0]
    page_rows = PAGE_SIZE * KV_ROWS

    ow_p, oc_s, sel_s, ow_s, win_new = attn_prompt_with_step(
        qrot_p, kvw_pb, None, None, nbp, t, tq, "win",
        [cmp_step(q3_s, cache_kv_cmp.reshape(n_pool, page_rows, HEAD_DIM), pt_flat, n_pages, past_len),
         win_step(qrot3_s, kvw_s.reshape(nbs, 1, KV_W), cache_kv_win.reshape(nbs, wbuf * KV_ROWS, HEAD_DIM),
                  pt_flat)])
    os_p, os_s = attn_prompt_with_step(
        qrot_p, kvs_pb, sel_p, expand_t, nbp, t, tq, "sel",
        [sel_step(qrot3_s, sel_s, kvs_s.reshape(nbs, 1, KV_W),
                  cache_kv_sel.reshape(n_pool, page_rows, HEAD_DIM), pt_flat, n_pages)])
    y_p = nsa_out(oc_p, os_p, ow_p, gates1_p, bias1, proj1_p, wo1, xp1, final_g, 256)
    y_s = nsa_out(oc_s.reshape(nbs, NSA_W), os_s.reshape(nbs, NSA_W), ow_s.reshape(nbs, NSA_W),
                  gates1_s, bias1, proj1_s, wo1, xs1, final_g, nbs)

    kv5 = lambda a, rows: a.reshape(-1, rows, 2, N_KV, HEAD_DIM)
    return (y_p.reshape(nbp, t, d), y_s.reshape(nbs, 1, d),
            pool_p, pool_s,
            c_p, c_s, n_p, n_s, m_p[:, 0, :M_HEADS], m_s[:, :M_HEADS],
            kv5(kvc_p, t), kv5(proj1_s[:, 2 * NSA_W:2 * NSA_W + KV_W], 1),
            kv5(kvs_p, t), kv5(kvs_s, 1),
            kv5(kvw_p, t)[:, t - wbuf:],
            kv5(win_new, wbuf))
```

```python
import functools

import jax
import jax.numpy as jnp
from jax import lax
from jax.experimental import pallas as pl
from jax.experimental.pallas import tpu as pltpu

f32 = jnp.float32
bf16 = jnp.bfloat16

D_MODEL = 2048
POOL_WINDOWS = (2, 4, 8, 16)
POOL_W = 1024
POOL_GROUP_W = 256
POOL_STATE = 15
M_HEADS = 4
M_W = 1024
M_HEAD_DIM = 256
N_HEADS = 16
HEAD_DIM = 128
N_KV = 4
Q_PER_KV = 4
NSA_W = 2048
KV_W = 1024
KV_ROWS = 2 * N_KV
CMP_BLOCK = 32
SEL_BLOCK = 64
SEL_TOPK = 16
WINDOW = 512
PAGE_SIZE = 128
ROPE_THETA = 10000.0
ATT_SCALE = HEAD_DIM ** -0.5
EPS = 1e-6
MAIN_W = 7168
LANES = 128
NEG = -1e30
MASK_BIAS = -(2.0 ** 100)
LOG2E = 1.4426950408889634
VMEM_LIMIT = 48 * 1024 * 1024

_NT = (((1,), (1,)), ((), ()))


def _cparams(sem):
    return pltpu.CompilerParams(dimension_semantics=sem, vmem_limit_bytes=VMEM_LIMIT)


def _sigmoid(x):
    return 1.0 / (1.0 + jnp.exp(-x))


def _silu(x):
    return x * _sigmoid(x)


def _log_sigmoid(x):
    return jnp.minimum(x, 0.0) - jnp.log1p(jnp.exp(-jnp.abs(x)))


def _mm(a, b):
    return jnp.dot(a, b, preferred_element_type=f32)


def _mm_nt(a, b):
    return lax.dot_general(a, b, _NT, preferred_element_type=f32)


def _norm_proj_kernel(*refs, starts, seg_ref):
    n_w = max(seg_ref) + 1
    x_ref, g_ref = refs[:2]
    w_refs = refs[2:2 + n_w]
    wg_ref, o_ref, og_ref, h_ref = refs[2 + n_w:]
    j = pl.program_id(1)

    @pl.when(j == 0)
    def _():
        x = x_ref[...]
        r = lax.rsqrt(jnp.mean(x * x, axis=-1, keepdims=True) + EPS)
        h = ((x * r) * g_ref[...]).astype(bf16)
        h_ref[...] = h
        og_ref[...] = _mm(h, wg_ref[...])

    for k, a in enumerate(seg_ref):
        @pl.when((j >= starts[k]) & (j < starts[k + 1]))
        def _(w_ref=w_refs[a]):
            o_ref[...] = _mm(h_ref[...], w_ref[...])


def norm_proj(x, g, segs, wg, tm, tn):
    arrays, seg_list = segs
    m, d = x.shape
    ng = wg.shape[1]
    starts = [0]
    for _, _, cols in seg_list:
        starts.append(starts[-1] + cols // tn)
    n_tiles = starts[-1]

    def w_spec(a):
        def index(i, j):
            idx = None
            for k, (ak, col0, cols) in enumerate(seg_list):
                if ak != a:
                    continue
                here = col0 // tn + jnp.clip(j - starts[k], 0, cols // tn - 1)
                idx = here if idx is None else jnp.where(j >= starts[k], here, idx)
            return (0, idx)
        return pl.BlockSpec((d, tn), index)

    return pl.pallas_call(
        functools.partial(_norm_proj_kernel, starts=tuple(starts), seg_ref=tuple(a for a, _, _ in seg_list)),
        grid=(m // tm, n_tiles),
        in_specs=[pl.BlockSpec((tm, d), lambda i, j: (i, 0)),
                  pl.BlockSpec((1, d), lambda i, j: (0, 0))]
                 + [w_spec(a) for a in range(len(arrays))]
                 + [pl.BlockSpec((d, ng), lambda i, j: (0, 0))],
        out_specs=[pl.BlockSpec((tm, tn), lambda i, j: (i, j)),
                   pl.BlockSpec((tm, ng), lambda i, j: (i, 0))],
        out_shape=[jax.ShapeDtypeStruct((m, n_tiles * tn), f32), jax.ShapeDtypeStruct((m, ng), f32)],
        scratch_shapes=[pltpu.VMEM((tm, d), bf16)],
        compiler_params=_cparams(("parallel", "arbitrary")),
        name="norm_proj",
    )(x, g.reshape(1, d), *arrays, wg)


def _out_proj_kernel(*refs, n_parts):
    a_refs = refs[:n_parts]
    w_refs = refs[n_parts:2 * n_parts]
    x_ref, o_ref = refs[2 * n_parts:]
    acc = x_ref[...]
    for a_ref, w_ref in zip(a_refs, w_refs):
        acc = acc + _mm(a_ref[...], w_ref[...])
    o_ref[...] = acc


def out_proj(parts, weights, x, tm):
    m, d = x.shape
    n_parts = len(parts)
    in_specs = [pl.BlockSpec((tm, a.shape[1]), lambda i: (i, 0)) for a in parts]
    in_specs += [pl.BlockSpec(w.shape, lambda i: (0, 0)) for w in weights]
    in_specs += [pl.BlockSpec((tm, d), lambda i: (i, 0))]
    args = list(parts) + list(weights) + [x]
    return pl.pallas_call(
        functools.partial(_out_proj_kernel, n_parts=n_parts),
        grid=(m // tm,),
        in_specs=in_specs,
        out_specs=pl.BlockSpec((tm, d), lambda i: (i, 0)),
        out_shape=jax.ShapeDtypeStruct((m, d), f32),
        compiler_params=_cparams(("parallel",)),
        name="out_proj",
    )(*args)


def _pool_kernel(u_ref, z_ref, w_ref, sc_ref, o_ref):
    g = pl.program_id(1)
    x = u_ref[...]
    row = lax.broadcasted_iota(jnp.int32, x.shape, 0)

    def back(a, s):
        return jnp.where(row >= s, pltpu.roll(a, s, axis=0), 0.0)

    s2 = x + back(x, 1)
    s4 = s2 + back(s2, 2)
    s8 = s4 + back(s4, 4)
    s16 = s8 + back(s8, 8)
    win = jnp.where(g == 0, s2, jnp.where(g == 1, s4, jnp.where(g == 2, s8, s16)))
    wlen = lax.shift_left(jnp.int32(2), g)
    cnt = jnp.minimum(row + 1, wlen).astype(f32)
    pooled = win / cnt - x
    y = _mm(pooled.astype(bf16), w_ref[0]) * sc_ref[...]
    o_ref[...] = (y * _silu(z_ref[...])).astype(bf16)


def pool_prompt(proj, w_pool, pool_scale, nb, t):
    ng = len(POOL_WINDOWS)
    return pl.pallas_call(
        _pool_kernel,
        grid=(nb, ng),
        in_specs=[pl.BlockSpec((t, POOL_GROUP_W), lambda b, g: (b, g)),
                  pl.BlockSpec((t, POOL_GROUP_W), lambda b, g: (b, ng + g)),
                  pl.BlockSpec((1, POOL_GROUP_W, POOL_GROUP_W), lambda b, g: (g, 0, 0)),
                  pl.BlockSpec((1, POOL_GROUP_W), lambda b, g: (0, g))],
        out_specs=pl.BlockSpec((t, POOL_GROUP_W), lambda b, g: (b, g)),
        out_shape=jax.ShapeDtypeStruct((nb * t, POOL_W), bf16),
        compiler_params=_cparams(("parallel", "arbitrary")),
        name="pool_prompt",
    )(proj, proj, w_pool, pool_scale.reshape(1, POOL_W))


def _pool_step_kernel(st_ref, u_ref, z_ref, w_ref, sc_ref, o_ref):
    u = u_ref[...]
    for g, wlen in enumerate(POOL_WINDOWS):
        lo = g * POOL_GROUP_W
        ug = u[:, lo:lo + POOL_GROUP_W]
        acc = ug
        for r in range(POOL_STATE + 1 - wlen, POOL_STATE):
            acc = acc + st_ref[:, r * POOL_W + lo:r * POOL_W + lo + POOL_GROUP_W]
        pooled = acc / float(wlen) - ug
        y = _mm(pooled.astype(bf16), w_ref[g]) * sc_ref[:, lo:lo + POOL_GROUP_W]
        o_ref[:, lo:lo + POOL_GROUP_W] = (y * _silu(z_ref[:, lo:lo + POOL_GROUP_W])).astype(bf16)


def pool_step(state_flat, proj, w_pool, pool_scale):
    nb = proj.shape[0]
    return pl.pallas_call(
        _pool_step_kernel,
        grid=(1,),
        in_specs=[pl.BlockSpec(state_flat.shape, lambda i: (0, 0)),
                  pl.BlockSpec((nb, POOL_W), lambda i: (0, 0)),
                  pl.BlockSpec((nb, POOL_W), lambda i: (0, 1)),
                  pl.BlockSpec(w_pool.shape, lambda i: (0, 0, 0)),
                  pl.BlockSpec((1, POOL_W), lambda i: (0, 0))],
        out_specs=pl.BlockSpec((nb, POOL_W), lambda i: (0, 0)),
        out_shape=jax.ShapeDtypeStruct((nb, POOL_W), bf16),
        compiler_params=_cparams(("arbitrary",)),
        name="pool_step",
    )(state_flat, proj, proj, w_pool, pool_scale.reshape(1, POOL_W))


def _head_out(hc, o, z, g):
    hc = hc * _sigmoid(o)
    hc = hc * lax.rsqrt(jnp.mean(hc * hc, axis=-1, keepdims=True) + EPS)
    return ((hc * g) * _silu(z)).astype(bf16)


def _mlstm_kernel(q_ref, k_ref, v_ref, o_ref, z_ref, gi_ref, gf_ref, bi_ref, bf_ref, mhg_ref,
                  y_ref, c_ref, n_ref, m_ref):
    @pl.when(pl.program_id(1) == 0)
    def _():
        c_ref[...] = jnp.zeros_like(c_ref)
        n_ref[...] = jnp.zeros_like(n_ref)
        m_ref[...] = jnp.zeros_like(m_ref)

    ln = q_ref.shape[0]
    gi = gi_ref[...] + bi_ref[...]
    lf = _log_sigmoid(gf_ref[...] + bf_ref[...])
    row = lax.broadcasted_iota(jnp.int32, lf.shape, 0)
    b = lf
    s = 1
    while s < ln:
        b = b + jnp.where(row >= s, pltpu.roll(b, s, axis=0), 0.0)
        s *= 2
    r_t = (gi - b).T
    tt = lax.broadcasted_iota(jnp.int32, (ln, ln), 0)
    ss = lax.broadcasted_iota(jnp.int32, (ln, ln), 1)
    causal = ss <= tt
    lane = lax.broadcasted_iota(jnp.int32, (1, LANES), 1)
    m_vec = m_ref[0]
    for h in range(M_HEADS):
        hs = slice(h * M_HEAD_DIM, (h + 1) * M_HEAD_DIM)
        b_col = b[:, h:h + 1]
        ig_col = gi[:, h:h + 1]
        m_prev = m_vec[:, h:h + 1]
        inter = b_col + m_prev
        dmat = jnp.where(causal, b_col + r_t[h:h + 1, :], -jnp.inf)
        m_t = jnp.maximum(inter, jnp.max(dmat, axis=1, keepdims=True))
        dw = jnp.exp(dmat - m_t)
        iw = jnp.exp(inter - m_t)
        q = q_ref[:, hs]
        k = k_ref[:, hs] * (M_HEAD_DIM ** -0.5)
        v = v_ref[:, hs]
        qb, kb, vb = q.astype(bf16), k.astype(bf16), v.astype(bf16)
        c = c_ref[0, h]
        n = n_ref[0, h:h + 1, :]
        qk = _mm_nt(qb, kb) * dw
        num = iw * _mm(qb, c.astype(bf16)) + _mm(qk.astype(bf16), vb)
        den = iw * jnp.sum(q * n, axis=1, keepdims=True) + jnp.sum(qk, axis=1, keepdims=True)
        hc = num / jnp.maximum(jnp.abs(den), jnp.exp(-m_t))
        y_ref[:, hs] = _head_out(hc, o_ref[:, hs], z_ref[:, hs], mhg_ref[:, hs])
        m_last = m_t[ln - 1:ln, :]
        b_last = b_col[ln - 1:ln, :]
        ws = jnp.exp(b_last - b_col + ig_col - m_last)
        dec = jnp.exp(b_last + m_prev - m_last)
        kw = ws * k
        c_ref[0, h] = dec * c + _mm(kw.T.astype(bf16), vb)
        n_ref[0, h:h + 1, :] = dec * n + jnp.sum(kw, axis=0, keepdims=True)
        m_vec = jnp.where(lane == h, m_last, m_vec)
    m_ref[0] = m_vec


def mlstm_prompt(proj, gates, bias_i, bias_f, mh_norm_g, nb, t, ln):
    nc = t // ln
    col = lambda cb: pl.BlockSpec((ln, M_W), lambda b, c: (b * nc + c, cb))
    gcol = lambda cb: pl.BlockSpec((ln, LANES), lambda b, c: (b * nc + c, cb))
    vec = lambda w: pl.BlockSpec((1, w), lambda b, c: (0, 0))
    return pl.pallas_call(
        _mlstm_kernel,
        grid=(nb, nc),
        in_specs=[col(2), col(3), col(4), col(5), col(6), gcol(0), gcol(1), vec(LANES), vec(LANES), vec(M_W)],
        out_specs=[pl.BlockSpec((ln, M_W), lambda b, c: (b * nc + c, 0)),
                   pl.BlockSpec((1, M_HEADS, M_HEAD_DIM, M_HEAD_DIM), lambda b, c: (b, 0, 0, 0)),
                   pl.BlockSpec((1, M_HEADS, M_HEAD_DIM), lambda b, c: (b, 0, 0)),
                   pl.BlockSpec((1, 1, LANES), lambda b, c: (b, 0, 0))],
        out_shape=[jax.ShapeDtypeStruct((nb * t, M_W), bf16),
                   jax.ShapeDtypeStruct((nb, M_HEADS, M_HEAD_DIM, M_HEAD_DIM), f32),
                   jax.ShapeDtypeStruct((nb, M_HEADS, M_HEAD_DIM), f32),
                   jax.ShapeDtypeStruct((nb, 1, LANES), f32)],
        compiler_params=_cparams(("parallel", "arbitrary")),
        name="mlstm_prompt",
    )(proj, proj, proj, proj, proj, gates, gates, bias_i, bias_f, mh_norm_g.reshape(1, M_W))


def _mlstm_step_kernel(q_ref, k_ref, v_ref, o_ref, z_ref, gi_ref, gf_ref, bi_ref, bf_ref, mhg_ref,
                       c_ref, n_ref, m_ref, y_ref, co_ref, no_ref, mo_ref):
    nb = q_ref.shape[0]
    gi = gi_ref[...] + bi_ref[...]
    lf = _log_sigmoid(gf_ref[...] + bf_ref[...])
    inter = lf + m_ref[...]
    m_t = jnp.maximum(inter, gi)
    dw_all = jnp.exp(gi - m_t)
    iw_all = jnp.exp(inter - m_t)
    em_all = jnp.exp(-m_t)
    mo_ref[...] = m_t
    d0 = lax.broadcasted_iota(jnp.int32, (M_HEAD_DIM, M_HEAD_DIM), 0)
    d1 = lax.broadcasted_iota(jnp.int32, (M_HEAD_DIM, M_HEAD_DIM), 1)
    eye = d0 == d1
    for j in range(nb):
        for h in range(M_HEADS):
            hs = slice(h * M_HEAD_DIM, (h + 1) * M_HEAD_DIM)
            dw = dw_all[j:j + 1, h:h + 1]
            iw = iw_all[j:j + 1, h:h + 1]
            em = em_all[j:j + 1, h:h + 1]
            q = q_ref[j:j + 1, hs]
            k = k_ref[j:j + 1, hs] * (M_HEAD_DIM ** -0.5)
            v = v_ref[j:j + 1, hs]
            c = c_ref[j, h]
            n = n_ref[j, h:h + 1, :]
            qc = _mm(jnp.broadcast_to(q, (8, M_HEAD_DIM)).astype(bf16), c.astype(bf16))[0:1, :]
            qk = jnp.sum(q * k, axis=1, keepdims=True) * dw
            num = iw * qc + qk * v
            den = iw * jnp.sum(q * n, axis=1, keepdims=True) + qk
            hc = num / jnp.maximum(jnp.abs(den), em)
            y_ref[j:j + 1, hs] = _head_out(hc, o_ref[j:j + 1, hs], z_ref[j:j + 1, hs], mhg_ref[:, hs])
            kdiag = jnp.where(eye, jnp.broadcast_to(k, (M_HEAD_DIM, M_HEAD_DIM)), 0.0).astype(bf16)
            vrep = jnp.broadcast_to(v, (M_HEAD_DIM, M_HEAD_DIM)).astype(bf16)
            co_ref[j, h] = iw * c + dw * _mm(kdiag, vrep)
            no_ref[j, h:h + 1, :] = iw * n + dw * k


def mlstm_step(proj, gates, bias_i, bias_f, mh_norm_g, c0, n0, m0_pad, bb):
    nb = proj.shape[0]
    col = lambda cb: pl.BlockSpec((bb, M_W), lambda i: (i, cb))
    gcol = lambda cb: pl.BlockSpec((bb, LANES), lambda i: (i, cb))
    vec = lambda w: pl.BlockSpec((1, w), lambda i: (0, 0))
    cspec = pl.BlockSpec((bb, M_HEADS, M_HEAD_DIM, M_HEAD_DIM), lambda i: (i, 0, 0, 0))
    nspec = pl.BlockSpec((bb, M_HEADS, M_HEAD_DIM), lambda i: (i, 0, 0))
    return pl.pallas_call(
        _mlstm_step_kernel,
        grid=(nb // bb,),
        in_specs=[col(2), col(3), col(4), col(5), col(6), gcol(0), gcol(1), vec(LANES), vec(LANES), vec(M_W),
                  cspec, nspec, gcol(0)],
        out_specs=[pl.BlockSpec((bb, M_W), lambda i: (i, 0)), cspec, nspec, gcol(0)],
        out_shape=[jax.ShapeDtypeStruct((nb, M_W), bf16),
                   jax.ShapeDtypeStruct(c0.shape, f32),
                   jax.ShapeDtypeStruct(n0.shape, f32),
                   jax.ShapeDtypeStruct((nb, LANES), f32)],
        compiler_params=_cparams(("parallel",)),
        name="mlstm_step",
    )(proj, proj, proj, proj, proj, gates, gates, bias_i, bias_f, mh_norm_g.reshape(1, M_W), c0, n0, m0_pad)


def _rope(x, cos, sin_signed):
    return x * cos + pltpu.roll(x, HEAD_DIM // 2, axis=1) * sin_signed


def _nsa_prep_kernel(*refs, prompt):
    q_ref, kvc_ref, kvs_ref, kvw_ref, cos_ref, sin_ref, qr_ref = refs[:7]
    tq = q_ref.shape[0]
    cos, sin = cos_ref[...], sin_ref[...]
    for h in range(N_HEADS):
        hs = slice(h * HEAD_DIM, (h + 1) * HEAD_DIM)
        qr_ref[:, hs] = _rope(q_ref[:, hs], cos, sin).astype(bf16)

    def chunks(src, rotate):
        for c in range(KV_ROWS):
            x = src[:, c * HEAD_DIM:(c + 1) * HEAD_DIM]
            yield c, (_rope(x, cos, sin) if rotate and c < N_KV else x)

    if prompt:
        kvc_i, kvs_i, kvw_i, blk_ref, kvs_b, kvw_b = refs[7:13]
        for c, x in chunks(kvc_ref, False):
            kvc_i[pl.ds(c, tq, stride=KV_ROWS), :] = x
        for src, dst_i, dst_b in ((kvs_ref, kvs_i, kvs_b), (kvw_ref, kvw_i, kvw_b)):
            for c, x in chunks(src, True):
                dst_i[pl.ds(c, tq, stride=KV_ROWS), :] = x
                dst_b[:, c * HEAD_DIM:(c + 1) * HEAD_DIM] = x.astype(bf16)
        nblk = tq // CMP_BLOCK
        means = jnp.sum(kvc_ref[...].reshape(nblk, CMP_BLOCK, KV_W), axis=1) * (1.0 / CMP_BLOCK)
        for c in range(KV_ROWS):
            blk_ref[pl.ds(c, nblk, stride=KV_ROWS), :] = means[:, c * HEAD_DIM:(c + 1) * HEAD_DIM]
    else:
        for src, dst in ((kvs_ref, refs[7]), (kvw_ref, refs[8])):
            for c, x in chunks(src, True):
                dst[:, c * HEAD_DIM:(c + 1) * HEAD_DIM] = x


def nsa_prep(proj, cos, sin, tq, n_pos_blocks, prompt):
    m = proj.shape[0]
    row = lambda w, cb: pl.BlockSpec((tq, w), lambda i: (i, cb))
    tab = pl.BlockSpec((tq, HEAD_DIM), lambda i: (i % n_pos_blocks, 0))
    if prompt:
        inter = pl.BlockSpec((tq * KV_ROWS, HEAD_DIM), lambda i: (i, 0))
        inter_shape = jax.ShapeDtypeStruct((m * KV_ROWS, HEAD_DIM), f32)
        out_specs = [row(NSA_W, 0), inter, inter, inter,
                     pl.BlockSpec((tq // CMP_BLOCK * KV_ROWS, HEAD_DIM), lambda i: (i, 0)), row(KV_W, 0), row(KV_W, 0)]
        out_shape = [jax.ShapeDtypeStruct((m, NSA_W), bf16), inter_shape, inter_shape, inter_shape,
                     jax.ShapeDtypeStruct((m // CMP_BLOCK * KV_ROWS, HEAD_DIM), f32),
                     jax.ShapeDtypeStruct((m, KV_W), bf16), jax.ShapeDtypeStruct((m, KV_W), bf16)]
    else:
        out_specs = [row(NSA_W, 0), row(KV_W, 0), row(KV_W, 0)]
        out_shape = [jax.ShapeDtypeStruct((m, NSA_W), bf16), jax.ShapeDtypeStruct((m, KV_W), f32),
                     jax.ShapeDtypeStruct((m, KV_W), f32)]
    return pl.pallas_call(
        functools.partial(_nsa_prep_kernel, prompt=prompt),
        grid=(m // tq,),
        in_specs=[row(NSA_W, 0), row(KV_W, 4), row(KV_W, 5), row(KV_W, 6), tab, tab],
        out_specs=out_specs,
        out_shape=out_shape,
        compiler_params=_cparams(("parallel",)),
        name="nsa_prep",
    )(proj, proj, proj, proj, cos, sin)


def _select_blocks(imp, q_pos, n_cand):
    lane = lax.broadcasted_iota(jnp.int32, imp.shape, 1)
    pair = imp + pltpu.roll(imp, LANES - 1, axis=1)
    cur2 = lax.shift_left(lax.shift_right_logical(q_pos, 6), 1)
    valid = ((lane & 1) == 0) & (lane <= cur2)
    v = jnp.where(lane == cur2, jnp.inf, pair)
    v = jnp.where(valid, v, -jnp.inf)
    cnt = jnp.zeros(imp.shape, f32)
    for i in range(n_cand):
        vi = v[:, 2 * i:2 * i + 1]
        before = jnp.where(lane > 2 * i, 1.0, 0.0)
        cnt = cnt + jnp.where(vi > v, 1.0, 0.0) + jnp.where(vi == v, before, 0.0)
    return jnp.where(valid & (cnt < SEL_TOPK), 1.0, 0.0)


def _pad_rows(x, rows):
    return jnp.concatenate([x, jnp.zeros((rows - x.shape[0], x.shape[1]), x.dtype)], axis=0)


def _masked_softmax(s, mask):
    s = jnp.where(mask, s, -jnp.inf)
    m = jnp.max(s, axis=-1, keepdims=True)
    m = jnp.where(m > -jnp.inf, m, 0.0)
    p = jnp.exp(s - m)
    return p / jnp.maximum(jnp.sum(p, axis=-1, keepdims=True), 1e-30)


def _select_block_rows(pair, q_pos):
    row = lax.broadcasted_iota(jnp.int32, pair.shape, 0)
    cur = lax.shift_right_logical(q_pos, 6)
    valid = row <= cur
    v = jnp.where(row == cur, jnp.inf, pair)
    v = jnp.where(valid, v, -jnp.inf)
    cnt = jnp.zeros(pair.shape, f32)
    for i in range(pair.shape[0]):
        vi = v[i:i + 1, :]
        before = jnp.where(row > i, 1.0, 0.0)
        cnt = cnt + jnp.where(vi > v, 1.0, 0.0) + jnp.where(vi == v, before, 0.0)
    return jnp.where(valid & (cnt < SEL_TOPK), 1.0, 0.0)


def _cmp_prompt_kernel(q_ref, blk_ref, oc_ref, sel_ref, pair_sc, flag_sc, *, n_blocks):
    tq = q_ref.shape[0]
    n_sel = n_blocks // 2
    t0 = pl.program_id(1) * tq
    row = lax.broadcasted_iota(jnp.int32, (LANES, tq), 0)
    q_pos = lax.broadcasted_iota(jnp.int32, (LANES, tq), 1) + t0
    vis = (row < n_blocks) & ((row + 1) * CMP_BLOCK - 1 <= q_pos)
    flag_sc[...] = jnp.zeros_like(flag_sc)

    def block_rows(c):
        return _pad_rows(blk_ref[pl.ds(c, n_blocks, stride=KV_ROWS), :], LANES).astype(bf16)

    for g in range(N_KV):
        kg = block_rows(g)
        vg = block_rows(N_KV + g)
        imp = jnp.zeros((LANES, tq), f32)
        for r in range(Q_PER_KV):
            hs = slice((g * Q_PER_KV + r) * HEAD_DIM, (g * Q_PER_KV + r + 1) * HEAD_DIM)
            s = jnp.where(vis, _mm_nt(kg, q_ref[:, hs].astype(bf16)) * ATT_SCALE, -jnp.inf)
            m = jnp.max(s, axis=0, keepdims=True)
            m = jnp.where(m > -jnp.inf, m, 0.0)
            p = jnp.exp(s - m)
            p = p / jnp.maximum(jnp.sum(p, axis=0, keepdims=True), 1e-30)
            oc_ref[:, hs] = _mm(p.T.astype(bf16), vg).astype(bf16)
            imp = imp + p
        pair = imp + pltpu.roll(imp, LANES - 1, axis=0)
        for h in range(tq // LANES):
            ts = slice(h * LANES, (h + 1) * LANES)
            pos = lax.broadcasted_iota(jnp.int32, (n_sel, LANES), 1) + (t0 + h * LANES)
            pair_sc[h] = pair[:, ts]
            flag_sc[:n_sel, :] = _select_block_rows(pair_sc[h, pl.ds(0, n_sel, stride=2), :], pos)
            sel_ref[ts, g * LANES:(g + 1) * LANES] = flag_sc[...].T.astype(bf16)


def cmp_prompt(proj, blocks, nb, t, tq):
    nq = t // tq
    n_blocks = t // CMP_BLOCK
    return pl.pallas_call(
        functools.partial(_cmp_prompt_kernel, n_blocks=n_blocks),
        grid=(nb, nq),
        in_specs=[pl.BlockSpec((tq, NSA_W), lambda b, i: (b * nq + i, 0)),
                  pl.BlockSpec((n_blocks * KV_ROWS, HEAD_DIM), lambda b, i: (b, 0))],
        out_specs=[pl.BlockSpec((tq, NSA_W), lambda b, i: (b * nq + i, 0)),
                   pl.BlockSpec((tq, N_KV * LANES), lambda b, i: (b * nq + i, 0))],
        out_shape=[jax.ShapeDtypeStruct((nb * t, NSA_W), bf16),
                   jax.ShapeDtypeStruct((nb * t, N_KV * LANES), bf16)],
        scratch_shapes=[pltpu.VMEM((tq // LANES, LANES, LANES), f32), pltpu.VMEM((LANES, LANES), f32)],
        compiler_params=_cparams(("parallel", "parallel")),
        name="cmp_prompt",
    )(proj, blocks)


def _attn_kernel(*refs, mode, tile, reach):
    if mode == "sel":
        q_ref, k_ref, v_ref, sel_ref, et_ref, o_ref, qp_sc, s_sc, mx_sc, mb_sc, acc_sc, rel_sc = refs
    else:
        q_ref, k_ref, v_ref, o_ref, qp_sc, s_sc, mx_sc, mb_sc, acc_sc, rel_sc = refs
    qi = pl.program_id(2)
    rows = Q_PER_KV * tile

    @pl.when(qi == 0)
    def _():
        rel_sc[...] = ((lax.broadcasted_iota(jnp.int32, rel_sc.shape, 0) & (tile - 1))
                       - lax.broadcasted_iota(jnp.int32, rel_sc.shape, 1))

    for r in range(Q_PER_KV):
        qh = q_ref[:, r * HEAD_DIM:(r + 1) * HEAD_DIM]
        if mode == "sel":
            bias = ((1.0 - sel_ref[...].astype(f32)) * MASK_BIAS).astype(bf16)
            qh = jnp.concatenate([qh, bias], axis=1)
        qp_sc[r * tile:(r + 1) * tile, :] = qh
    lo = 0 if mode == "sel" else jnp.maximum(qi - reach, 0)

    def scores(kj, n, masked):
        start = pl.multiple_of(kj * tile, tile)
        kt = k_ref[pl.ds(start, n * tile), :]
        if mode == "sel":
            kt = jnp.concatenate([kt, et_ref[pl.ds(start, n * tile), :]], axis=1)
        s = _mm_nt(qp_sc[...], kt)
        if not masked:
            return s
        dpos = rel_sc[:, :n * tile] + (qi - kj) * tile
        if mode == "win":
            ok = pltpu.bitcast(dpos, jnp.uint32) <= WINDOW
        else:
            ok = dpos >= 0
        return jnp.where(ok, s, MASK_BIAS)

    def store_scores(kj, n, s):
        mx = mx_sc[...]
        for i in range(n):
            s_sc[kj + i] = s[:, i * tile:(i + 1) * tile]
        for c in range(n * tile // LANES):
            mx = jnp.maximum(mx, s[:, c * LANES:(c + 1) * LANES])
        mx_sc[...] = mx

    def weigh(kj, n):
        start = pl.multiple_of(kj * tile, tile)
        mb = mb_sc[...]
        p = jnp.concatenate([jnp.exp2((s_sc[kj + i][:, c * LANES:(c + 1) * LANES] - mb) * (ATT_SCALE * LOG2E))
                             for i in range(n) for c in range(tile // LANES)], axis=1)
        vt = jnp.concatenate([v_ref[pl.ds(start, n * tile), :], jnp.ones((n * tile, HEAD_DIM), bf16)], axis=1)
        acc_sc[...] += _mm(p.astype(bf16), vt)

    def in_pairs(first, count, fn):
        def trip(i, carry):
            fn(first + 2 * i, 2)
            return carry

        lax.fori_loop(0, count // 2, trip, 0)

        @pl.when(count % 2 == 1)
        def _():
            fn(first + count - 1, 1)

    mx_sc[...] = jnp.full_like(mx_sc, MASK_BIAS)
    if mode == "sel":
        in_pairs(lo, qi - lo, lambda kj, n: store_scores(kj, n, scores(kj, n, False)))
        store_scores(qi, 1, scores(qi, 1, True))
    else:
        in_pairs(lo, qi - lo + 1, lambda kj, n: store_scores(kj, n, scores(kj, n, True)))
    m = jnp.max(mx_sc[...], axis=1, keepdims=True)
    mb_sc[...] = jnp.broadcast_to(m, mb_sc.shape)
    acc_sc[...] = jnp.zeros_like(acc_sc)
    in_pairs(lo, qi - lo + 1, weigh)
    acc = acc_sc[...]
    o = (acc[:, :HEAD_DIM] / acc[:, HEAD_DIM:]).astype(bf16)
    for r in range(Q_PER_KV):
        o_ref[:, r * HEAD_DIM:(r + 1) * HEAD_DIM] = o[r * tile:(r + 1) * tile, :]


def _attn_step_kernel(pt_ref, *refs, counts, bodies):
    n_in = sum(c[0] for c in counts)
    n_out = sum(c[1] for c in counts)
    i, o, s = 0, n_in, n_in + n_out
    for k, (body, (ci, co, cs)) in enumerate(zip(bodies, counts)):
        part = (*refs[i:i + ci], *refs[o:o + co], *refs[s:s + cs])
        body(*part) if k == 0 else body(pt_ref, *part)
        i, o, s = i + ci, o + co, s + cs


def attn_prompt_with_step(q_rot, kv_b, sel, expand_t, nb, t, tile, mode, steps):
    nq = t // tile
    reach = -(-WINDOW // tile)
    qw = Q_PER_KV * HEAD_DIM
    kdim = 2 * HEAD_DIM if mode == "sel" else HEAD_DIM
    rows = Q_PER_KV * tile
    in_specs = [pl.BlockSpec((tile, qw), lambda b, g, i, pt: (b * nq + i, g)),
                pl.BlockSpec((t, HEAD_DIM), lambda b, g, i, pt: (b, g)),
                pl.BlockSpec((t, HEAD_DIM), lambda b, g, i, pt: (b, N_KV + g))]
    args = [q_rot, kv_b, kv_b]
    if mode == "sel":
        in_specs += [pl.BlockSpec((tile, LANES), lambda b, g, i, pt: (b * nq + i, g)),
                     pl.BlockSpec((t, LANES), lambda b, g, i, pt: (0, 0))]
        args += [sel, expand_t]
    scratch = [pltpu.VMEM((rows, kdim), bf16), pltpu.VMEM((nq, rows, tile), f32), pltpu.VMEM((rows, LANES), f32),
               pltpu.VMEM((rows, LANES), f32), pltpu.VMEM((rows, 2 * HEAD_DIM), f32),
               pltpu.VMEM((rows, 2 * tile), jnp.int32)]
    out_specs = [pl.BlockSpec((tile, qw), lambda b, g, i, pt: (b * nq + i, g))]
    out_shape = [jax.ShapeDtypeStruct((nb * t, NSA_W), bf16)]
    bodies = [functools.partial(_attn_kernel, mode=mode, tile=tile, reach=reach)]
    counts = [(len(in_specs), 1, len(scratch))]
    pt_flat = None
    for step in steps:
        body, pt_flat, s_args, s_in, s_out, s_shape, s_scratch = step(lambda b, g, i: (b * N_KV + g) * nq + i)
        assert s_shape[0].shape[0] == nb * N_KV * nq, "one sample per attention grid step"
        bodies.append(body)
        counts.append((len(s_in), len(s_out), len(s_scratch)))
        args, in_specs, out_specs = args + s_args, in_specs + s_in, out_specs + s_out
        out_shape, scratch = out_shape + s_shape, scratch + s_scratch
    return pl.pallas_call(
        functools.partial(_attn_step_kernel, counts=tuple(counts), bodies=tuple(bodies)),
        grid_spec=pltpu.PrefetchScalarGridSpec(
            num_scalar_prefetch=1,
            grid=(nb, N_KV, nq),
            in_specs=in_specs,
            out_specs=out_specs,
            scratch_shapes=scratch),
        out_shape=out_shape,
        compiler_params=_cparams(("parallel", "parallel", "arbitrary")),
        name="attn_" + mode + "_step",
    )(pt_flat, *args)


def _nsa_out_kernel(oc_ref, os_ref, ow_ref, g_ref, b_ref, z_ref, w_ref, x_ref, fg_ref, o_ref, a0_sc, a1_sc):
    i = pl.program_id(0)

    @pl.when(i == 0)
    def _():
        a1_sc[...] = jnp.zeros_like(a1_sc)

    def step(wr_sc, rd_sc):
        gate = _sigmoid(g_ref[...] + b_ref[...])
        for h in range(N_HEADS):
            hs = slice(h * HEAD_DIM, (h + 1) * HEAD_DIM)
            o = (gate[:, 3 * h:3 * h + 1] * oc_ref[:, hs].astype(f32)
                 + gate[:, 3 * h + 1:3 * h + 2] * os_ref[:, hs].astype(f32)
                 + gate[:, 3 * h + 2:3 * h + 3] * ow_ref[:, hs].astype(f32))
            wr_sc[:, hs] = (o * _silu(z_ref[:, hs])).astype(bf16)
        acc = x_ref[...] + _mm(rd_sc[...], w_ref[...])
        r = lax.rsqrt(jnp.mean(acc * acc, axis=-1, keepdims=True) + EPS)
        o_ref[...] = (acc * r) * fg_ref[...]

    @pl.when(i % 2 == 0)
    def _():
        step(a0_sc, a1_sc)

    @pl.when(i % 2 == 1)
    def _():
        step(a1_sc, a0_sc)


def nsa_out(o_c, o_s, o_w, gates, bias, proj, w_out, x, final_g, tm):
    m, d = x.shape
    n = m // tm
    ahead = lambda w, cb: pl.BlockSpec((tm, w), lambda i: (jnp.minimum(i, n - 1), cb))
    behind = pl.BlockSpec((tm, d), lambda i: (jnp.maximum(i - 1, 0), 0))
    return pl.pallas_call(
        _nsa_out_kernel,
        grid=(n + 1,),
        in_specs=[ahead(NSA_W, 0), ahead(NSA_W, 0), ahead(NSA_W, 0), ahead(LANES, 0),
                  pl.BlockSpec((1, LANES), lambda i: (0, 0)),
                  ahead(NSA_W, 1),
                  pl.BlockSpec(w_out.shape, lambda i: (0, 0)),
                  behind,
                  pl.BlockSpec((1, d), lambda i: (0, 0))],
        out_specs=behind,
        out_shape=jax.ShapeDtypeStruct((m, d), f32),
        scratch_shapes=[pltpu.VMEM((tm, NSA_W), bf16), pltpu.VMEM((tm, NSA_W), bf16)],
        compiler_params=_cparams(("arbitrary",)),
        name="nsa_out",
    )(o_c, o_s, o_w, gates, bias, proj, w_out, x, final_g.reshape(1, d))


def _head_group(shape):
    return lax.shift_right_logical(lax.broadcasted_iota(jnp.int32, shape, 0), 2)


def _cmp_step_kernel(pt_ref, q_ref, *refs, n_pages, q_pos):
    pages = refs[:n_pages]
    oc_ref, sel_ref, blk_sc = refs[n_pages:]
    per_page = PAGE_SIZE // CMP_BLOCK
    for p in range(n_pages):
        x = pages[p][0].reshape(per_page, CMP_BLOCK, KV_ROWS, HEAD_DIM)
        means = jnp.sum(x, axis=1) * (1.0 / CMP_BLOCK)
        blk_sc[p * per_page * KV_ROWS:(p + 1) * per_page * KV_ROWS, :] = means.reshape(per_page * KV_ROWS, HEAD_DIM)
    n_blocks = blk_sc.shape[0] // KV_ROWS
    qb = q_ref[0].astype(bf16)
    grp = _head_group((N_HEADS, LANES))
    lane = lax.broadcasted_iota(jnp.int32, (N_HEADS, LANES), 1)
    s = jnp.zeros((N_HEADS, LANES), f32)
    for g in range(N_KV):
        kg = _pad_rows(blk_sc[pl.ds(g, n_blocks, stride=KV_ROWS), :], LANES).astype(bf16)
        s = jnp.where(grp == g, _mm_nt(qb, kg), s)
    vis = (lane < n_blocks) & ((lane + 1) * CMP_BLOCK - 1 <= q_pos)
    p = _masked_softmax(s * ATT_SCALE, vis)
    pb = p.astype(bf16)
    o = jnp.zeros((N_HEADS, HEAD_DIM), f32)
    for g in range(N_KV):
        vg = _pad_rows(blk_sc[pl.ds(N_KV + g, n_blocks, stride=KV_ROWS), :], LANES).astype(bf16)
        o = jnp.where(grp == g, _mm(pb, vg), o)
    oc_ref[0] = o
    row8 = lax.broadcasted_iota(jnp.int32, (8, LANES), 0)
    imp = jnp.zeros((8, LANES), f32)
    for g in range(N_KV):
        imp_g = jnp.sum(p[g * Q_PER_KV:(g + 1) * Q_PER_KV, :], axis=0, keepdims=True)
        imp = jnp.where(row8 == g, imp_g, imp)
    sel_ref[0] = _select_blocks(imp, jnp.full((8, LANES), q_pos, jnp.int32), q_pos // SEL_BLOCK + 1)


def _sample_specs(sample, n_pages):
    per = lambda *shape: pl.BlockSpec((1,) + shape, lambda *a: (sample(*a[:-1]), 0, 0))
    pages = [pl.BlockSpec((1, PAGE_SIZE * KV_ROWS, HEAD_DIM),
                          lambda *a, p=p: (a[-1][sample(*a[:-1]) * n_pages + p], 0, 0)) for p in range(n_pages)]
    return per, pages


def cmp_step(q3, cache, pt_flat, n_pages, q_pos):
    def build(sample):
        nb = q3.shape[0]
        per, pages = _sample_specs(sample, n_pages)
        return (functools.partial(_cmp_step_kernel, n_pages=n_pages, q_pos=q_pos), pt_flat,
                [q3] + [cache] * n_pages,
                [per(N_HEADS, HEAD_DIM)] + pages,
                [per(N_HEADS, HEAD_DIM), per(8, LANES)],
                [jax.ShapeDtypeStruct((nb, N_HEADS, HEAD_DIM), f32), jax.ShapeDtypeStruct((nb, 8, LANES), f32)],
                [pltpu.VMEM((n_pages * PAGE_SIZE // CMP_BLOCK * KV_ROWS, HEAD_DIM), f32)])
    return build


def _decode_attend(qb, n_tiles, kv_tile, flags, new_row, s_sc):
    grp = _head_group((N_HEADS, LANES))
    lane = lax.broadcasted_iota(jnp.int32, (N_HEADS, LANES), 1)
    for p in range(n_tiles):
        sp = jnp.zeros((N_HEADS, LANES), f32)
        for g in range(N_KV):
            sg = _mm_nt(qb, kv_tile(p, g).astype(bf16)) * ATT_SCALE
            if flags is not None:
                f0 = flags[g:g + 1, 4 * p:4 * p + 1]
                f1 = flags[g:g + 1, 4 * p + 2:4 * p + 3]
                sg = jnp.where(jnp.where(lane < SEL_BLOCK, f0, f1) > 0.5, sg, NEG)
            sp = jnp.where(grp == g, sg, sp)
        s_sc[:, p * LANES:(p + 1) * LANES] = sp
    qf = qb.astype(f32)
    s_new = jnp.zeros((N_HEADS, 1), f32)
    grp1 = _head_group((N_HEADS, 1))
    for g in range(N_KV):
        kn = new_row[:, g * HEAD_DIM:(g + 1) * HEAD_DIM].astype(bf16).astype(f32)
        s_new = jnp.where(grp1 == g, jnp.sum(qf * kn, axis=1, keepdims=True) * ATT_SCALE, s_new)
    s_all = s_sc[...]
    m = jnp.maximum(jnp.max(s_all, axis=1, keepdims=True), s_new)
    p_all = jnp.exp(s_all - m)
    p_new = jnp.exp(s_new - m)
    den = jnp.sum(p_all, axis=1, keepdims=True) + p_new
    o = jnp.zeros((N_HEADS, HEAD_DIM), f32)
    for g in range(N_KV):
        vn = new_row[:, KV_W // 2 + g * HEAD_DIM:KV_W // 2 + (g + 1) * HEAD_DIM].astype(bf16).astype(f32)
        o = jnp.where(grp == g, p_new * vn, o)
    for p in range(n_tiles):
        pb = p_all[:, p * LANES:(p + 1) * LANES].astype(bf16)
        for g in range(N_KV):
            o = o + jnp.where(grp == g, _mm(pb, kv_tile(p, N_KV + g).astype(bf16)), 0.0)
    return o / den


def _sel_step_kernel(pt_ref, q_ref, sel_ref, new_ref, *refs, n_pages):
    pages = refs[:n_pages]
    o_ref, s_sc = refs[n_pages:]
    kv_tile = lambda p, c: pages[p][0, pl.ds(c, PAGE_SIZE, stride=KV_ROWS), :]
    o_ref[0] = _decode_attend(q_ref[0], n_pages, kv_tile, sel_ref[0], new_ref[0], s_sc)


def sel_step(q3, sel, new_rows, cache, pt_flat, n_pages):
    def build(sample):
        nb = q3.shape[0]
        per, pages = _sample_specs(sample, n_pages)
        return (functools.partial(_sel_step_kernel, n_pages=n_pages), pt_flat,
                [q3, sel, new_rows] + [cache] * n_pages,
                [per(N_HEADS, HEAD_DIM), per(8, LANES), per(1, KV_W)] + pages,
                [per(N_HEADS, HEAD_DIM)],
                [jax.ShapeDtypeStruct((nb, N_HEADS, HEAD_DIM), f32)],
                [pltpu.VMEM((N_HEADS, n_pages * PAGE_SIZE), f32)])
    return build


def _win_step_kernel(pt_ref, q_ref, new_ref, new8_ref, win_ref, o_ref, wo_ref, s_sc):
    wrows = win_ref.shape[1]
    n_tiles = wrows // (LANES * KV_ROWS)
    kv_tile = lambda p, c: win_ref[0, pl.ds(p * LANES * KV_ROWS + c, LANES, stride=KV_ROWS), :]
    o_ref[0] = _decode_attend(q_ref[0], n_tiles, kv_tile, None, new_ref[0], s_sc)
    wo_ref[0, :wrows - KV_ROWS, :] = win_ref[0, KV_ROWS:, :]
    wo_ref[0, wrows - KV_ROWS:, :] = new8_ref[0]


def win_step(q3, new_rows, win, pt_flat):
    def build(sample):
        nb, wrows = win.shape[0], win.shape[1]
        per, _ = _sample_specs(sample, 0)
        return (_win_step_kernel, pt_flat,
                [q3, new_rows, new_rows.reshape(nb, KV_ROWS, HEAD_DIM), win],
                [per(N_HEADS, HEAD_DIM), per(1, KV_W), per(KV_ROWS, HEAD_DIM), per(wrows, HEAD_DIM)],
                [per(N_HEADS, HEAD_DIM), per(wrows, HEAD_DIM)],
                [jax.ShapeDtypeStruct((nb, N_HEADS, HEAD_DIM), f32), jax.ShapeDtypeStruct(win.shape, f32)],
                [pltpu.VMEM((N_HEADS, wrows // KV_ROWS), f32)])
    return build


def _rope_tables(pos):
    half = HEAD_DIM // 2
    inv = ROPE_THETA ** (-jnp.arange(half, dtype=f32) / half)
    ang = pos.astype(f32)[:, None] * inv[None, :]
    cos, sin = jnp.cos(ang), jnp.sin(ang)
    return jnp.concatenate([cos, cos], axis=1), jnp.concatenate([-sin, sin], axis=1)


def _pad_cols(a, width):
    return jnp.pad(a, ((0, 0), (0, width - a.shape[1])))


def kernel(x_prompt, x_sample, state_pool, state_mlstm_c, state_mlstm_n, state_mlstm_m, cache_kv_cmp, cache_kv_sel, cache_kv_win, page_table, norm0_g, w_in0, b_gate0, w_pool, pool_scale, mh_norm_g, w_out0, norm1_g, w_in1, b_gate1, w_out1, final_g):
    nbp, t, d = x_prompt.shape
    nbs = x_sample.shape[0]
    mp = nbp * t
    n_pages = page_table.shape[1]
    past_len = n_pages * PAGE_SIZE
    wbuf = cache_kv_win.shape[1]

    w_in0 = w_in0.astype(bf16)
    w_in1 = w_in1.astype(bf16)
    w0 = ([w_in0], [(0, 0, MAIN_W)])
    wg0 = jnp.concatenate([_pad_cols(w_in0[:, MAIN_W:MAIN_W + M_HEADS], LANES),
                           _pad_cols(w_in0[:, MAIN_W + M_HEADS:], LANES)], axis=1)
    bias_i = _pad_cols(b_gate0[None, :M_HEADS], LANES)
    bias_f = _pad_cols(b_gate0[None, M_HEADS:], LANES)
    g_lo = NSA_W + 3 * KV_W
    g_hi = g_lo + 3 * N_HEADS
    w1 = ([w_in1, w_in1[:, g_hi:]], [(0, 0, NSA_W), (1, 0, NSA_W), (0, NSA_W, 3 * KV_W)])
    wg1 = _pad_cols(w_in1[:, g_lo:g_hi], LANES)
    bias1 = _pad_cols(b_gate1[None, :], LANES)
    w_pool_b = w_pool.astype(bf16)
    wo0_pool = w_out0[:POOL_W].astype(bf16)
    wo0_m = w_out0[POOL_W:].astype(bf16)
    wo1 = w_out1.astype(bf16)

    xp = x_prompt.reshape(mp, d)
    xs = x_sample.reshape(nbs, d)

    proj_p, gates_p = norm_proj(xp, norm0_g, w0, wg0, 1024, 1024)
    proj_s, gates_s = norm_proj(xs, norm0_g, w0, wg0, nbs, 1024)

    ypool_p = pool_prompt(proj_p, w_pool_b, pool_scale, nbp, t)
    ym_p, c_p, n_p, m_p = mlstm_prompt(proj_p, gates_p, bias_i, bias_f, mh_norm_g, nbp, t, 256)
    xp1 = out_proj([ypool_p, ym_p], [wo0_pool, wo0_m], xp, 512)

    ypool_s = pool_step(state_pool.reshape(nbs, POOL_STATE * POOL_W), proj_s, w_pool_b, pool_scale)
    m0_pad = _pad_cols(state_mlstm_m, LANES)
    ym_s, c_s, n_s, m_s = mlstm_step(proj_s, gates_s, bias_i, bias_f, mh_norm_g,
                                     state_mlstm_c, state_mlstm_n, m0_pad, 8)
    xs1 = out_proj([ypool_s, ym_s], [wo0_pool, wo0_m], xs, nbs)

    pool_p = proj_p.reshape(nbp, t, MAIN_W)[:, t - POOL_STATE:, :POOL_W]
    pool_s = jnp.concatenate([state_pool[:, 1:], proj_s[:, None, :POOL_W]], axis=1)

    proj1_p, gates1_p = norm_proj(xp1, norm1_g, w1, wg1, 1024, 1024)
    proj1_s, gates1_s = norm_proj(xs1, norm1_g, w1, wg1, nbs, 1024)

    tq = 256
    cos_p, sin_p = _rope_tables(jnp.arange(t))
    qrot_p, kvc_p, kvs_p, kvw_p, blocks_p, kvs_pb, kvw_pb = nsa_prep(proj1_p, cos_p, sin_p, tq, t // tq, True)
    oc_p, sel_p = cmp_prompt(proj1_p, blocks_p, nbp, t, tq)
    expand_t = (jnp.arange(LANES)[None, :] == jnp.arange(t)[:, None] // SEL_BLOCK).astype(bf16)

    cos_s, sin_s = _rope_tables(jnp.full((nbs,), past_len))
    qrot_s, kvs_s, kvw_s = nsa_prep(proj1_s, cos_s, sin_s, nbs, 1, False)
    pt_flat = page_table.reshape(-1)
    q3_s = proj1_s[:, :NSA_W].reshape(nbs, N_HEADS, HEAD_DIM)
    qrot3_s = qrot_s.reshape(nbs, N_HEADS, HEAD_DIM)
    n_pool = cache_kv_cmp.shape[0]
    page_rows = PAGE_SIZE * KV_ROWS

    ow_p, oc_s, sel_s, ow_s, win_new = attn_prompt_with_step(
        qrot_p, kvw_pb, None, None, nbp, t, tq, "win",
        [cmp_step(q3_s, cache_kv_cmp.reshape(n_pool, page_rows, HEAD_DIM), pt_flat, n_pages, past_len),
         win_step(qrot3_s, kvw_s.reshape(nbs, 1, KV_W), cache_kv_win.reshape(nbs, wbuf * KV_ROWS, HEAD_DIM),
                  pt_flat)])
    os_p, os_s = attn_prompt_with_step(
        qrot_p, kvs_pb, sel_p, expand_t, nbp, t, tq, "sel",
        [sel_step(qrot3_s, sel_s, kvs_s.reshape(nbs, 1, KV_W),
                  cache_kv_sel.reshape(n_pool, page_rows, HEAD_DIM), pt_flat, n_pages)])
    y_p = nsa_out(oc_p, os_p, ow_p, gates1_p, bias1, proj1_p, wo1, xp1, final_g, 256)
    y_s = nsa_out(oc_s.reshape(nbs, NSA_W), os_s.reshape(nbs, NSA_W), ow_s.reshape(nbs, NSA_W),
                  gates1_s, bias1, proj1_s, wo1, xs1, final_g, nbs)

    kv5 = lambda a, rows: a.reshape(-1, rows, 2, N_KV, HEAD_DIM)
    return (y_p.reshape(nbp, t, d), y_s.reshape(nbs, 1, d),
            pool_p, pool_s,
            c_p, c_s, n_p, n_s, m_p[:, 0, :M_HEADS], m_s[:, :M_HEADS],
            kv5(kvc_p, t), kv5(proj1_s[:, 2 * NSA_W:2 * NSA_W + KV_W], 1),
            kv5(kvs_p, t), kv5(kvs_s, 1),
            kv5(kvw_p, t)[:, t - wbuf:],
            kv5(win_new, wbuf))
```

```python
import functools

import jax
import jax.numpy as jnp
from jax import lax
from jax.experimental import pallas as pl
from jax.experimental.pallas import tpu as pltpu

f32 = jnp.float32
bf16 = jnp.bfloat16

POOL_WINDOWS = (2, 4, 8, 16)
POOL_W = 1024
POOL_GROUP_W = 256
POOL_STATE = 15
M_HEADS = 4
M_W = 1024
M_HEAD_DIM = 256
N_HEADS = 16
HEAD_DIM = 128
N_KV = 4
Q_PER_KV = 4
NSA_W = 2048
KV_W = 1024
KV_ROWS = 2 * N_KV
CMP_BLOCK = 32
SEL_BLOCK = 64
SEL_TOPK = 16
WINDOW = 512
PAGE_SIZE = 128
ROPE_THETA = 10000.0
ATT_SCALE = HEAD_DIM ** -0.5
EPS = 1e-6
MAIN_W = 7168
LANES = 128
SUBLANES = 8
SEL_SHIFT = SEL_BLOCK.bit_length() - 1
GROUP_SHIFT = Q_PER_KV.bit_length() - 1
PROJ_TM, PROJ_TN = 1024, 1024
OUT_TM = 512
NSA_OUT_TM = 256
ATTN_TILE = 256
MLSTM_CHUNK = 256
MLSTM_STEP_BB = 8
NEG = -1e30
MASK_BIAS = -(2.0 ** 100)
LOG2E = 1.4426950408889634
VMEM_LIMIT = 48 * 1024 * 1024

_NT = (((1,), (1,)), ((), ()))


def _cparams(sem):
    return pltpu.CompilerParams(dimension_semantics=sem, vmem_limit_bytes=VMEM_LIMIT)


def _sigmoid(x):
    return 1.0 / (1.0 + jnp.exp(-x))


def _silu(x):
    return x * _sigmoid(x)


def _log_sigmoid(x):
    return jnp.minimum(x, 0.0) - jnp.log1p(jnp.exp(-jnp.abs(x)))


def _mm(a, b):
    return jnp.dot(a, b, preferred_element_type=f32)


def _mm_nt(a, b):
    return lax.dot_general(a, b, _NT, preferred_element_type=f32)


def _norm_proj_kernel(*refs, starts, seg_ref):
    n_w = max(seg_ref) + 1
    x_ref, g_ref = refs[:2]
    w_refs = refs[2:2 + n_w]
    wg_ref, o_ref, og_ref, h_ref = refs[2 + n_w:]
    j = pl.program_id(1)

    @pl.when(j == 0)
    def _():
        x = x_ref[...]
        r = lax.rsqrt(jnp.mean(x * x, axis=-1, keepdims=True) + EPS)
        h = ((x * r) * g_ref[...]).astype(bf16)
        h_ref[...] = h
        og_ref[...] = _mm(h, wg_ref[...])

    for k, a in enumerate(seg_ref):
        @pl.when((j >= starts[k]) & (j < starts[k + 1]))
        def _(w_ref=w_refs[a]):
            o_ref[...] = _mm(h_ref[...], w_ref[...])


def norm_proj(x, g, segs, wg, tm, tn):
    arrays, seg_list = segs
    m, d = x.shape
    ng = wg.shape[1]
    starts = [0]
    for _, _, cols in seg_list:
        starts.append(starts[-1] + cols // tn)
    n_tiles = starts[-1]

    def w_spec(a):
        def index(i, j):
            idx = None
            for k, (ak, col0, cols) in enumerate(seg_list):
                if ak != a:
                    continue
                here = col0 // tn + jnp.clip(j - starts[k], 0, cols // tn - 1)
                idx = here if idx is None else jnp.where(j >= starts[k], here, idx)
            return (0, idx)
        return pl.BlockSpec((d, tn), index)

    return pl.pallas_call(
        functools.partial(_norm_proj_kernel, starts=tuple(starts), seg_ref=tuple(a for a, _, _ in seg_list)),
        grid=(m // tm, n_tiles),
        in_specs=[pl.BlockSpec((tm, d), lambda i, j: (i, 0)),
                  pl.BlockSpec((1, d), lambda i, j: (0, 0))]
                 + [w_spec(a) for a in range(len(arrays))]
                 + [pl.BlockSpec((d, ng), lambda i, j: (0, 0))],
        out_specs=[pl.BlockSpec((tm, tn), lambda i, j: (i, j)),
                   pl.BlockSpec((tm, ng), lambda i, j: (i, 0))],
        out_shape=[jax.ShapeDtypeStruct((m, n_tiles * tn), f32), jax.ShapeDtypeStruct((m, ng), f32)],
        scratch_shapes=[pltpu.VMEM((tm, d), bf16)],
        compiler_params=_cparams(("parallel", "arbitrary")),
        name="norm_proj",
    )(x, g.reshape(1, d), *arrays, wg)


def _out_proj_kernel(*refs, n_parts):
    a_refs = refs[:n_parts]
    w_refs = refs[n_parts:2 * n_parts]
    x_ref, o_ref = refs[2 * n_parts:]
    acc = x_ref[...]
    for a_ref, w_ref in zip(a_refs, w_refs):
        acc = acc + _mm(a_ref[...], w_ref[...])
    o_ref[...] = acc


def out_proj(parts, weights, x, tm):
    m, d = x.shape
    n_parts = len(parts)
    in_specs = [pl.BlockSpec((tm, a.shape[1]), lambda i: (i, 0)) for a in parts]
    in_specs += [pl.BlockSpec(w.shape, lambda i: (0, 0)) for w in weights]
    in_specs += [pl.BlockSpec((tm, d), lambda i: (i, 0))]
    args = list(parts) + list(weights) + [x]
    return pl.pallas_call(
        functools.partial(_out_proj_kernel, n_parts=n_parts),
        grid=(m // tm,),
        in_specs=in_specs,
        out_specs=pl.BlockSpec((tm, d), lambda i: (i, 0)),
        out_shape=jax.ShapeDtypeStruct((m, d), f32),
        compiler_params=_cparams(("parallel",)),
        name="out_proj",
    )(*args)


def _pool_kernel(u_ref, z_ref, w_ref, sc_ref, o_ref):
    g = pl.program_id(1)
    x = u_ref[...]
    row = lax.broadcasted_iota(jnp.int32, x.shape, 0)

    def back(a, s):
        return jnp.where(row >= s, pltpu.roll(a, s, axis=0), 0.0)

    s2 = x + back(x, 1)
    s4 = s2 + back(s2, 2)
    s8 = s4 + back(s4, 4)
    s16 = s8 + back(s8, 8)
    win = jnp.where(g == 0, s2, jnp.where(g == 1, s4, jnp.where(g == 2, s8, s16)))
    wlen = lax.shift_left(jnp.int32(2), g)
    cnt = jnp.minimum(row + 1, wlen).astype(f32)
    pooled = win / cnt - x
    y = _mm(pooled.astype(bf16), w_ref[0]) * sc_ref[...]
    o_ref[...] = (y * _silu(z_ref[...])).astype(bf16)


def pool_prompt(proj, w_pool, pool_scale, nb, t):
    ng = len(POOL_WINDOWS)
    return pl.pallas_call(
        _pool_kernel,
        grid=(nb, ng),
        in_specs=[pl.BlockSpec((t, POOL_GROUP_W), lambda b, g: (b, g)),
                  pl.BlockSpec((t, POOL_GROUP_W), lambda b, g: (b, ng + g)),
                  pl.BlockSpec((1, POOL_GROUP_W, POOL_GROUP_W), lambda b, g: (g, 0, 0)),
                  pl.BlockSpec((1, POOL_GROUP_W), lambda b, g: (0, g))],
        out_specs=pl.BlockSpec((t, POOL_GROUP_W), lambda b, g: (b, g)),
        out_shape=jax.ShapeDtypeStruct((nb * t, POOL_W), bf16),
        compiler_params=_cparams(("parallel", "arbitrary")),
        name="pool_prompt",
    )(proj, proj, w_pool, pool_scale.reshape(1, POOL_W))


def _pool_step_kernel(st_ref, u_ref, z_ref, w_ref, sc_ref, o_ref):
    u = u_ref[...]
    for g, wlen in enumerate(POOL_WINDOWS):
        lo = g * POOL_GROUP_W
        ug = u[:, lo:lo + POOL_GROUP_W]
        acc = ug
        for r in range(POOL_STATE + 1 - wlen, POOL_STATE):
            acc = acc + st_ref[:, r * POOL_W + lo:r * POOL_W + lo + POOL_GROUP_W]
        pooled = acc / float(wlen) - ug
        y = _mm(pooled.astype(bf16), w_ref[g]) * sc_ref[:, lo:lo + POOL_GROUP_W]
        o_ref[:, lo:lo + POOL_GROUP_W] = (y * _silu(z_ref[:, lo:lo + POOL_GROUP_W])).astype(bf16)


def pool_step(state_flat, proj, w_pool, pool_scale):
    nb = proj.shape[0]
    return pl.pallas_call(
        _pool_step_kernel,
        grid=(1,),
        in_specs=[pl.BlockSpec(state_flat.shape, lambda i: (0, 0)),
                  pl.BlockSpec((nb, POOL_W), lambda i: (0, 0)),
                  pl.BlockSpec((nb, POOL_W), lambda i: (0, 1)),
                  pl.BlockSpec(w_pool.shape, lambda i: (0, 0, 0)),
                  pl.BlockSpec((1, POOL_W), lambda i: (0, 0))],
        out_specs=pl.BlockSpec((nb, POOL_W), lambda i: (0, 0)),
        out_shape=jax.ShapeDtypeStruct((nb, POOL_W), bf16),
        compiler_params=_cparams(("arbitrary",)),
        name="pool_step",
    )(state_flat, proj, proj, w_pool, pool_scale.reshape(1, POOL_W))


def _head_out(hc, o, z, g):
    hc = hc * _sigmoid(o)
    hc = hc * lax.rsqrt(jnp.mean(hc * hc, axis=-1, keepdims=True) + EPS)
    return ((hc * g) * _silu(z)).astype(bf16)


def _mlstm_kernel(q_ref, k_ref, v_ref, o_ref, z_ref, gi_ref, gf_ref, bi_ref, bf_ref, mhg_ref,
                  y_ref, c_ref, n_ref, m_ref):
    @pl.when(pl.program_id(1) == 0)
    def _():
        c_ref[...] = jnp.zeros_like(c_ref)
        n_ref[...] = jnp.zeros_like(n_ref)
        m_ref[...] = jnp.zeros_like(m_ref)

    ln = q_ref.shape[0]
    gi = gi_ref[...] + bi_ref[...]
    lf = _log_sigmoid(gf_ref[...] + bf_ref[...])
    row = lax.broadcasted_iota(jnp.int32, lf.shape, 0)
    b = lf
    s = 1
    while s < ln:
        b = b + jnp.where(row >= s, pltpu.roll(b, s, axis=0), 0.0)
        s *= 2
    r_t = (gi - b).T
    tt = lax.broadcasted_iota(jnp.int32, (ln, ln), 0)
    ss = lax.broadcasted_iota(jnp.int32, (ln, ln), 1)
    causal = ss <= tt
    lane = lax.broadcasted_iota(jnp.int32, (1, LANES), 1)
    m_vec = m_ref[0]
    for h in range(M_HEADS):
        hs = slice(h * M_HEAD_DIM, (h + 1) * M_HEAD_DIM)
        b_col = b[:, h:h + 1]
        ig_col = gi[:, h:h + 1]
        m_prev = m_vec[:, h:h + 1]
        inter = b_col + m_prev
        dmat = jnp.where(causal, b_col + r_t[h:h + 1, :], -jnp.inf)
        m_t = jnp.maximum(inter, jnp.max(dmat, axis=1, keepdims=True))
        dw = jnp.exp(dmat - m_t)
        iw = jnp.exp(inter - m_t)
        q = q_ref[:, hs]
        k = k_ref[:, hs] * (M_HEAD_DIM ** -0.5)
        v = v_ref[:, hs]
        qb, kb, vb = q.astype(bf16), k.astype(bf16), v.astype(bf16)
        c = c_ref[0, h]
        n = n_ref[0, h:h + 1, :]
        qk = _mm_nt(qb, kb) * dw
        num = iw * _mm(qb, c.astype(bf16)) + _mm(qk.astype(bf16), vb)
        den = iw * jnp.sum(q * n, axis=1, keepdims=True) + jnp.sum(qk, axis=1, keepdims=True)
        hc = num / jnp.maximum(jnp.abs(den), jnp.exp(-m_t))
        y_ref[:, hs] = _head_out(hc, o_ref[:, hs], z_ref[:, hs], mhg_ref[:, hs])
        m_last = m_t[ln - 1:ln, :]
        b_last = b_col[ln - 1:ln, :]
        ws = jnp.exp(b_last - b_col + ig_col - m_last)
        dec = jnp.exp(b_last + m_prev - m_last)
        kw = ws * k
        c_ref[0, h] = dec * c + _mm(kw.T.astype(bf16), vb)
        n_ref[0, h:h + 1, :] = dec * n + jnp.sum(kw, axis=0, keepdims=True)
        m_vec = jnp.where(lane == h, m_last, m_vec)
    m_ref[0] = m_vec


def mlstm_prompt(proj, gates, bias_i, bias_f, mh_norm_g, nb, t, ln):
    nc = t // ln
    col = lambda cb: pl.BlockSpec((ln, M_W), lambda b, c: (b * nc + c, cb))
    gcol = lambda cb: pl.BlockSpec((ln, LANES), lambda b, c: (b * nc + c, cb))
    vec = lambda w: pl.BlockSpec((1, w), lambda b, c: (0, 0))
    return pl.pallas_call(
        _mlstm_kernel,
        grid=(nb, nc),
        in_specs=[col(2), col(3), col(4), col(5), col(6), gcol(0), gcol(1), vec(LANES), vec(LANES), vec(M_W)],
        out_specs=[pl.BlockSpec((ln, M_W), lambda b, c: (b * nc + c, 0)),
                   pl.BlockSpec((1, M_HEADS, M_HEAD_DIM, M_HEAD_DIM), lambda b, c: (b, 0, 0, 0)),
                   pl.BlockSpec((1, M_HEADS, M_HEAD_DIM), lambda b, c: (b, 0, 0)),
                   pl.BlockSpec((1, 1, LANES), lambda b, c: (b, 0, 0))],
        out_shape=[jax.ShapeDtypeStruct((nb * t, M_W), bf16),
                   jax.ShapeDtypeStruct((nb, M_HEADS, M_HEAD_DIM, M_HEAD_DIM), f32),
                   jax.ShapeDtypeStruct((nb, M_HEADS, M_HEAD_DIM), f32),
                   jax.ShapeDtypeStruct((nb, 1, LANES), f32)],
        compiler_params=_cparams(("parallel", "arbitrary")),
        name="mlstm_prompt",
    )(proj, proj, proj, proj, proj, gates, gates, bias_i, bias_f, mh_norm_g.reshape(1, M_W))


def _mlstm_step_kernel(q_ref, k_ref, v_ref, o_ref, z_ref, gi_ref, gf_ref, bi_ref, bf_ref, mhg_ref,
                       c_ref, n_ref, m_ref, y_ref, co_ref, no_ref, mo_ref):
    nb = q_ref.shape[0]
    gi = gi_ref[...] + bi_ref[...]
    lf = _log_sigmoid(gf_ref[...] + bf_ref[...])
    inter = lf + m_ref[...]
    m_t = jnp.maximum(inter, gi)
    dw_all = jnp.exp(gi - m_t)
    iw_all = jnp.exp(inter - m_t)
    em_all = jnp.exp(-m_t)
    mo_ref[...] = m_t
    d0 = lax.broadcasted_iota(jnp.int32, (M_HEAD_DIM, M_HEAD_DIM), 0)
    d1 = lax.broadcasted_iota(jnp.int32, (M_HEAD_DIM, M_HEAD_DIM), 1)
    eye = d0 == d1
    for j in range(nb):
        for h in range(M_HEADS):
            hs = slice(h * M_HEAD_DIM, (h + 1) * M_HEAD_DIM)
            dw = dw_all[j:j + 1, h:h + 1]
            iw = iw_all[j:j + 1, h:h + 1]
            em = em_all[j:j + 1, h:h + 1]
            q = q_ref[j:j + 1, hs]
            k = k_ref[j:j + 1, hs] * (M_HEAD_DIM ** -0.5)
            v = v_ref[j:j + 1, hs]
            c = c_ref[j, h]
            n = n_ref[j, h:h + 1, :]
            qc = _mm(jnp.broadcast_to(q, (8, M_HEAD_DIM)).astype(bf16), c.astype(bf16))[0:1, :]
            qk = jnp.sum(q * k, axis=1, keepdims=True) * dw
            num = iw * qc + qk * v
            den = iw * jnp.sum(q * n, axis=1, keepdims=True) + qk
            hc = num / jnp.maximum(jnp.abs(den), em)
            y_ref[j:j + 1, hs] = _head_out(hc, o_ref[j:j + 1, hs], z_ref[j:j + 1, hs], mhg_ref[:, hs])
            kdiag = jnp.where(eye, jnp.broadcast_to(k, (M_HEAD_DIM, M_HEAD_DIM)), 0.0).astype(bf16)
            vrep = jnp.broadcast_to(v, (M_HEAD_DIM, M_HEAD_DIM)).astype(bf16)
            co_ref[j, h] = iw * c + dw * _mm(kdiag, vrep)
            no_ref[j, h:h + 1, :] = iw * n + dw * k


def mlstm_step(proj, gates, bias_i, bias_f, mh_norm_g, c0, n0, m0_pad, bb):
    nb = proj.shape[0]
    col = lambda cb: pl.BlockSpec((bb, M_W), lambda i: (i, cb))
    gcol = lambda cb: pl.BlockSpec((bb, LANES), lambda i: (i, cb))
    vec = lambda w: pl.BlockSpec((1, w), lambda i: (0, 0))
    cspec = pl.BlockSpec((bb, M_HEADS, M_HEAD_DIM, M_HEAD_DIM), lambda i: (i, 0, 0, 0))
    nspec = pl.BlockSpec((bb, M_HEADS, M_HEAD_DIM), lambda i: (i, 0, 0))
    return pl.pallas_call(
        _mlstm_step_kernel,
        grid=(nb // bb,),
        in_specs=[col(2), col(3), col(4), col(5), col(6), gcol(0), gcol(1), vec(LANES), vec(LANES), vec(M_W),
                  cspec, nspec, gcol(0)],
        out_specs=[pl.BlockSpec((bb, M_W), lambda i: (i, 0)), cspec, nspec, gcol(0)],
        out_shape=[jax.ShapeDtypeStruct((nb, M_W), bf16),
                   jax.ShapeDtypeStruct(c0.shape, f32),
                   jax.ShapeDtypeStruct(n0.shape, f32),
                   jax.ShapeDtypeStruct((nb, LANES), f32)],
        compiler_params=_cparams(("parallel",)),
        name="mlstm_step",
    )(proj, proj, proj, proj, proj, gates, gates, bias_i, bias_f, mh_norm_g.reshape(1, M_W), c0, n0, m0_pad)


def _rope(x, cos, sin_signed):
    return x * cos + pltpu.roll(x, HEAD_DIM // 2, axis=1) * sin_signed


def _nsa_prep_kernel(*refs, prompt):
    q_ref, kvc_ref, kvs_ref, kvw_ref, cos_ref, sin_ref, qr_ref = refs[:7]
    tq = q_ref.shape[0]
    cos, sin = cos_ref[...], sin_ref[...]
    for h in range(N_HEADS):
        hs = slice(h * HEAD_DIM, (h + 1) * HEAD_DIM)
        qr_ref[:, hs] = _rope(q_ref[:, hs], cos, sin).astype(bf16)

    def chunks(src, rotate):
        for c in range(KV_ROWS):
            x = src[:, c * HEAD_DIM:(c + 1) * HEAD_DIM]
            yield c, (_rope(x, cos, sin) if rotate and c < N_KV else x)

    if prompt:
        kvc_i, kvs_i, kvw_i, blk_ref, kvs_b, kvw_b = refs[7:13]
        for c, x in chunks(kvc_ref, False):
            kvc_i[pl.ds(c, tq, stride=KV_ROWS), :] = x
        for src, dst_i, dst_b in ((kvs_ref, kvs_i, kvs_b), (kvw_ref, kvw_i, kvw_b)):
            for c, x in chunks(src, True):
                dst_i[pl.ds(c, tq, stride=KV_ROWS), :] = x
                dst_b[:, c * HEAD_DIM:(c + 1) * HEAD_DIM] = x.astype(bf16)
        nblk = tq // CMP_BLOCK
        means = jnp.sum(kvc_ref[...].reshape(nblk, CMP_BLOCK, KV_W), axis=1) * (1.0 / CMP_BLOCK)
        for c in range(KV_ROWS):
            blk_ref[pl.ds(c, nblk, stride=KV_ROWS), :] = means[:, c * HEAD_DIM:(c + 1) * HEAD_DIM]
    else:
        for src, dst in ((kvs_ref, refs[7]), (kvw_ref, refs[8])):
            for c, x in chunks(src, True):
                dst[:, c * HEAD_DIM:(c + 1) * HEAD_DIM] = x


def nsa_prep(proj, cos, sin, tq, n_pos_blocks, prompt):
    m = proj.shape[0]
    row = lambda w, cb: pl.BlockSpec((tq, w), lambda i: (i, cb))
    tab = pl.BlockSpec((tq, HEAD_DIM), lambda i: (i % n_pos_blocks, 0))
    if prompt:
        inter = pl.BlockSpec((tq * KV_ROWS, HEAD_DIM), lambda i: (i, 0))
        inter_shape = jax.ShapeDtypeStruct((m * KV_ROWS, HEAD_DIM), f32)
        out_specs = [row(NSA_W, 0), inter, inter, inter,
                     pl.BlockSpec((tq // CMP_BLOCK * KV_ROWS, HEAD_DIM), lambda i: (i, 0)), row(KV_W, 0), row(KV_W, 0)]
        out_shape = [jax.ShapeDtypeStruct((m, NSA_W), bf16), inter_shape, inter_shape, inter_shape,
                     jax.ShapeDtypeStruct((m // CMP_BLOCK * KV_ROWS, HEAD_DIM), f32),
                     jax.ShapeDtypeStruct((m, KV_W), bf16), jax.ShapeDtypeStruct((m, KV_W), bf16)]
    else:
        out_specs = [row(NSA_W, 0), row(KV_W, 0), row(KV_W, 0)]
        out_shape = [jax.ShapeDtypeStruct((m, NSA_W), bf16), jax.ShapeDtypeStruct((m, KV_W), f32),
                     jax.ShapeDtypeStruct((m, KV_W), f32)]
    return pl.pallas_call(
        functools.partial(_nsa_prep_kernel, prompt=prompt),
        grid=(m // tq,),
        in_specs=[row(NSA_W, 0), row(KV_W, 4), row(KV_W, 5), row(KV_W, 6), tab, tab],
        out_specs=out_specs,
        out_shape=out_shape,
        compiler_params=_cparams(("parallel",)),
        name="nsa_prep",
    )(proj, proj, proj, proj, cos, sin)


def _select_blocks(imp, q_pos, n_cand):
    lane = lax.broadcasted_iota(jnp.int32, imp.shape, 1)
    pair = imp + pltpu.roll(imp, LANES - 1, axis=1)
    cur2 = lax.shift_left(lax.shift_right_logical(q_pos, SEL_SHIFT), 1)
    valid = ((lane & 1) == 0) & (lane <= cur2)
    v = jnp.where(lane == cur2, jnp.inf, pair)
    v = jnp.where(valid, v, -jnp.inf)
    cnt = jnp.zeros(imp.shape, f32)
    for i in range(n_cand):
        vi = v[:, 2 * i:2 * i + 1]
        before = jnp.where(lane > 2 * i, 1.0, 0.0)
        cnt = cnt + jnp.where(vi > v, 1.0, 0.0) + jnp.where(vi == v, before, 0.0)
    return jnp.where(valid & (cnt < SEL_TOPK), 1.0, 0.0)


def _pad_rows(x, rows):
    return jnp.concatenate([x, jnp.zeros((rows - x.shape[0], x.shape[1]), x.dtype)], axis=0)


def _masked_softmax(s, mask):
    s = jnp.where(mask, s, -jnp.inf)
    m = jnp.max(s, axis=-1, keepdims=True)
    m = jnp.where(m > -jnp.inf, m, 0.0)
    p = jnp.exp(s - m)
    return p / jnp.maximum(jnp.sum(p, axis=-1, keepdims=True), 1e-30)


def _select_block_rows(pair, q_pos):
    row = lax.broadcasted_iota(jnp.int32, pair.shape, 0)
    cur = lax.shift_right_logical(q_pos, SEL_SHIFT)
    valid = row <= cur
    v = jnp.where(row == cur, jnp.inf, pair)
    v = jnp.where(valid, v, -jnp.inf)
    cnt = jnp.zeros(pair.shape, f32)
    for i in range(pair.shape[0]):
        vi = v[i:i + 1, :]
        before = jnp.where(row > i, 1.0, 0.0)
        cnt = cnt + jnp.where(vi > v, 1.0, 0.0) + jnp.where(vi == v, before, 0.0)
    return jnp.where(valid & (cnt < SEL_TOPK), 1.0, 0.0)


def _cmp_prompt_kernel(q_ref, blk_ref, oc_ref, sel_ref, pair_sc, flag_sc, *, n_blocks):
    tq = q_ref.shape[0]
    n_sel = n_blocks // 2
    t0 = pl.program_id(1) * tq
    row = lax.broadcasted_iota(jnp.int32, (LANES, tq), 0)
    q_pos = lax.broadcasted_iota(jnp.int32, (LANES, tq), 1) + t0
    vis = (row < n_blocks) & ((row + 1) * CMP_BLOCK - 1 <= q_pos)
    flag_sc[...] = jnp.zeros_like(flag_sc)

    def block_rows(c):
        return _pad_rows(blk_ref[pl.ds(c, n_blocks, stride=KV_ROWS), :], LANES).astype(bf16)

    for g in range(N_KV):
        kg = block_rows(g)
        vg = block_rows(N_KV + g)
        imp = jnp.zeros((LANES, tq), f32)
        for r in range(Q_PER_KV):
            hs = slice((g * Q_PER_KV + r) * HEAD_DIM, (g * Q_PER_KV + r + 1) * HEAD_DIM)
            s = jnp.where(vis, _mm_nt(kg, q_ref[:, hs].astype(bf16)) * ATT_SCALE, -jnp.inf)
            m = jnp.max(s, axis=0, keepdims=True)
            m = jnp.where(m > -jnp.inf, m, 0.0)
            p = jnp.exp(s - m)
            p = p / jnp.maximum(jnp.sum(p, axis=0, keepdims=True), 1e-30)
            oc_ref[:, hs] = _mm(p.T.astype(bf16), vg).astype(bf16)
            imp = imp + p
        pair = imp + pltpu.roll(imp, LANES - 1, axis=0)
        for h in range(tq // LANES):
            ts = slice(h * LANES, (h + 1) * LANES)
            pos = lax.broadcasted_iota(jnp.int32, (n_sel, LANES), 1) + (t0 + h * LANES)
            pair_sc[h] = pair[:, ts]
            flag_sc[:n_sel, :] = _select_block_rows(pair_sc[h, pl.ds(0, n_sel, stride=2), :], pos)
            sel_ref[ts, g * LANES:(g + 1) * LANES] = flag_sc[...].T.astype(bf16)


def cmp_prompt(proj, blocks, nb, t, tq):
    nq = t // tq
    n_blocks = t // CMP_BLOCK
    return pl.pallas_call(
        functools.partial(_cmp_prompt_kernel, n_blocks=n_blocks),
        grid=(nb, nq),
        in_specs=[pl.BlockSpec((tq, NSA_W), lambda b, i: (b * nq + i, 0)),
                  pl.BlockSpec((n_blocks * KV_ROWS, HEAD_DIM), lambda b, i: (b, 0))],
        out_specs=[pl.BlockSpec((tq, NSA_W), lambda b, i: (b * nq + i, 0)),
                   pl.BlockSpec((tq, N_KV * LANES), lambda b, i: (b * nq + i, 0))],
        out_shape=[jax.ShapeDtypeStruct((nb * t, NSA_W), bf16),
                   jax.ShapeDtypeStruct((nb * t, N_KV * LANES), bf16)],
        scratch_shapes=[pltpu.VMEM((tq // LANES, LANES, LANES), f32), pltpu.VMEM((LANES, LANES), f32)],
        compiler_params=_cparams(("parallel", "parallel")),
        name="cmp_prompt",
    )(proj, blocks)


def _attn_kernel(*refs, mode, tile, reach):
    if mode == "sel":
        q_ref, k_ref, v_ref, sel_ref, et_ref, o_ref, qp_sc, s_sc, mx_sc, mb_sc, acc_sc, rel_sc = refs
    else:
        q_ref, k_ref, v_ref, o_ref, qp_sc, s_sc, mx_sc, mb_sc, acc_sc, rel_sc = refs
    qi = pl.program_id(2)

    @pl.when(qi == 0)
    def _():
        rel_sc[...] = ((lax.broadcasted_iota(jnp.int32, rel_sc.shape, 0) & (tile - 1))
                       - lax.broadcasted_iota(jnp.int32, rel_sc.shape, 1))

    for r in range(Q_PER_KV):
        qh = q_ref[:, r * HEAD_DIM:(r + 1) * HEAD_DIM]
        if mode == "sel":
            bias = ((1.0 - sel_ref[...].astype(f32)) * MASK_BIAS).astype(bf16)
            qh = jnp.concatenate([qh, bias], axis=1)
        qp_sc[r * tile:(r + 1) * tile, :] = qh
    lo = 0 if mode == "sel" else jnp.maximum(qi - reach, 0)

    def scores(kj, n, masked):
        start = pl.multiple_of(kj * tile, tile)
        kt = k_ref[pl.ds(start, n * tile), :]
        if mode == "sel":
            kt = jnp.concatenate([kt, et_ref[pl.ds(start, n * tile), :]], axis=1)
        s = _mm_nt(qp_sc[...], kt)
        if not masked:
            return s
        dpos = rel_sc[:, :n * tile] + (qi - kj) * tile
        if mode == "win":
            ok = pltpu.bitcast(dpos, jnp.uint32) <= WINDOW
        else:
            ok = dpos >= 0
        return jnp.where(ok, s, MASK_BIAS)

    def store_scores(kj, n, s):
        mx = mx_sc[...]
        for i in range(n):
            s_sc[kj + i] = s[:, i * tile:(i + 1) * tile]
        for c in range(n * tile // LANES):
            mx = jnp.maximum(mx, s[:, c * LANES:(c + 1) * LANES])
        mx_sc[...] = mx

    def weigh(kj, n):
        start = pl.multiple_of(kj * tile, tile)
        mb = mb_sc[...]
        p = jnp.concatenate([jnp.exp2((s_sc[kj + i][:, c * LANES:(c + 1) * LANES] - mb) * (ATT_SCALE * LOG2E))
                             for i in range(n) for c in range(tile // LANES)], axis=1)
        vt = jnp.concatenate([v_ref[pl.ds(start, n * tile), :], jnp.ones((n * tile, HEAD_DIM), bf16)], axis=1)
        acc_sc[...] += _mm(p.astype(bf16), vt)

    def in_pairs(first, count, fn):
        def trip(i, carry):
            fn(first + 2 * i, 2)
            return carry

        lax.fori_loop(0, count // 2, trip, 0)

        @pl.when(count % 2 == 1)
        def _():
            fn(first + count - 1, 1)

    mx_sc[...] = jnp.full_like(mx_sc, MASK_BIAS)
    if mode == "sel":
        in_pairs(lo, qi - lo, lambda kj, n: store_scores(kj, n, scores(kj, n, False)))
        store_scores(qi, 1, scores(qi, 1, True))
    else:
        in_pairs(lo, qi - lo + 1, lambda kj, n: store_scores(kj, n, scores(kj, n, True)))
    m = jnp.max(mx_sc[...], axis=1, keepdims=True)
    mb_sc[...] = jnp.broadcast_to(m, mb_sc.shape)
    acc_sc[...] = jnp.zeros_like(acc_sc)
    in_pairs(lo, qi - lo + 1, weigh)
    acc = acc_sc[...]
    o = (acc[:, :HEAD_DIM] / acc[:, HEAD_DIM:]).astype(bf16)
    for r in range(Q_PER_KV):
        o_ref[:, r * HEAD_DIM:(r + 1) * HEAD_DIM] = o[r * tile:(r + 1) * tile, :]


def _attn_step_kernel(pt_ref, *refs, counts, bodies):
    n_in = sum(c[0] for c in counts)
    n_out = sum(c[1] for c in counts)
    i, o, s = 0, n_in, n_in + n_out
    for k, (body, (ci, co, cs)) in enumerate(zip(bodies, counts)):
        part = (*refs[i:i + ci], *refs[o:o + co], *refs[s:s + cs])
        body(*part) if k == 0 else body(pt_ref, *part)
        i, o, s = i + ci, o + co, s + cs


def attn_prompt_with_step(q_rot, kv_b, sel, expand_t, nb, t, tile, mode, steps):
    nq = t // tile
    reach = -(-WINDOW // tile)
    qw = Q_PER_KV * HEAD_DIM
    kdim = 2 * HEAD_DIM if mode == "sel" else HEAD_DIM
    rows = Q_PER_KV * tile
    in_specs = [pl.BlockSpec((tile, qw), lambda b, g, i, pt: (b * nq + i, g)),
                pl.BlockSpec((t, HEAD_DIM), lambda b, g, i, pt: (b, g)),
                pl.BlockSpec((t, HEAD_DIM), lambda b, g, i, pt: (b, N_KV + g))]
    args = [q_rot, kv_b, kv_b]
    if mode == "sel":
        in_specs += [pl.BlockSpec((tile, LANES), lambda b, g, i, pt: (b * nq + i, g)),
                     pl.BlockSpec((t, LANES), lambda b, g, i, pt: (0, 0))]
        args += [sel, expand_t]
    scratch = [pltpu.VMEM((rows, kdim), bf16), pltpu.VMEM((nq, rows, tile), f32), pltpu.VMEM((rows, LANES), f32),
               pltpu.VMEM((rows, LANES), f32), pltpu.VMEM((rows, 2 * HEAD_DIM), f32),
               pltpu.VMEM((rows, 2 * tile), jnp.int32)]
    out_specs = [pl.BlockSpec((tile, qw), lambda b, g, i, pt: (b * nq + i, g))]
    out_shape = [jax.ShapeDtypeStruct((nb * t, NSA_W), bf16)]
    bodies = [functools.partial(_attn_kernel, mode=mode, tile=tile, reach=reach)]
    counts = [(len(in_specs), 1, len(scratch))]
    pt_flat = None
    for step in steps:
        body, pt_flat, s_args, s_in, s_out, s_shape, s_scratch = step(lambda b, g, i: (b * N_KV + g) * nq + i)
        assert s_shape[0].shape[0] == nb * N_KV * nq, "one sample per attention grid step"
        bodies.append(body)
        counts.append((len(s_in), len(s_out), len(s_scratch)))
        args, in_specs, out_specs = args + s_args, in_specs + s_in, out_specs + s_out
        out_shape, scratch = out_shape + s_shape, scratch + s_scratch
    return pl.pallas_call(
        functools.partial(_attn_step_kernel, counts=tuple(counts), bodies=tuple(bodies)),
        grid_spec=pltpu.PrefetchScalarGridSpec(
            num_scalar_prefetch=1,
            grid=(nb, N_KV, nq),
            in_specs=in_specs,
            out_specs=out_specs,
            scratch_shapes=scratch),
        out_shape=out_shape,
        compiler_params=_cparams(("parallel", "parallel", "arbitrary")),
        name="attn_" + mode + "_step",
    )(pt_flat, *args)


def _nsa_out_kernel(oc_ref, os_ref, ow_ref, g_ref, b_ref, z_ref, w_ref, x_ref, fg_ref, o_ref, a0_sc, a1_sc):
    i = pl.program_id(0)

    @pl.when(i == 0)
    def _():
        a1_sc[...] = jnp.zeros_like(a1_sc)

    def step(wr_sc, rd_sc):
        gate = _sigmoid(g_ref[...] + b_ref[...])
        for h in range(N_HEADS):
            hs = slice(h * HEAD_DIM, (h + 1) * HEAD_DIM)
            o = (gate[:, 3 * h:3 * h + 1] * oc_ref[:, hs].astype(f32)
                 + gate[:, 3 * h + 1:3 * h + 2] * os_ref[:, hs].astype(f32)
                 + gate[:, 3 * h + 2:3 * h + 3] * ow_ref[:, hs].astype(f32))
            wr_sc[:, hs] = (o * _silu(z_ref[:, hs])).astype(bf16)
        acc = x_ref[...] + _mm(rd_sc[...], w_ref[...])
        r = lax.rsqrt(jnp.mean(acc * acc, axis=-1, keepdims=True) + EPS)
        o_ref[...] = (acc * r) * fg_ref[...]

    @pl.when(i % 2 == 0)
    def _():
        step(a0_sc, a1_sc)

    @pl.when(i % 2 == 1)
    def _():
        step(a1_sc, a0_sc)


def nsa_out(o_c, o_s, o_w, gates, bias, proj, w_out, x, final_g, tm):
    m, d = x.shape
    n = m // tm
    ahead = lambda w, cb: pl.BlockSpec((tm, w), lambda i: (jnp.minimum(i, n - 1), cb))
    behind = pl.BlockSpec((tm, d), lambda i: (jnp.maximum(i - 1, 0), 0))
    return pl.pallas_call(
        _nsa_out_kernel,
        grid=(n + 1,),
        in_specs=[ahead(NSA_W, 0), ahead(NSA_W, 0), ahead(NSA_W, 0), ahead(LANES, 0),
                  pl.BlockSpec((1, LANES), lambda i: (0, 0)),
                  ahead(NSA_W, 1),
                  pl.BlockSpec(w_out.shape, lambda i: (0, 0)),
                  behind,
                  pl.BlockSpec((1, d), lambda i: (0, 0))],
        out_specs=behind,
        out_shape=jax.ShapeDtypeStruct((m, d), f32),
        scratch_shapes=[pltpu.VMEM((tm, NSA_W), bf16), pltpu.VMEM((tm, NSA_W), bf16)],
        compiler_params=_cparams(("arbitrary",)),
        name="nsa_out",
    )(o_c, o_s, o_w, gates, bias, proj, w_out, x, final_g.reshape(1, d))


def _head_group(shape):
    return lax.shift_right_logical(lax.broadcasted_iota(jnp.int32, shape, 0), GROUP_SHIFT)


def _cmp_step_kernel(pt_ref, q_ref, *refs, n_pages, q_pos):
    pages = refs[:n_pages]
    oc_ref, sel_ref, blk_sc = refs[n_pages:]
    per_page = PAGE_SIZE // CMP_BLOCK
    for p in range(n_pages):
        x = pages[p][0].reshape(per_page, CMP_BLOCK, KV_ROWS, HEAD_DIM)
        means = jnp.sum(x, axis=1) * (1.0 / CMP_BLOCK)
        blk_sc[p * per_page * KV_ROWS:(p + 1) * per_page * KV_ROWS, :] = means.reshape(per_page * KV_ROWS, HEAD_DIM)
    n_blocks = blk_sc.shape[0] // KV_ROWS
    qb = q_ref[0].astype(bf16)
    grp = _head_group((N_HEADS, LANES))
    lane = lax.broadcasted_iota(jnp.int32, (N_HEADS, LANES), 1)
    s = jnp.zeros((N_HEADS, LANES), f32)
    for g in range(N_KV):
        kg = _pad_rows(blk_sc[pl.ds(g, n_blocks, stride=KV_ROWS), :], LANES).astype(bf16)
        s = jnp.where(grp == g, _mm_nt(qb, kg), s)
    vis = (lane < n_blocks) & ((lane + 1) * CMP_BLOCK - 1 <= q_pos)
    p = _masked_softmax(s * ATT_SCALE, vis)
    pb = p.astype(bf16)
    o = jnp.zeros((N_HEADS, HEAD_DIM), f32)
    for g in range(N_KV):
        vg = _pad_rows(blk_sc[pl.ds(N_KV + g, n_blocks, stride=KV_ROWS), :], LANES).astype(bf16)
        o = jnp.where(grp == g, _mm(pb, vg), o)
    oc_ref[0] = o
    grow = lax.broadcasted_iota(jnp.int32, (SUBLANES, LANES), 0)
    imp = jnp.zeros((SUBLANES, LANES), f32)
    for g in range(N_KV):
        imp_g = jnp.sum(p[g * Q_PER_KV:(g + 1) * Q_PER_KV, :], axis=0, keepdims=True)
        imp = jnp.where(grow == g, imp_g, imp)
    sel_ref[0] = _select_blocks(imp, jnp.full((SUBLANES, LANES), q_pos, jnp.int32), q_pos // SEL_BLOCK + 1)


def _sample_specs(sample, n_pages):
    per = lambda *shape: pl.BlockSpec((1,) + shape, lambda *a: (sample(*a[:-1]), 0, 0))
    pages = [pl.BlockSpec((1, PAGE_SIZE * KV_ROWS, HEAD_DIM),
                          lambda *a, p=p: (a[-1][sample(*a[:-1]) * n_pages + p], 0, 0)) for p in range(n_pages)]
    return per, pages


def cmp_step(q3, cache, pt_flat, n_pages, q_pos):
    def build(sample):
        nb = q3.shape[0]
        per, pages = _sample_specs(sample, n_pages)
        return (functools.partial(_cmp_step_kernel, n_pages=n_pages, q_pos=q_pos), pt_flat,
                [q3] + [cache] * n_pages,
                [per(N_HEADS, HEAD_DIM)] + pages,
                [per(N_HEADS, HEAD_DIM), per(SUBLANES, LANES)],
                [jax.ShapeDtypeStruct((nb, N_HEADS, HEAD_DIM), f32),
                 jax.ShapeDtypeStruct((nb, SUBLANES, LANES), f32)],
                [pltpu.VMEM((n_pages * PAGE_SIZE // CMP_BLOCK * KV_ROWS, HEAD_DIM), f32)])
    return build


def _decode_attend(qb, n_tiles, kv_tile, flags, new_row, s_sc):
    grp = _head_group((N_HEADS, LANES))
    lane = lax.broadcasted_iota(jnp.int32, (N_HEADS, LANES), 1)
    for p in range(n_tiles):
        sp = jnp.zeros((N_HEADS, LANES), f32)
        for g in range(N_KV):
            sg = _mm_nt(qb, kv_tile(p, g).astype(bf16)) * ATT_SCALE
            if flags is not None:
                j0 = p * (LANES // SEL_BLOCK)
                f0 = flags[g:g + 1, 2 * j0:2 * j0 + 1]
                f1 = flags[g:g + 1, 2 * j0 + 2:2 * j0 + 3]
                sg = jnp.where(jnp.where(lane < SEL_BLOCK, f0, f1) > 0.5, sg, NEG)
            sp = jnp.where(grp == g, sg, sp)
        s_sc[:, p * LANES:(p + 1) * LANES] = sp
    qf = qb.astype(f32)
    s_new = jnp.zeros((N_HEADS, 1), f32)
    grp1 = _head_group((N_HEADS, 1))
    for g in range(N_KV):
        kn = new_row[:, g * HEAD_DIM:(g + 1) * HEAD_DIM].astype(bf16).astype(f32)
        s_new = jnp.where(grp1 == g, jnp.sum(qf * kn, axis=1, keepdims=True) * ATT_SCALE, s_new)
    s_all = s_sc[...]
    m = jnp.maximum(jnp.max(s_all, axis=1, keepdims=True), s_new)
    p_all = jnp.exp(s_all - m)
    p_new = jnp.exp(s_new - m)
    den = jnp.sum(p_all, axis=1, keepdims=True) + p_new
    o = jnp.zeros((N_HEADS, HEAD_DIM), f32)
    for g in range(N_KV):
        vn = new_row[:, KV_W // 2 + g * HEAD_DIM:KV_W // 2 + (g + 1) * HEAD_DIM].astype(bf16).astype(f32)
        o = jnp.where(grp == g, p_new * vn, o)
    for p in range(n_tiles):
        pb = p_all[:, p * LANES:(p + 1) * LANES].astype(bf16)
        for g in range(N_KV):
            o = o + jnp.where(grp == g, _mm(pb, kv_tile(p, N_KV + g).astype(bf16)), 0.0)
    return o / den


def _sel_step_kernel(pt_ref, q_ref, sel_ref, new_ref, *refs, n_pages):
    pages = refs[:n_pages]
    o_ref, s_sc = refs[n_pages:]
    kv_tile = lambda p, c: pages[p][0, pl.ds(c, PAGE_SIZE, stride=KV_ROWS), :]
    o_ref[0] = _decode_attend(q_ref[0], n_pages, kv_tile, sel_ref[0], new_ref[0], s_sc)


def sel_step(q3, sel, new_rows, cache, pt_flat, n_pages):
    def build(sample):
        nb = q3.shape[0]
        per, pages = _sample_specs(sample, n_pages)
        return (functools.partial(_sel_step_kernel, n_pages=n_pages), pt_flat,
                [q3, sel, new_rows] + [cache] * n_pages,
                [per(N_HEADS, HEAD_DIM), per(SUBLANES, LANES), per(1, KV_W)] + pages,
                [per(N_HEADS, HEAD_DIM)],
                [jax.ShapeDtypeStruct((nb, N_HEADS, HEAD_DIM), f32)],
                [pltpu.VMEM((N_HEADS, n_pages * PAGE_SIZE), f32)])
    return build


def _win_step_kernel(pt_ref, q_ref, new_ref, new8_ref, win_ref, o_ref, wo_ref, s_sc):
    wrows = win_ref.shape[1]
    n_tiles = wrows // (LANES * KV_ROWS)
    kv_tile = lambda p, c: win_ref[0, pl.ds(p * LANES * KV_ROWS + c, LANES, stride=KV_ROWS), :]
    o_ref[0] = _decode_attend(q_ref[0], n_tiles, kv_tile, None, new_ref[0], s_sc)
    wo_ref[0, :wrows - KV_ROWS, :] = win_ref[0, KV_ROWS:, :]
    wo_ref[0, wrows - KV_ROWS:, :] = new8_ref[0]


def win_step(q3, new_rows, win, pt_flat):
    def build(sample):
        nb, wrows = win.shape[0], win.shape[1]
        per, _ = _sample_specs(sample, 0)
        return (_win_step_kernel, pt_flat,
                [q3, new_rows, new_rows.reshape(nb, KV_ROWS, HEAD_DIM), win],
                [per(N_HEADS, HEAD_DIM), per(1, KV_W), per(KV_ROWS, HEAD_DIM), per(wrows, HEAD_DIM)],
                [per(N_HEADS, HEAD_DIM), per(wrows, HEAD_DIM)],
                [jax.ShapeDtypeStruct((nb, N_HEADS, HEAD_DIM), f32), jax.ShapeDtypeStruct(win.shape, f32)],
                [pltpu.VMEM((N_HEADS, wrows // KV_ROWS), f32)])
    return build


def _rope_tables(pos):
    half = HEAD_DIM // 2
    inv = ROPE_THETA ** (-jnp.arange(half, dtype=f32) / half)
    ang = pos.astype(f32)[:, None] * inv[None, :]
    cos, sin = jnp.cos(ang), jnp.sin(ang)
    return jnp.concatenate([cos, cos], axis=1), jnp.concatenate([-sin, sin], axis=1)


def _pad_cols(a, width):
    return jnp.pad(a, ((0, 0), (0, width - a.shape[1])))


def kernel(x_prompt, x_sample, state_pool, state_mlstm_c, state_mlstm_n, state_mlstm_m, cache_kv_cmp, cache_kv_sel, cache_kv_win, page_table, norm0_g, w_in0, b_gate0, w_pool, pool_scale, mh_norm_g, w_out0, norm1_g, w_in1, b_gate1, w_out1, final_g):
    nbp, t, d = x_prompt.shape
    nbs = x_sample.shape[0]
    mp = nbp * t
    n_pages = page_table.shape[1]
    past_len = n_pages * PAGE_SIZE
    wbuf = cache_kv_win.shape[1]

    w_in0 = w_in0.astype(bf16)
    w_in1 = w_in1.astype(bf16)
    w0 = ([w_in0], [(0, 0, MAIN_W)])
    wg0 = jnp.concatenate([_pad_cols(w_in0[:, MAIN_W:MAIN_W + M_HEADS], LANES),
                           _pad_cols(w_in0[:, MAIN_W + M_HEADS:], LANES)], axis=1)
    bias_i = _pad_cols(b_gate0[None, :M_HEADS], LANES)
    bias_f = _pad_cols(b_gate0[None, M_HEADS:], LANES)
    g_lo = NSA_W + 3 * KV_W
    g_hi = g_lo + 3 * N_HEADS
    w1 = ([w_in1, w_in1[:, g_hi:]], [(0, 0, NSA_W), (1, 0, NSA_W), (0, NSA_W, 3 * KV_W)])
    wg1 = _pad_cols(w_in1[:, g_lo:g_hi], LANES)
    bias1 = _pad_cols(b_gate1[None, :], LANES)
    w_pool_b = w_pool.astype(bf16)
    wo0_pool = w_out0[:POOL_W].astype(bf16)
    wo0_m = w_out0[POOL_W:].astype(bf16)
    wo1 = w_out1.astype(bf16)

    xp = x_prompt.reshape(mp, d)
    xs = x_sample.reshape(nbs, d)

    proj_p, gates_p = norm_proj(xp, norm0_g, w0, wg0, PROJ_TM, PROJ_TN)
    proj_s, gates_s = norm_proj(xs, norm0_g, w0, wg0, nbs, PROJ_TN)

    ypool_p = pool_prompt(proj_p, w_pool_b, pool_scale, nbp, t)
    ym_p, c_p, n_p, m_p = mlstm_prompt(proj_p, gates_p, bias_i, bias_f, mh_norm_g, nbp, t, MLSTM_CHUNK)
    xp1 = out_proj([ypool_p, ym_p], [wo0_pool, wo0_m], xp, OUT_TM)

    ypool_s = pool_step(state_pool.reshape(nbs, POOL_STATE * POOL_W), proj_s, w_pool_b, pool_scale)
    m0_pad = _pad_cols(state_mlstm_m, LANES)
    ym_s, c_s, n_s, m_s = mlstm_step(proj_s, gates_s, bias_i, bias_f, mh_norm_g,
                                     state_mlstm_c, state_mlstm_n, m0_pad, MLSTM_STEP_BB)
    xs1 = out_proj([ypool_s, ym_s], [wo0_pool, wo0_m], xs, nbs)

    pool_p = proj_p.reshape(nbp, t, MAIN_W)[:, t - POOL_STATE:, :POOL_W]
    pool_s = jnp.concatenate([state_pool[:, 1:], proj_s[:, None, :POOL_W]], axis=1)

    proj1_p, gates1_p = norm_proj(xp1, norm1_g, w1, wg1, PROJ_TM, PROJ_TN)
    proj1_s, gates1_s = norm_proj(xs1, norm1_g, w1, wg1, nbs, PROJ_TN)

    tq = ATTN_TILE
    cos_p, sin_p = _rope_tables(jnp.arange(t))
    qrot_p, kvc_p, kvs_p, kvw_p, blocks_p, kvs_pb, kvw_pb = nsa_prep(proj1_p, cos_p, sin_p, tq, t // tq, True)
    oc_p, sel_p = cmp_prompt(proj1_p, blocks_p, nbp, t, tq)
    expand_t = (jnp.arange(LANES)[None, :] == jnp.arange(t)[:, None] // SEL_BLOCK).astype(bf16)

    cos_s, sin_s = _rope_tables(jnp.full((nbs,), past_len))
    qrot_s, kvs_s, kvw_s = nsa_prep(proj1_s, cos_s, sin_s, nbs, 1, False)
    pt_flat = page_table.reshape(-1)
    q3_s = proj1_s[:, :NSA_W].reshape(nbs, N_HEADS, HEAD_DIM)
    qrot3_s = qrot_s.reshape(nbs, N_HEADS, HEAD_DIM)
    n_pool = cache_kv_cmp.shape[0]
    page_rows = PAGE_SIZE * KV_ROWS

    ow_p, oc_s, sel_s, ow_s, win_new = attn_prompt_with_step(
        qrot_p, kvw_pb, None, None, nbp, t, tq, "win",
        [cmp_step(q3_s, cache_kv_cmp.reshape(n_pool, page_rows, HEAD_DIM), pt_flat, n_pages, past_len),
         win_step(qrot3_s, kvw_s.reshape(nbs, 1, KV_W), cache_kv_win.reshape(nbs, wbuf * KV_ROWS, HEAD_DIM),
                  pt_flat)])
    os_p, os_s = attn_prompt_with_step(
        qrot_p, kvs_pb, sel_p, expand_t, nbp, t, tq, "sel",
        [sel_step(qrot3_s, sel_s, kvs_s.reshape(nbs, 1, KV_W),
                  cache_kv_sel.reshape(n_pool, page_rows, HEAD_DIM), pt_flat, n_pages)])
    y_p = nsa_out(oc_p, os_p, ow_p, gates1_p, bias1, proj1_p, wo1, xp1, final_g, NSA_OUT_TM)
    y_s = nsa_out(oc_s.reshape(nbs, NSA_W), os_s.reshape(nbs, NSA_W), ow_s.reshape(nbs, NSA_W),
                  gates1_s, bias1, proj1_s, wo1, xs1, final_g, nbs)

    kv5 = lambda a, rows: a.reshape(-1, rows, 2, N_KV, HEAD_DIM)
    return (y_p.reshape(nbp, t, d), y_s.reshape(nbs, 1, d),
            pool_p, pool_s,
            c_p, c_s, n_p, n_s, m_p[:, 0, :M_HEADS], m_s[:, :M_HEADS],
            kv5(kvc_p, t), kv5(proj1_s[:, 2 * NSA_W:2 * NSA_W + KV_W], 1),
            kv5(kvs_p, t), kv5(kvs_s, 1),
            kv5(kvw_p, t)[:, t - wbuf:],
            kv5(win_new, wbuf))
```

```python
import functools

import jax
import jax.numpy as jnp
from jax import lax
from jax.experimental import pallas as pl
from jax.experimental.pallas import tpu as pltpu

f32 = jnp.float32
bf16 = jnp.bfloat16

POOL_WINDOWS = (2, 4, 8, 16)
POOL_W = 1024
POOL_GROUP_W = 256
POOL_STATE = 15
M_HEADS = 4
M_W = 1024
M_HEAD_DIM = 256
N_HEADS = 16
HEAD_DIM = 128
N_KV = 4
Q_PER_KV = 4
NSA_W = 2048
KV_W = 1024
KV_ROWS = 2 * N_KV
CMP_BLOCK = 32
SEL_BLOCK = 64
SEL_TOPK = 16
WINDOW = 512
PAGE_SIZE = 128
ROPE_THETA = 10000.0
ATT_SCALE = HEAD_DIM ** -0.5
EPS = 1e-6
MAIN_W = 7168
LANES = 128
SUBLANES = 8
SEL_SHIFT = SEL_BLOCK.bit_length() - 1
GROUP_SHIFT = Q_PER_KV.bit_length() - 1
PROJ_TM, PROJ_TN = 1024, 1024
OUT_TM = 512
NSA_OUT_TM = 256
ATTN_TILE = 256
MLSTM_CHUNK = 256
MLSTM_STEP_BB = 8
NEG = -1e30
MASK_BIAS = -(2.0 ** 100)
LOG2E = 1.4426950408889634
VMEM_LIMIT = 48 * 1024 * 1024

_NT = (((1,), (1,)), ((), ()))


def _cparams(sem):
    return pltpu.CompilerParams(dimension_semantics=sem, vmem_limit_bytes=VMEM_LIMIT)


def _sigmoid(x):
    return 1.0 / (1.0 + jnp.exp(-x))


def _silu(x):
    return x * _sigmoid(x)


def _log_sigmoid(x):
    return jnp.minimum(x, 0.0) - jnp.log1p(jnp.exp(-jnp.abs(x)))


def _mm(a, b):
    return jnp.dot(a, b, preferred_element_type=f32)


def _mm_nt(a, b):
    return lax.dot_general(a, b, _NT, preferred_element_type=f32)


def _norm_proj_kernel(*refs, starts, seg_ref):
    n_w = max(seg_ref) + 1
    x_ref, g_ref = refs[:2]
    w_refs = refs[2:2 + n_w]
    wg_ref, o_ref, og_ref, h_ref = refs[2 + n_w:]
    j = pl.program_id(1)

    @pl.when(j == 0)
    def _():
        x = x_ref[...]
        r = lax.rsqrt(jnp.mean(x * x, axis=-1, keepdims=True) + EPS)
        h = ((x * r) * g_ref[...]).astype(bf16)
        h_ref[...] = h
        og_ref[...] = _mm(h, wg_ref[...])

    for k, a in enumerate(seg_ref):
        @pl.when((j >= starts[k]) & (j < starts[k + 1]))
        def _(w_ref=w_refs[a]):
            o_ref[...] = _mm(h_ref[...], w_ref[...])


def norm_proj(x, g, segs, wg, tm, tn):
    arrays, seg_list = segs
    m, d = x.shape
    ng = wg.shape[1]
    starts = [0]
    for _, _, cols in seg_list:
        starts.append(starts[-1] + cols // tn)
    n_tiles = starts[-1]

    def w_spec(a):
        def index(i, j):
            idx = None
            for k, (ak, col0, cols) in enumerate(seg_list):
                if ak != a:
                    continue
                here = col0 // tn + jnp.clip(j - starts[k], 0, cols // tn - 1)
                idx = here if idx is None else jnp.where(j >= starts[k], here, idx)
            return (0, idx)
        return pl.BlockSpec((d, tn), index)

    return pl.pallas_call(
        functools.partial(_norm_proj_kernel, starts=tuple(starts), seg_ref=tuple(a for a, _, _ in seg_list)),
        grid=(m // tm, n_tiles),
        in_specs=[pl.BlockSpec((tm, d), lambda i, j: (i, 0)),
                  pl.BlockSpec((1, d), lambda i, j: (0, 0))]
                 + [w_spec(a) for a in range(len(arrays))]
                 + [pl.BlockSpec((d, ng), lambda i, j: (0, 0))],
        out_specs=[pl.BlockSpec((tm, tn), lambda i, j: (i, j)),
                   pl.BlockSpec((tm, ng), lambda i, j: (i, 0))],
        out_shape=[jax.ShapeDtypeStruct((m, n_tiles * tn), f32), jax.ShapeDtypeStruct((m, ng), f32)],
        scratch_shapes=[pltpu.VMEM((tm, d), bf16)],
        compiler_params=_cparams(("parallel", "arbitrary")),
        name="norm_proj",
    )(x, g.reshape(1, d), *arrays, wg)


def _out_proj_kernel(*refs, n_parts):
    a_refs = refs[:n_parts]
    w_refs = refs[n_parts:2 * n_parts]
    x_ref, o_ref = refs[2 * n_parts:]
    acc = x_ref[...]
    for a_ref, w_ref in zip(a_refs, w_refs):
        acc = acc + _mm(a_ref[...], w_ref[...])
    o_ref[...] = acc


def out_proj(parts, weights, x, tm):
    m, d = x.shape
    n_parts = len(parts)
    in_specs = [pl.BlockSpec((tm, a.shape[1]), lambda i: (i, 0)) for a in parts]
    in_specs += [pl.BlockSpec(w.shape, lambda i: (0, 0)) for w in weights]
    in_specs += [pl.BlockSpec((tm, d), lambda i: (i, 0))]
    args = list(parts) + list(weights) + [x]
    return pl.pallas_call(
        functools.partial(_out_proj_kernel, n_parts=n_parts),
        grid=(m // tm,),
        in_specs=in_specs,
        out_specs=pl.BlockSpec((tm, d), lambda i: (i, 0)),
        out_shape=jax.ShapeDtypeStruct((m, d), f32),
        compiler_params=_cparams(("parallel",)),
        name="out_proj",
    )(*args)


def _pool_kernel(u_ref, z_ref, w_ref, sc_ref, o_ref):
    g = pl.program_id(1)
    x = u_ref[...]
    row = lax.broadcasted_iota(jnp.int32, x.shape, 0)

    def back(a, s):
        return jnp.where(row >= s, pltpu.roll(a, s, axis=0), 0.0)

    s2 = x + back(x, 1)
    s4 = s2 + back(s2, 2)
    s8 = s4 + back(s4, 4)
    s16 = s8 + back(s8, 8)
    win = jnp.where(g == 0, s2, jnp.where(g == 1, s4, jnp.where(g == 2, s8, s16)))
    wlen = lax.shift_left(jnp.int32(2), g)
    cnt = jnp.minimum(row + 1, wlen).astype(f32)
    pooled = win / cnt - x
    y = _mm(pooled.astype(bf16), w_ref[0]) * sc_ref[...]
    o_ref[...] = (y * _silu(z_ref[...])).astype(bf16)


def pool_prompt(proj, w_pool, pool_scale, nb, t):
    ng = len(POOL_WINDOWS)
    return pl.pallas_call(
        _pool_kernel,
        grid=(nb, ng),
        in_specs=[pl.BlockSpec((t, POOL_GROUP_W), lambda b, g: (b, g)),
                  pl.BlockSpec((t, POOL_GROUP_W), lambda b, g: (b, ng + g)),
                  pl.BlockSpec((1, POOL_GROUP_W, POOL_GROUP_W), lambda b, g: (g, 0, 0)),
                  pl.BlockSpec((1, POOL_GROUP_W), lambda b, g: (0, g))],
        out_specs=pl.BlockSpec((t, POOL_GROUP_W), lambda b, g: (b, g)),
        out_shape=jax.ShapeDtypeStruct((nb * t, POOL_W), bf16),
        compiler_params=_cparams(("parallel", "arbitrary")),
        name="pool_prompt",
    )(proj, proj, w_pool, pool_scale.reshape(1, POOL_W))


def _pool_step_kernel(st_ref, u_ref, z_ref, w_ref, sc_ref, o_ref):
    u = u_ref[...]
    for g, wlen in enumerate(POOL_WINDOWS):
        lo = g * POOL_GROUP_W
        ug = u[:, lo:lo + POOL_GROUP_W]
        acc = ug
        for r in range(POOL_STATE + 1 - wlen, POOL_STATE):
            acc = acc + st_ref[:, r * POOL_W + lo:r * POOL_W + lo + POOL_GROUP_W]
        pooled = acc / float(wlen) - ug
        y = _mm(pooled.astype(bf16), w_ref[g]) * sc_ref[:, lo:lo + POOL_GROUP_W]
        o_ref[:, lo:lo + POOL_GROUP_W] = (y * _silu(z_ref[:, lo:lo + POOL_GROUP_W])).astype(bf16)


def pool_step(state_flat, proj, w_pool, pool_scale):
    nb = proj.shape[0]
    return pl.pallas_call(
        _pool_step_kernel,
        grid=(1,),
        in_specs=[pl.BlockSpec(state_flat.shape, lambda i: (0, 0)),
                  pl.BlockSpec((nb, POOL_W), lambda i: (0, 0)),
                  pl.BlockSpec((nb, POOL_W), lambda i: (0, 1)),
                  pl.BlockSpec(w_pool.shape, lambda i: (0, 0, 0)),
                  pl.BlockSpec((1, POOL_W), lambda i: (0, 0))],
        out_specs=pl.BlockSpec((nb, POOL_W), lambda i: (0, 0)),
        out_shape=jax.ShapeDtypeStruct((nb, POOL_W), bf16),
        compiler_params=_cparams(("arbitrary",)),
        name="pool_step",
    )(state_flat, proj, proj, w_pool, pool_scale.reshape(1, POOL_W))


def _head_out(hc, o, z, g):
    hc = hc * _sigmoid(o)
    hc = hc * lax.rsqrt(jnp.mean(hc * hc, axis=-1, keepdims=True) + EPS)
    return ((hc * g) * _silu(z)).astype(bf16)


def _mlstm_kernel(q_ref, k_ref, v_ref, o_ref, z_ref, gi_ref, gf_ref, bi_ref, bf_ref, mhg_ref,
                  y_ref, c_ref, n_ref, m_ref):
    @pl.when(pl.program_id(1) == 0)
    def _():
        c_ref[...] = jnp.zeros_like(c_ref)
        n_ref[...] = jnp.zeros_like(n_ref)
        m_ref[...] = jnp.zeros_like(m_ref)

    ln = q_ref.shape[0]
    gi = gi_ref[...] + bi_ref[...]
    lf = _log_sigmoid(gf_ref[...] + bf_ref[...])
    row = lax.broadcasted_iota(jnp.int32, lf.shape, 0)
    b = lf
    s = 1
    while s < ln:
        b = b + jnp.where(row >= s, pltpu.roll(b, s, axis=0), 0.0)
        s *= 2
    r_t = (gi - b).T
    tt = lax.broadcasted_iota(jnp.int32, (ln, ln), 0)
    ss = lax.broadcasted_iota(jnp.int32, (ln, ln), 1)
    causal = ss <= tt
    lane = lax.broadcasted_iota(jnp.int32, (1, LANES), 1)
    m_vec = m_ref[0]
    for h in range(M_HEADS):
        hs = slice(h * M_HEAD_DIM, (h + 1) * M_HEAD_DIM)
        b_col = b[:, h:h + 1]
        ig_col = gi[:, h:h + 1]
        m_prev = m_vec[:, h:h + 1]
        inter = b_col + m_prev
        dmat = jnp.where(causal, b_col + r_t[h:h + 1, :], -jnp.inf)
        m_t = jnp.maximum(inter, jnp.max(dmat, axis=1, keepdims=True))
        dw = jnp.exp(dmat - m_t)
        iw = jnp.exp(inter - m_t)
        q = q_ref[:, hs]
        k = k_ref[:, hs] * (M_HEAD_DIM ** -0.5)
        v = v_ref[:, hs]
        qb, kb, vb = q.astype(bf16), k.astype(bf16), v.astype(bf16)
        c = c_ref[0, h]
        n = n_ref[0, h:h + 1, :]
        qk = _mm_nt(qb, kb) * dw
        num = iw * _mm(qb, c.astype(bf16)) + _mm(qk.astype(bf16), vb)
        den = iw * jnp.sum(q * n, axis=1, keepdims=True) + jnp.sum(qk, axis=1, keepdims=True)
        hc = num / jnp.maximum(jnp.abs(den), jnp.exp(-m_t))
        y_ref[:, hs] = _head_out(hc, o_ref[:, hs], z_ref[:, hs], mhg_ref[:, hs])
        m_last = m_t[ln - 1:ln, :]
        b_last = b_col[ln - 1:ln, :]
        ws = jnp.exp(b_last - b_col + ig_col - m_last)
        dec = jnp.exp(b_last + m_prev - m_last)
        kw = ws * k
        c_ref[0, h] = dec * c + _mm(kw.T.astype(bf16), vb)
        n_ref[0, h:h + 1, :] = dec * n + jnp.sum(kw, axis=0, keepdims=True)
        m_vec = jnp.where(lane == h, m_last, m_vec)
    m_ref[0] = m_vec


def mlstm_prompt(proj, gates, bias_i, bias_f, mh_norm_g, nb, t, ln):
    nc = t // ln
    col = lambda cb: pl.BlockSpec((ln, M_W), lambda b, c: (b * nc + c, cb))
    gcol = lambda cb: pl.BlockSpec((ln, LANES), lambda b, c: (b * nc + c, cb))
    vec = lambda w: pl.BlockSpec((1, w), lambda b, c: (0, 0))
    return pl.pallas_call(
        _mlstm_kernel,
        grid=(nb, nc),
        in_specs=[col(2), col(3), col(4), col(5), col(6), gcol(0), gcol(1), vec(LANES), vec(LANES), vec(M_W)],
        out_specs=[pl.BlockSpec((ln, M_W), lambda b, c: (b * nc + c, 0)),
                   pl.BlockSpec((1, M_HEADS, M_HEAD_DIM, M_HEAD_DIM), lambda b, c: (b, 0, 0, 0)),
                   pl.BlockSpec((1, M_HEADS, M_HEAD_DIM), lambda b, c: (b, 0, 0)),
                   pl.BlockSpec((1, 1, LANES), lambda b, c: (b, 0, 0))],
        out_shape=[jax.ShapeDtypeStruct((nb * t, M_W), bf16),
                   jax.ShapeDtypeStruct((nb, M_HEADS, M_HEAD_DIM, M_HEAD_DIM), f32),
                   jax.ShapeDtypeStruct((nb, M_HEADS, M_HEAD_DIM), f32),
                   jax.ShapeDtypeStruct((nb, 1, LANES), f32)],
        compiler_params=_cparams(("parallel", "arbitrary")),
        name="mlstm_prompt",
    )(proj, proj, proj, proj, proj, gates, gates, bias_i, bias_f, mh_norm_g.reshape(1, M_W))


def _mlstm_step_kernel(q_ref, k_ref, v_ref, o_ref, z_ref, gi_ref, gf_ref, bi_ref, bf_ref, mhg_ref,
                       c_ref, n_ref, m_ref, y_ref, co_ref, no_ref, mo_ref):
    nb = q_ref.shape[0]
    gi = gi_ref[...] + bi_ref[...]
    lf = _log_sigmoid(gf_ref[...] + bf_ref[...])
    inter = lf + m_ref[...]
    m_t = jnp.maximum(inter, gi)
    dw_all = jnp.exp(gi - m_t)
    iw_all = jnp.exp(inter - m_t)
    em_all = jnp.exp(-m_t)
    mo_ref[...] = m_t
    d0 = lax.broadcasted_iota(jnp.int32, (M_HEAD_DIM, M_HEAD_DIM), 0)
    d1 = lax.broadcasted_iota(jnp.int32, (M_HEAD_DIM, M_HEAD_DIM), 1)
    eye = d0 == d1
    for j in range(nb):
        for h in range(M_HEADS):
            hs = slice(h * M_HEAD_DIM, (h + 1) * M_HEAD_DIM)
            dw = dw_all[j:j + 1, h:h + 1]
            iw = iw_all[j:j + 1, h:h + 1]
            em = em_all[j:j + 1, h:h + 1]
            q = q_ref[j:j + 1, hs]
            k = k_ref[j:j + 1, hs] * (M_HEAD_DIM ** -0.5)
            v = v_ref[j:j + 1, hs]
            c = c_ref[j, h]
            n = n_ref[j, h:h + 1, :]
            qc = _mm(jnp.broadcast_to(q, (8, M_HEAD_DIM)).astype(bf16), c.astype(bf16))[0:1, :]
            qk = jnp.sum(q * k, axis=1, keepdims=True) * dw
            num = iw * qc + qk * v
            den = iw * jnp.sum(q * n, axis=1, keepdims=True) + qk
            hc = num / jnp.maximum(jnp.abs(den), em)
            y_ref[j:j + 1, hs] = _head_out(hc, o_ref[j:j + 1, hs], z_ref[j:j + 1, hs], mhg_ref[:, hs])
            kdiag = jnp.where(eye, jnp.broadcast_to(k, (M_HEAD_DIM, M_HEAD_DIM)), 0.0).astype(bf16)
            vrep = jnp.broadcast_to(v, (M_HEAD_DIM, M_HEAD_DIM)).astype(bf16)
            co_ref[j, h] = iw * c + dw * _mm(kdiag, vrep)
            no_ref[j, h:h + 1, :] = iw * n + dw * k


def mlstm_step(proj, gates, bias_i, bias_f, mh_norm_g, c0, n0, m0_pad, bb):
    nb = proj.shape[0]
    col = lambda cb: pl.BlockSpec((bb, M_W), lambda i: (i, cb))
    gcol = lambda cb: pl.BlockSpec((bb, LANES), lambda i: (i, cb))
    vec = lambda w: pl.BlockSpec((1, w), lambda i: (0, 0))
    cspec = pl.BlockSpec((bb, M_HEADS, M_HEAD_DIM, M_HEAD_DIM), lambda i: (i, 0, 0, 0))
    nspec = pl.BlockSpec((bb, M_HEADS, M_HEAD_DIM), lambda i: (i, 0, 0))
    return pl.pallas_call(
        _mlstm_step_kernel,
        grid=(nb // bb,),
        in_specs=[col(2), col(3), col(4), col(5), col(6), gcol(0), gcol(1), vec(LANES), vec(LANES), vec(M_W),
                  cspec, nspec, gcol(0)],
        out_specs=[pl.BlockSpec((bb, M_W), lambda i: (i, 0)), cspec, nspec, gcol(0)],
        out_shape=[jax.ShapeDtypeStruct((nb, M_W), bf16),
                   jax.ShapeDtypeStruct(c0.shape, f32),
                   jax.ShapeDtypeStruct(n0.shape, f32),
                   jax.ShapeDtypeStruct((nb, LANES), f32)],
        compiler_params=_cparams(("parallel",)),
        name="mlstm_step",
    )(proj, proj, proj, proj, proj, gates, gates, bias_i, bias_f, mh_norm_g.reshape(1, M_W), c0, n0, m0_pad)


def _rope(x, cos, sin_signed):
    return x * cos + pltpu.roll(x, HEAD_DIM // 2, axis=1) * sin_signed


def _nsa_prep_kernel(*refs, prompt):
    q_ref, kvc_ref, kvs_ref, kvw_ref, cos_ref, sin_ref, qr_ref = refs[:7]
    tq = q_ref.shape[0]
    cos, sin = cos_ref[...], sin_ref[...]
    for h in range(N_HEADS):
        hs = slice(h * HEAD_DIM, (h + 1) * HEAD_DIM)
        qr_ref[:, hs] = _rope(q_ref[:, hs], cos, sin).astype(bf16)

    def chunks(src, rotate):
        for c in range(KV_ROWS):
            x = src[:, c * HEAD_DIM:(c + 1) * HEAD_DIM]
            yield c, (_rope(x, cos, sin) if rotate and c < N_KV else x)

    if prompt:
        kvc_i, kvs_i, kvw_i, blk_ref, kvs_b, kvw_b = refs[7:13]
        for c, x in chunks(kvc_ref, False):
            kvc_i[pl.ds(c, tq, stride=KV_ROWS), :] = x
        for src, dst_i, dst_b in ((kvs_ref, kvs_i, kvs_b), (kvw_ref, kvw_i, kvw_b)):
            for c, x in chunks(src, True):
                dst_i[pl.ds(c, tq, stride=KV_ROWS), :] = x
                dst_b[:, c * HEAD_DIM:(c + 1) * HEAD_DIM] = x.astype(bf16)
        nblk = tq // CMP_BLOCK
        means = jnp.sum(kvc_ref[...].reshape(nblk, CMP_BLOCK, KV_W), axis=1) * (1.0 / CMP_BLOCK)
        for c in range(KV_ROWS):
            blk_ref[pl.ds(c, nblk, stride=KV_ROWS), :] = means[:, c * HEAD_DIM:(c + 1) * HEAD_DIM]
    else:
        for src, dst in ((kvs_ref, refs[7]), (kvw_ref, refs[8])):
            for c, x in chunks(src, True):
                dst[:, c * HEAD_DIM:(c + 1) * HEAD_DIM] = x


def nsa_prep(proj, cos, sin, tq, n_pos_blocks, prompt):
    m = proj.shape[0]
    row = lambda w, cb: pl.BlockSpec((tq, w), lambda i: (i, cb))
    tab = pl.BlockSpec((tq, HEAD_DIM), lambda i: (i % n_pos_blocks, 0))
    if prompt:
        inter = pl.BlockSpec((tq * KV_ROWS, HEAD_DIM), lambda i: (i, 0))
        inter_shape = jax.ShapeDtypeStruct((m * KV_ROWS, HEAD_DIM), f32)
        out_specs = [row(NSA_W, 0), inter, inter, inter,
                     pl.BlockSpec((tq // CMP_BLOCK * KV_ROWS, HEAD_DIM), lambda i: (i, 0)), row(KV_W, 0), row(KV_W, 0)]
        out_shape = [jax.ShapeDtypeStruct((m, NSA_W), bf16), inter_shape, inter_shape, inter_shape,
                     jax.ShapeDtypeStruct((m // CMP_BLOCK * KV_ROWS, HEAD_DIM), f32),
                     jax.ShapeDtypeStruct((m, KV_W), bf16), jax.ShapeDtypeStruct((m, KV_W), bf16)]
    else:
        out_specs = [row(NSA_W, 0), row(KV_W, 0), row(KV_W, 0)]
        out_shape = [jax.ShapeDtypeStruct((m, NSA_W), bf16), jax.ShapeDtypeStruct((m, KV_W), f32),
                     jax.ShapeDtypeStruct((m, KV_W), f32)]
    return pl.pallas_call(
        functools.partial(_nsa_prep_kernel, prompt=prompt),
        grid=(m // tq,),
        in_specs=[row(NSA_W, 0), row(KV_W, 4), row(KV_W, 5), row(KV_W, 6), tab, tab],
        out_specs=out_specs,
        out_shape=out_shape,
        compiler_params=_cparams(("parallel",)),
        name="nsa_prep",
    )(proj, proj, proj, proj, cos, sin)


def _select_blocks(imp, q_pos, n_cand):
    lane = lax.broadcasted_iota(jnp.int32, imp.shape, 1)
    pair = imp + pltpu.roll(imp, LANES - 1, axis=1)
    cur2 = lax.shift_left(lax.shift_right_logical(q_pos, SEL_SHIFT), 1)
    valid = ((lane & 1) == 0) & (lane <= cur2)
    v = jnp.where(lane == cur2, jnp.inf, pair)
    v = jnp.where(valid, v, -jnp.inf)
    cnt = jnp.zeros(imp.shape, f32)
    for i in range(n_cand):
        vi = v[:, 2 * i:2 * i + 1]
        before = jnp.where(lane > 2 * i, 1.0, 0.0)
        cnt = cnt + jnp.where(vi > v, 1.0, 0.0) + jnp.where(vi == v, before, 0.0)
    return jnp.where(valid & (cnt < SEL_TOPK), 1.0, 0.0)


def _pad_rows(x, rows):
    return jnp.concatenate([x, jnp.zeros((rows - x.shape[0], x.shape[1]), x.dtype)], axis=0)


def _masked_softmax(s, mask):
    s = jnp.where(mask, s, -jnp.inf)
    m = jnp.max(s, axis=-1, keepdims=True)
    m = jnp.where(m > -jnp.inf, m, 0.0)
    p = jnp.exp(s - m)
    return p / jnp.maximum(jnp.sum(p, axis=-1, keepdims=True), 1e-30)


def _select_block_rows(pair, q_pos):
    row = lax.broadcasted_iota(jnp.int32, pair.shape, 0)
    cur = lax.shift_right_logical(q_pos, SEL_SHIFT)
    valid = row <= cur
    v = jnp.where(row == cur, jnp.inf, pair)
    v = jnp.where(valid, v, -jnp.inf)
    cnt = jnp.zeros(pair.shape, f32)
    for i in range(pair.shape[0]):
        vi = v[i:i + 1, :]
        before = jnp.where(row > i, 1.0, 0.0)
        cnt = cnt + jnp.where(vi > v, 1.0, 0.0) + jnp.where(vi == v, before, 0.0)
    return jnp.where(valid & (cnt < SEL_TOPK), 1.0, 0.0)


def _cmp_prompt_kernel(q_ref, blk_ref, oc_ref, sel_ref, pair_sc, flag_sc, *, n_blocks):
    tq = q_ref.shape[0]
    n_sel = n_blocks // 2
    t0 = pl.program_id(1) * tq
    row = lax.broadcasted_iota(jnp.int32, (LANES, tq), 0)
    q_pos = lax.broadcasted_iota(jnp.int32, (LANES, tq), 1) + t0
    vis = (row < n_blocks) & ((row + 1) * CMP_BLOCK - 1 <= q_pos)
    flag_sc[...] = jnp.zeros_like(flag_sc)

    def block_rows(c):
        return _pad_rows(blk_ref[pl.ds(c, n_blocks, stride=KV_ROWS), :], LANES).astype(bf16)

    for g in range(N_KV):
        kg = block_rows(g)
        vg = block_rows(N_KV + g)
        imp = jnp.zeros((LANES, tq), f32)
        for r in range(Q_PER_KV):
            hs = slice((g * Q_PER_KV + r) * HEAD_DIM, (g * Q_PER_KV + r + 1) * HEAD_DIM)
            s = jnp.where(vis, _mm_nt(kg, q_ref[:, hs].astype(bf16)) * ATT_SCALE, -jnp.inf)
            m = jnp.max(s, axis=0, keepdims=True)
            m = jnp.where(m > -jnp.inf, m, 0.0)
            p = jnp.exp(s - m)
            p = p / jnp.maximum(jnp.sum(p, axis=0, keepdims=True), 1e-30)
            oc_ref[:, hs] = _mm(p.T.astype(bf16), vg).astype(bf16)
            imp = imp + p
        pair = imp + pltpu.roll(imp, LANES - 1, axis=0)
        for h in range(tq // LANES):
            ts = slice(h * LANES, (h + 1) * LANES)
            pos = lax.broadcasted_iota(jnp.int32, (n_sel, LANES), 1) + (t0 + h * LANES)
            pair_sc[h] = pair[:, ts]
            flag_sc[:n_sel, :] = _select_block_rows(pair_sc[h, pl.ds(0, n_sel, stride=2), :], pos)
            sel_ref[ts, g * LANES:(g + 1) * LANES] = flag_sc[...].T.astype(bf16)


def cmp_prompt(proj, blocks, nb, t, tq):
    nq = t // tq
    n_blocks = t // CMP_BLOCK
    return pl.pallas_call(
        functools.partial(_cmp_prompt_kernel, n_blocks=n_blocks),
        grid=(nb, nq),
        in_specs=[pl.BlockSpec((tq, NSA_W), lambda b, i: (b * nq + i, 0)),
                  pl.BlockSpec((n_blocks * KV_ROWS, HEAD_DIM), lambda b, i: (b, 0))],
        out_specs=[pl.BlockSpec((tq, NSA_W), lambda b, i: (b * nq + i, 0)),
                   pl.BlockSpec((tq, N_KV * LANES), lambda b, i: (b * nq + i, 0))],
        out_shape=[jax.ShapeDtypeStruct((nb * t, NSA_W), bf16),
                   jax.ShapeDtypeStruct((nb * t, N_KV * LANES), bf16)],
        scratch_shapes=[pltpu.VMEM((tq // LANES, LANES, LANES), f32), pltpu.VMEM((LANES, LANES), f32)],
        compiler_params=_cparams(("parallel", "parallel")),
        name="cmp_prompt",
    )(proj, blocks)


def _attn_kernel(*refs, mode, tile, reach):
    if mode == "sel":
        q_ref, k_ref, v_ref, sel_ref, et_ref, o_ref, qp_sc, s_sc, mx_sc, mb_sc, acc_sc, rel_sc = refs
    else:
        q_ref, k_ref, v_ref, o_ref, qp_sc, s_sc, mx_sc, mb_sc, acc_sc, rel_sc = refs
    qi = pl.program_id(2)

    @pl.when(qi == 0)
    def _():
        rel_sc[...] = ((lax.broadcasted_iota(jnp.int32, rel_sc.shape, 0) & (tile - 1))
                       - lax.broadcasted_iota(jnp.int32, rel_sc.shape, 1))

    for r in range(Q_PER_KV):
        qh = q_ref[:, r * HEAD_DIM:(r + 1) * HEAD_DIM]
        if mode == "sel":
            bias = ((1.0 - sel_ref[...].astype(f32)) * MASK_BIAS).astype(bf16)
            qh = jnp.concatenate([qh, bias], axis=1)
        qp_sc[r * tile:(r + 1) * tile, :] = qh
    lo = 0 if mode == "sel" else jnp.maximum(qi - reach, 0)

    def scores(kj, n, masked):
        start = pl.multiple_of(kj * tile, tile)
        kt = k_ref[pl.ds(start, n * tile), :]
        if mode == "sel":
            kt = jnp.concatenate([kt, et_ref[pl.ds(start, n * tile), :]], axis=1)
        s = _mm_nt(qp_sc[...], kt)
        if not masked:
            return s
        dpos = rel_sc[:, :n * tile] + (qi - kj) * tile
        if mode == "win":
            ok = pltpu.bitcast(dpos, jnp.uint32) <= WINDOW
        else:
            ok = dpos >= 0
        return jnp.where(ok, s, MASK_BIAS)

    def store_scores(kj, n, s):
        mx = mx_sc[...]
        for i in range(n):
            s_sc[kj + i] = s[:, i * tile:(i + 1) * tile]
        for c in range(n * tile // LANES):
            mx = jnp.maximum(mx, s[:, c * LANES:(c + 1) * LANES])
        mx_sc[...] = mx

    def weigh(kj, n):
        start = pl.multiple_of(kj * tile, tile)
        mb = mb_sc[...]
        p = jnp.concatenate([jnp.exp2((s_sc[kj + i][:, c * LANES:(c + 1) * LANES] - mb) * (ATT_SCALE * LOG2E))
                             for i in range(n) for c in range(tile // LANES)], axis=1)
        vt = jnp.concatenate([v_ref[pl.ds(start, n * tile), :], jnp.ones((n * tile, HEAD_DIM), bf16)], axis=1)
        acc_sc[...] += _mm(p.astype(bf16), vt)

    def in_pairs(first, count, fn, group=2):
        def trip(i, carry):
            fn(first + group * i, group)
            return carry

        lax.fori_loop(0, count // group, trip, 0)
        done = count // group * group
        n = group // 2
        while n:
            @pl.when((count - done) & n != 0)
            def _(n=n):
                fn(first + done + ((count - done) & ~(2 * n - 1)), n)
            n //= 2

    mx_sc[...] = jnp.full_like(mx_sc, MASK_BIAS)
    if mode == "sel":
        in_pairs(lo, qi - lo, lambda kj, n: store_scores(kj, n, scores(kj, n, False)), group=4)
        store_scores(qi, 1, scores(qi, 1, True))
    else:
        in_pairs(lo, qi - lo + 1, lambda kj, n: store_scores(kj, n, scores(kj, n, True)))
    m = jnp.max(mx_sc[...], axis=1, keepdims=True)
    mb_sc[...] = jnp.broadcast_to(m, mb_sc.shape)
    acc_sc[...] = jnp.zeros_like(acc_sc)
    in_pairs(lo, qi - lo + 1, weigh, group=4 if mode == "sel" else 2)
    acc = acc_sc[...]
    o = (acc[:, :HEAD_DIM] / acc[:, HEAD_DIM:]).astype(bf16)
    for r in range(Q_PER_KV):
        o_ref[:, r * HEAD_DIM:(r + 1) * HEAD_DIM] = o[r * tile:(r + 1) * tile, :]


def _attn_step_kernel(pt_ref, *refs, counts, bodies):
    n_in = sum(c[0] for c in counts)
    n_out = sum(c[1] for c in counts)
    i, o, s = 0, n_in, n_in + n_out
    for k, (body, (ci, co, cs)) in enumerate(zip(bodies, counts)):
        part = (*refs[i:i + ci], *refs[o:o + co], *refs[s:s + cs])
        body(*part) if k == 0 else body(pt_ref, *part)
        i, o, s = i + ci, o + co, s + cs


def attn_prompt_with_step(q_rot, kv_b, sel, expand_t, nb, t, tile, mode, steps):
    nq = t // tile
    reach = -(-WINDOW // tile)
    qw = Q_PER_KV * HEAD_DIM
    kdim = 2 * HEAD_DIM if mode == "sel" else HEAD_DIM
    rows = Q_PER_KV * tile
    in_specs = [pl.BlockSpec((tile, qw), lambda b, g, i, pt: (b * nq + i, g)),
                pl.BlockSpec((t, HEAD_DIM), lambda b, g, i, pt: (b, g)),
                pl.BlockSpec((t, HEAD_DIM), lambda b, g, i, pt: (b, N_KV + g))]
    args = [q_rot, kv_b, kv_b]
    if mode == "sel":
        in_specs += [pl.BlockSpec((tile, LANES), lambda b, g, i, pt: (b * nq + i, g)),
                     pl.BlockSpec((t, LANES), lambda b, g, i, pt: (0, 0))]
        args += [sel, expand_t]
    scratch = [pltpu.VMEM((rows, kdim), bf16), pltpu.VMEM((nq, rows, tile), f32), pltpu.VMEM((rows, LANES), f32),
               pltpu.VMEM((rows, LANES), f32), pltpu.VMEM((rows, 2 * HEAD_DIM), f32),
               pltpu.VMEM((rows, 2 * tile), jnp.int32)]
    out_specs = [pl.BlockSpec((tile, qw), lambda b, g, i, pt: (b * nq + i, g))]
    out_shape = [jax.ShapeDtypeStruct((nb * t, NSA_W), bf16)]
    bodies = [functools.partial(_attn_kernel, mode=mode, tile=tile, reach=reach)]
    counts = [(len(in_specs), 1, len(scratch))]
    pt_flat = None
    for step in steps:
        body, pt_flat, s_args, s_in, s_out, s_shape, s_scratch = step(lambda b, g, i: (b * N_KV + g) * nq + i)
        assert s_shape[0].shape[0] == nb * N_KV * nq, "one sample per attention grid step"
        bodies.append(body)
        counts.append((len(s_in), len(s_out), len(s_scratch)))
        args, in_specs, out_specs = args + s_args, in_specs + s_in, out_specs + s_out
        out_shape, scratch = out_shape + s_shape, scratch + s_scratch
    return pl.pallas_call(
        functools.partial(_attn_step_kernel, counts=tuple(counts), bodies=tuple(bodies)),
        grid_spec=pltpu.PrefetchScalarGridSpec(
            num_scalar_prefetch=1,
            grid=(nb, N_KV, nq),
            in_specs=in_specs,
            out_specs=out_specs,
            scratch_shapes=scratch),
        out_shape=out_shape,
        compiler_params=_cparams(("parallel", "parallel", "arbitrary")),
        name="attn_" + mode + "_step",
    )(pt_flat, *args)


def _nsa_out_kernel(oc_ref, os_ref, ow_ref, g_ref, b_ref, z_ref, w_ref, x_ref, fg_ref, o_ref, a0_sc, a1_sc):
    i = pl.program_id(0)

    @pl.when(i == 0)
    def _():
        a1_sc[...] = jnp.zeros_like(a1_sc)

    def step(wr_sc, rd_sc):
        gate = _sigmoid(g_ref[...] + b_ref[...])
        for h in range(N_HEADS):
            hs = slice(h * HEAD_DIM, (h + 1) * HEAD_DIM)
            o = (gate[:, 3 * h:3 * h + 1] * oc_ref[:, hs].astype(f32)
                 + gate[:, 3 * h + 1:3 * h + 2] * os_ref[:, hs].astype(f32)
                 + gate[:, 3 * h + 2:3 * h + 3] * ow_ref[:, hs].astype(f32))
            wr_sc[:, hs] = (o * _silu(z_ref[:, hs])).astype(bf16)
        acc = x_ref[...] + _mm(rd_sc[...], w_ref[...])
        r = lax.rsqrt(jnp.mean(acc * acc, axis=-1, keepdims=True) + EPS)
        o_ref[...] = (acc * r) * fg_ref[...]

    @pl.when(i % 2 == 0)
    def _():
        step(a0_sc, a1_sc)

    @pl.when(i % 2 == 1)
    def _():
        step(a1_sc, a0_sc)


def nsa_out(o_c, o_s, o_w, gates, bias, proj, w_out, x, final_g, tm):
    m, d = x.shape
    n = m // tm
    ahead = lambda w, cb: pl.BlockSpec((tm, w), lambda i: (jnp.minimum(i, n - 1), cb))
    behind = pl.BlockSpec((tm, d), lambda i: (jnp.maximum(i - 1, 0), 0))
    return pl.pallas_call(
        _nsa_out_kernel,
        grid=(n + 1,),
        in_specs=[ahead(NSA_W, 0), ahead(NSA_W, 0), ahead(NSA_W, 0), ahead(LANES, 0),
                  pl.BlockSpec((1, LANES), lambda i: (0, 0)),
                  ahead(NSA_W, 1),
                  pl.BlockSpec(w_out.shape, lambda i: (0, 0)),
                  behind,
                  pl.BlockSpec((1, d), lambda i: (0, 0))],
        out_specs=behind,
        out_shape=jax.ShapeDtypeStruct((m, d), f32),
        scratch_shapes=[pltpu.VMEM((tm, NSA_W), bf16), pltpu.VMEM((tm, NSA_W), bf16)],
        compiler_params=_cparams(("arbitrary",)),
        name="nsa_out",
    )(o_c, o_s, o_w, gates, bias, proj, w_out, x, final_g.reshape(1, d))


def _head_group(shape):
    return lax.shift_right_logical(lax.broadcasted_iota(jnp.int32, shape, 0), GROUP_SHIFT)


def _cmp_step_kernel(pt_ref, q_ref, *refs, n_pages, q_pos):
    pages = refs[:n_pages]
    oc_ref, sel_ref, blk_sc = refs[n_pages:]
    per_page = PAGE_SIZE // CMP_BLOCK
    for p in range(n_pages):
        x = pages[p][0].reshape(per_page, CMP_BLOCK, KV_ROWS, HEAD_DIM)
        means = jnp.sum(x, axis=1) * (1.0 / CMP_BLOCK)
        blk_sc[p * per_page * KV_ROWS:(p + 1) * per_page * KV_ROWS, :] = means.reshape(per_page * KV_ROWS, HEAD_DIM)
    n_blocks = blk_sc.shape[0] // KV_ROWS
    qb = q_ref[0].astype(bf16)
    grp = _head_group((N_HEADS, LANES))
    lane = lax.broadcasted_iota(jnp.int32, (N_HEADS, LANES), 1)
    s = jnp.zeros((N_HEADS, LANES), f32)
    for g in range(N_KV):
        kg = _pad_rows(blk_sc[pl.ds(g, n_blocks, stride=KV_ROWS), :], LANES).astype(bf16)
        s = jnp.where(grp == g, _mm_nt(qb, kg), s)
    vis = (lane < n_blocks) & ((lane + 1) * CMP_BLOCK - 1 <= q_pos)
    p = _masked_softmax(s * ATT_SCALE, vis)
    pb = p.astype(bf16)
    o = jnp.zeros((N_HEADS, HEAD_DIM), f32)
    for g in range(N_KV):
        vg = _pad_rows(blk_sc[pl.ds(N_KV + g, n_blocks, stride=KV_ROWS), :], LANES).astype(bf16)
        o = jnp.where(grp == g, _mm(pb, vg), o)
    oc_ref[0] = o
    grow = lax.broadcasted_iota(jnp.int32, (SUBLANES, LANES), 0)
    imp = jnp.zeros((SUBLANES, LANES), f32)
    for g in range(N_KV):
        imp_g = jnp.sum(p[g * Q_PER_KV:(g + 1) * Q_PER_KV, :], axis=0, keepdims=True)
        imp = jnp.where(grow == g, imp_g, imp)
    sel_ref[0] = _select_blocks(imp, jnp.full((SUBLANES, LANES), q_pos, jnp.int32), q_pos // SEL_BLOCK + 1)


def _sample_specs(sample, n_pages):
    per = lambda *shape: pl.BlockSpec((1,) + shape, lambda *a: (sample(*a[:-1]), 0, 0))
    pages = [pl.BlockSpec((1, PAGE_SIZE * KV_ROWS, HEAD_DIM),
                          lambda *a, p=p: (a[-1][sample(*a[:-1]) * n_pages + p], 0, 0)) for p in range(n_pages)]
    return per, pages


def cmp_step(q3, cache, pt_flat, n_pages, q_pos):
    def build(sample):
        nb = q3.shape[0]
        per, pages = _sample_specs(sample, n_pages)
        return (functools.partial(_cmp_step_kernel, n_pages=n_pages, q_pos=q_pos), pt_flat,
                [q3] + [cache] * n_pages,
                [per(N_HEADS, HEAD_DIM)] + pages,
                [per(N_HEADS, HEAD_DIM), per(SUBLANES, LANES)],
                [jax.ShapeDtypeStruct((nb, N_HEADS, HEAD_DIM), f32),
                 jax.ShapeDtypeStruct((nb, SUBLANES, LANES), f32)],
                [pltpu.VMEM((n_pages * PAGE_SIZE // CMP_BLOCK * KV_ROWS, HEAD_DIM), f32)])
    return build


def _decode_attend(qb, n_tiles, kv_tile, flags, new_row, s_sc):
    grp = _head_group((N_HEADS, LANES))
    lane = lax.broadcasted_iota(jnp.int32, (N_HEADS, LANES), 1)
    for p in range(n_tiles):
        sp = jnp.zeros((N_HEADS, LANES), f32)
        for g in range(N_KV):
            sg = _mm_nt(qb, kv_tile(p, g).astype(bf16)) * ATT_SCALE
            if flags is not None:
                j0 = p * (LANES // SEL_BLOCK)
                f0 = flags[g:g + 1, 2 * j0:2 * j0 + 1]
                f1 = flags[g:g + 1, 2 * j0 + 2:2 * j0 + 3]
                sg = jnp.where(jnp.where(lane < SEL_BLOCK, f0, f1) > 0.5, sg, NEG)
            sp = jnp.where(grp == g, sg, sp)
        s_sc[:, p * LANES:(p + 1) * LANES] = sp
    qf = qb.astype(f32)
    s_new = jnp.zeros((N_HEADS, 1), f32)
    grp1 = _head_group((N_HEADS, 1))
    for g in range(N_KV):
        kn = new_row[:, g * HEAD_DIM:(g + 1) * HEAD_DIM].astype(bf16).astype(f32)
        s_new = jnp.where(grp1 == g, jnp.sum(qf * kn, axis=1, keepdims=True) * ATT_SCALE, s_new)
    s_all = s_sc[...]
    m = jnp.maximum(jnp.max(s_all, axis=1, keepdims=True), s_new)
    p_all = jnp.exp(s_all - m)
    p_new = jnp.exp(s_new - m)
    den = jnp.sum(p_all, axis=1, keepdims=True) + p_new
    o = jnp.zeros((N_HEADS, HEAD_DIM), f32)
    for g in range(N_KV):
        vn = new_row[:, KV_W // 2 + g * HEAD_DIM:KV_W // 2 + (g + 1) * HEAD_DIM].astype(bf16).astype(f32)
        o = jnp.where(grp == g, p_new * vn, o)
    for p in range(n_tiles):
        pb = p_all[:, p * LANES:(p + 1) * LANES].astype(bf16)
        for g in range(N_KV):
            o = o + jnp.where(grp == g, _mm(pb, kv_tile(p, N_KV + g).astype(bf16)), 0.0)
    return o / den


def _sel_step_kernel(pt_ref, q_ref, sel_ref, new_ref, *refs, n_pages):
    pages = refs[:n_pages]
    o_ref, s_sc = refs[n_pages:]
    kv_tile = lambda p, c: pages[p][0, pl.ds(c, PAGE_SIZE, stride=KV_ROWS), :]
    o_ref[0] = _decode_attend(q_ref[0], n_pages, kv_tile, sel_ref[0], new_ref[0], s_sc)


def sel_step(q3, sel, new_rows, cache, pt_flat, n_pages):
    def build(sample):
        nb = q3.shape[0]
        per, pages = _sample_specs(sample, n_pages)
        return (functools.partial(_sel_step_kernel, n_pages=n_pages), pt_flat,
                [q3, sel, new_rows] + [cache] * n_pages,
                [per(N_HEADS, HEAD_DIM), per(SUBLANES, LANES), per(1, KV_W)] + pages,
                [per(N_HEADS, HEAD_DIM)],
                [jax.ShapeDtypeStruct((nb, N_HEADS, HEAD_DIM), f32)],
                [pltpu.VMEM((N_HEADS, n_pages * PAGE_SIZE), f32)])
    return build


def _win_step_kernel(pt_ref, q_ref, new_ref, new8_ref, win_ref, o_ref, wo_ref, s_sc):
    wrows = win_ref.shape[1]
    n_tiles = wrows // (LANES * KV_ROWS)
    kv_tile = lambda p, c: win_ref[0, pl.ds(p * LANES * KV_ROWS + c, LANES, stride=KV_ROWS), :]
    o_ref[0] = _decode_attend(q_ref[0], n_tiles, kv_tile, None, new_ref[0], s_sc)
    wo_ref[0, :wrows - KV_ROWS, :] = win_ref[0, KV_ROWS:, :]
    wo_ref[0, wrows - KV_ROWS:, :] = new8_ref[0]


def win_step(q3, new_rows, win, pt_flat):
    def build(sample):
        nb, wrows = win.shape[0], win.shape[1]
        per, _ = _sample_specs(sample, 0)
        return (_win_step_kernel, pt_flat,
                [q3, new_rows, new_rows.reshape(nb, KV_ROWS, HEAD_DIM), win],
                [per(N_HEADS, HEAD_DIM), per(1, KV_W), per(KV_ROWS, HEAD_DIM), per(wrows, HEAD_DIM)],
                [per(N_HEADS, HEAD_DIM), per(wrows, HEAD_DIM)],
                [jax.ShapeDtypeStruct((nb, N_HEADS, HEAD_DIM), f32), jax.ShapeDtypeStruct(win.shape, f32)],
                [pltpu.VMEM((N_HEADS, wrows // KV_ROWS), f32)])
    return build


def _rope_tables(pos):
    half = HEAD_DIM // 2
    inv = ROPE_THETA ** (-jnp.arange(half, dtype=f32) / half)
    ang = pos.astype(f32)[:, None] * inv[None, :]
    cos, sin = jnp.cos(ang), jnp.sin(ang)
    return jnp.concatenate([cos, cos], axis=1), jnp.concatenate([-sin, sin], axis=1)


def _pad_cols(a, width):
    return jnp.pad(a, ((0, 0), (0, width - a.shape[1])))


def kernel(x_prompt, x_sample, state_pool, state_mlstm_c, state_mlstm_n, state_mlstm_m, cache_kv_cmp, cache_kv_sel, cache_kv_win, page_table, norm0_g, w_in0, b_gate0, w_pool, pool_scale, mh_norm_g, w_out0, norm1_g, w_in1, b_gate1, w_out1, final_g):
    nbp, t, d = x_prompt.shape
    nbs = x_sample.shape[0]
    mp = nbp * t
    n_pages = page_table.shape[1]
    past_len = n_pages * PAGE_SIZE
    wbuf = cache_kv_win.shape[1]

    w_in0 = w_in0.astype(bf16)
    w_in1 = w_in1.astype(bf16)
    w0 = ([w_in0], [(0, 0, MAIN_W)])
    wg0 = jnp.concatenate([_pad_cols(w_in0[:, MAIN_W:MAIN_W + M_HEADS], LANES),
                           _pad_cols(w_in0[:, MAIN_W + M_HEADS:], LANES)], axis=1)
    bias_i = _pad_cols(b_gate0[None, :M_HEADS], LANES)
    bias_f = _pad_cols(b_gate0[None, M_HEADS:], LANES)
    g_lo = NSA_W + 3 * KV_W
    g_hi = g_lo + 3 * N_HEADS
    w1 = ([w_in1, w_in1[:, g_hi:]], [(0, 0, NSA_W), (1, 0, NSA_W), (0, NSA_W, 3 * KV_W)])
    wg1 = _pad_cols(w_in1[:, g_lo:g_hi], LANES)
    bias1 = _pad_cols(b_gate1[None, :], LANES)
    w_pool_b = w_pool.astype(bf16)
    wo0_pool = w_out0[:POOL_W].astype(bf16)
    wo0_m = w_out0[POOL_W:].astype(bf16)
    wo1 = w_out1.astype(bf16)

    xp = x_prompt.reshape(mp, d)
    xs = x_sample.reshape(nbs, d)

    proj_p, gates_p = norm_proj(xp, norm0_g, w0, wg0, PROJ_TM, PROJ_TN)
    proj_s, gates_s = norm_proj(xs, norm0_g, w0, wg0, nbs, PROJ_TN)

    ypool_p = pool_prompt(proj_p, w_pool_b, pool_scale, nbp, t)
    ym_p, c_p, n_p, m_p = mlstm_prompt(proj_p, gates_p, bias_i, bias_f, mh_norm_g, nbp, t, MLSTM_CHUNK)
    xp1 = out_proj([ypool_p, ym_p], [wo0_pool, wo0_m], xp, OUT_TM)

    ypool_s = pool_step(state_pool.reshape(nbs, POOL_STATE * POOL_W), proj_s, w_pool_b, pool_scale)
    m0_pad = _pad_cols(state_mlstm_m, LANES)
    ym_s, c_s, n_s, m_s = mlstm_step(proj_s, gates_s, bias_i, bias_f, mh_norm_g,
                                     state_mlstm_c, state_mlstm_n, m0_pad, MLSTM_STEP_BB)
    xs1 = out_proj([ypool_s, ym_s], [wo0_pool, wo0_m], xs, nbs)

    pool_p = proj_p.reshape(nbp, t, MAIN_W)[:, t - POOL_STATE:, :POOL_W]
    pool_s = jnp.concatenate([state_pool[:, 1:], proj_s[:, None, :POOL_W]], axis=1)

    proj1_p, gates1_p = norm_proj(xp1, norm1_g, w1, wg1, PROJ_TM, PROJ_TN)
    proj1_s, gates1_s = norm_proj(xs1, norm1_g, w1, wg1, nbs, PROJ_TN)

    tq = ATTN_TILE
    cos_p, sin_p = _rope_tables(jnp.arange(t))
    qrot_p, kvc_p, kvs_p, kvw_p, blocks_p, kvs_pb, kvw_pb = nsa_prep(proj1_p, cos_p, sin_p, tq, t // tq, True)
    oc_p, sel_p = cmp_prompt(proj1_p, blocks_p, nbp, t, tq)
    expand_t = (jnp.arange(LANES)[None, :] == jnp.arange(t)[:, None] // SEL_BLOCK).astype(bf16)

    cos_s, sin_s = _rope_tables(jnp.full((nbs,), past_len))
    qrot_s, kvs_s, kvw_s = nsa_prep(proj1_s, cos_s, sin_s, nbs, 1, False)
    pt_flat = page_table.reshape(-1)
    q3_s = proj1_s[:, :NSA_W].reshape(nbs, N_HEADS, HEAD_DIM)
    qrot3_s = qrot_s.reshape(nbs, N_HEADS, HEAD_DIM)
    n_pool = cache_kv_cmp.shape[0]
    page_rows = PAGE_SIZE * KV_ROWS

    ow_p, oc_s, sel_s, ow_s, win_new = attn_prompt_with_step(
        qrot_p, kvw_pb, None, None, nbp, t, tq, "win",
        [cmp_step(q3_s, cache_kv_cmp.reshape(n_pool, page_rows, HEAD_DIM), pt_flat, n_pages, past_len),
         win_step(qrot3_s, kvw_s.reshape(nbs, 1, KV_W), cache_kv_win.reshape(nbs, wbuf * KV_ROWS, HEAD_DIM),
                  pt_flat)])
    os_p, os_s = attn_prompt_with_step(
        qrot_p, kvs_pb, sel_p, expand_t, nbp, t, tq, "sel",
        [sel_step(qrot3_s, sel_s, kvs_s.reshape(nbs, 1, KV_W),
                  cache_kv_sel.reshape(n_pool, page_rows, HEAD_DIM), pt_flat, n_pages)])
    y_p = nsa_out(oc_p, os_p, ow_p, gates1_p, bias1, proj1_p, wo1, xp1, final_g, NSA_OUT_TM)
    y_s = nsa_out(oc_s.reshape(nbs, NSA_W), os_s.reshape(nbs, NSA_W), ow_s.reshape(nbs, NSA_W),
                  gates1_s, bias1, proj1_s, wo1, xs1, final_g, nbs)

    kv5 = lambda a, rows: a.reshape(-1, rows, 2, N_KV, HEAD_DIM)
    return (y_p.reshape(nbp, t, d), y_s.reshape(nbs, 1, d),
            pool_p, pool_s,
            c_p, c_s, n_p, n_s, m_p[:, 0, :M_HEADS], m_s[:, :M_HEADS],
            kv5(kvc_p, t), kv5(proj1_s[:, 2 * NSA_W:2 * NSA_W + KV_W], 1),
            kv5(kvs_p, t), kv5(kvs_s, 1),
            kv5(kvw_p, t)[:, t - wbuf:],
            kv5(win_new, wbuf))
```

```python
import functools

import jax
import jax.numpy as jnp
from jax import lax
from jax.experimental import pallas as pl
from jax.experimental.pallas import tpu as pltpu

f32 = jnp.float32
bf16 = jnp.bfloat16

POOL_WINDOWS = (2, 4, 8, 16)
POOL_W = 1024
POOL_GROUP_W = 256
POOL_STATE = 15
M_HEADS = 4
M_W = 1024
M_HEAD_DIM = 256
N_HEADS = 16
HEAD_DIM = 128
N_KV = 4
Q_PER_KV = 4
NSA_W = 2048
KV_W = 1024
KV_ROWS = 2 * N_KV
CMP_BLOCK = 32
SEL_BLOCK = 64
SEL_TOPK = 16
WINDOW = 512
PAGE_SIZE = 128
ROPE_THETA = 10000.0
ATT_SCALE = HEAD_DIM ** -0.5
EPS = 1e-6
MAIN_W = 7168
LANES = 128
SUBLANES = 8
SEL_SHIFT = SEL_BLOCK.bit_length() - 1
GROUP_SHIFT = Q_PER_KV.bit_length() - 1
PROJ_TM, PROJ_TN = 1024, 1024
OUT_TM = 512
NSA_OUT_TM = 256
ATTN_TILE = 256
MLSTM_CHUNK = 256
MLSTM_STEP_BB = 8
NEG = -1e30
MASK_BIAS = -(2.0 ** 100)
LOG2E = 1.4426950408889634
VMEM_LIMIT = 48 * 1024 * 1024

_NT = (((1,), (1,)), ((), ()))


def _cparams(sem):
    return pltpu.CompilerParams(dimension_semantics=sem, vmem_limit_bytes=VMEM_LIMIT)


def _sigmoid(x):
    return 1.0 / (1.0 + jnp.exp(-x))


def _silu(x):
    return x * _sigmoid(x)


def _log_sigmoid(x):
    return jnp.minimum(x, 0.0) - jnp.log1p(jnp.exp(-jnp.abs(x)))


def _mm(a, b):
    return jnp.dot(a, b, preferred_element_type=f32)


def _mm_nt(a, b):
    return lax.dot_general(a, b, _NT, preferred_element_type=f32)


def _norm_proj_kernel(*refs, starts, seg_ref):
    n_w = max(seg_ref) + 1
    x_ref, g_ref = refs[:2]
    w_refs = refs[2:2 + n_w]
    wg_ref, o_ref, og_ref, h_ref = refs[2 + n_w:]
    j = pl.program_id(1)

    @pl.when(j == 0)
    def _():
        x = x_ref[...]
        r = lax.rsqrt(jnp.mean(x * x, axis=-1, keepdims=True) + EPS)
        h = ((x * r) * g_ref[...]).astype(bf16)
        h_ref[...] = h
        og_ref[...] = _mm(h, wg_ref[...])

    for k, a in enumerate(seg_ref):
        @pl.when((j >= starts[k]) & (j < starts[k + 1]))
        def _(w_ref=w_refs[a]):
            o_ref[...] = _mm(h_ref[...], w_ref[...])


def norm_proj(x, g, segs, wg, tm, tn):
    arrays, seg_list = segs
    m, d = x.shape
    ng = wg.shape[1]
    starts = [0]
    for _, _, cols in seg_list:
        starts.append(starts[-1] + cols // tn)
    n_tiles = starts[-1]

    def w_spec(a):
        def index(i, j):
            idx = None
            for k, (ak, col0, cols) in enumerate(seg_list):
                if ak != a:
                    continue
                here = col0 // tn + jnp.clip(j - starts[k], 0, cols // tn - 1)
                idx = here if idx is None else jnp.where(j >= starts[k], here, idx)
            return (0, idx)
        return pl.BlockSpec((d, tn), index)

    return pl.pallas_call(
        functools.partial(_norm_proj_kernel, starts=tuple(starts), seg_ref=tuple(a for a, _, _ in seg_list)),
        grid=(m // tm, n_tiles),
        in_specs=[pl.BlockSpec((tm, d), lambda i, j: (i, 0)),
                  pl.BlockSpec((1, d), lambda i, j: (0, 0))]
                 + [w_spec(a) for a in range(len(arrays))]
                 + [pl.BlockSpec((d, ng), lambda i, j: (0, 0))],
        out_specs=[pl.BlockSpec((tm, tn), lambda i, j: (i, j)),
                   pl.BlockSpec((tm, ng), lambda i, j: (i, 0))],
        out_shape=[jax.ShapeDtypeStruct((m, n_tiles * tn), f32), jax.ShapeDtypeStruct((m, ng), f32)],
        scratch_shapes=[pltpu.VMEM((tm, d), bf16)],
        compiler_params=_cparams(("parallel", "arbitrary")),
        name="norm_proj",
    )(x, g.reshape(1, d), *arrays, wg)


def _out_proj_kernel(*refs, n_parts):
    a_refs = refs[:n_parts]
    w_refs = refs[n_parts:2 * n_parts]
    x_ref, o_ref = refs[2 * n_parts:]
    acc = x_ref[...]
    for a_ref, w_ref in zip(a_refs, w_refs):
        acc = acc + _mm(a_ref[...], w_ref[...])
    o_ref[...] = acc


def out_proj(parts, weights, x, tm):
    m, d = x.shape
    n_parts = len(parts)
    in_specs = [pl.BlockSpec((tm, a.shape[1]), lambda i: (i, 0)) for a in parts]
    in_specs += [pl.BlockSpec(w.shape, lambda i: (0, 0)) for w in weights]
    in_specs += [pl.BlockSpec((tm, d), lambda i: (i, 0))]
    args = list(parts) + list(weights) + [x]
    return pl.pallas_call(
        functools.partial(_out_proj_kernel, n_parts=n_parts),
        grid=(m // tm,),
        in_specs=in_specs,
        out_specs=pl.BlockSpec((tm, d), lambda i: (i, 0)),
        out_shape=jax.ShapeDtypeStruct((m, d), f32),
        compiler_params=_cparams(("parallel",)),
        name="out_proj",
    )(*args)


def _pool_kernel(u_ref, z_ref, w_ref, sc_ref, o_ref):
    g = pl.program_id(1)
    x = u_ref[...]
    row = lax.broadcasted_iota(jnp.int32, x.shape, 0)

    def back(a, s):
        return jnp.where(row >= s, pltpu.roll(a, s, axis=0), 0.0)

    s2 = x + back(x, 1)
    s4 = s2 + back(s2, 2)
    s8 = s4 + back(s4, 4)
    s16 = s8 + back(s8, 8)
    win = jnp.where(g == 0, s2, jnp.where(g == 1, s4, jnp.where(g == 2, s8, s16)))
    wlen = lax.shift_left(jnp.int32(2), g)
    cnt = jnp.minimum(row + 1, wlen).astype(f32)
    pooled = win / cnt - x
    y = _mm(pooled.astype(bf16), w_ref[0]) * sc_ref[...]
    o_ref[...] = (y * _silu(z_ref[...])).astype(bf16)


def pool_prompt(proj, w_pool, pool_scale, nb, t):
    ng = len(POOL_WINDOWS)
    return pl.pallas_call(
        _pool_kernel,
        grid=(nb, ng),
        in_specs=[pl.BlockSpec((t, POOL_GROUP_W), lambda b, g: (b, g)),
                  pl.BlockSpec((t, POOL_GROUP_W), lambda b, g: (b, ng + g)),
                  pl.BlockSpec((1, POOL_GROUP_W, POOL_GROUP_W), lambda b, g: (g, 0, 0)),
                  pl.BlockSpec((1, POOL_GROUP_W), lambda b, g: (0, g))],
        out_specs=pl.BlockSpec((t, POOL_GROUP_W), lambda b, g: (b, g)),
        out_shape=jax.ShapeDtypeStruct((nb * t, POOL_W), bf16),
        compiler_params=_cparams(("parallel", "arbitrary")),
        name="pool_prompt",
    )(proj, proj, w_pool, pool_scale.reshape(1, POOL_W))


def _pool_step_kernel(st_ref, u_ref, z_ref, w_ref, sc_ref, o_ref):
    u = u_ref[...]
    for g, wlen in enumerate(POOL_WINDOWS):
        lo = g * POOL_GROUP_W
        ug = u[:, lo:lo + POOL_GROUP_W]
        acc = ug
        for r in range(POOL_STATE + 1 - wlen, POOL_STATE):
            acc = acc + st_ref[:, r * POOL_W + lo:r * POOL_W + lo + POOL_GROUP_W]
        pooled = acc / float(wlen) - ug
        y = _mm(pooled.astype(bf16), w_ref[g]) * sc_ref[:, lo:lo + POOL_GROUP_W]
        o_ref[:, lo:lo + POOL_GROUP_W] = (y * _silu(z_ref[:, lo:lo + POOL_GROUP_W])).astype(bf16)


def pool_step(state_flat, proj, w_pool, pool_scale):
    nb = proj.shape[0]
    return pl.pallas_call(
        _pool_step_kernel,
        grid=(1,),
        in_specs=[pl.BlockSpec(state_flat.shape, lambda i: (0, 0)),
                  pl.BlockSpec((nb, POOL_W), lambda i: (0, 0)),
                  pl.BlockSpec((nb, POOL_W), lambda i: (0, 1)),
                  pl.BlockSpec(w_pool.shape, lambda i: (0, 0, 0)),
                  pl.BlockSpec((1, POOL_W), lambda i: (0, 0))],
        out_specs=pl.BlockSpec((nb, POOL_W), lambda i: (0, 0)),
        out_shape=jax.ShapeDtypeStruct((nb, POOL_W), bf16),
        compiler_params=_cparams(("arbitrary",)),
        name="pool_step",
    )(state_flat, proj, proj, w_pool, pool_scale.reshape(1, POOL_W))


def _head_out(hc, o, z, g):
    hc = hc * _sigmoid(o)
    hc = hc * lax.rsqrt(jnp.mean(hc * hc, axis=-1, keepdims=True) + EPS)
    return ((hc * g) * _silu(z)).astype(bf16)


def _mlstm_kernel(q_ref, k_ref, v_ref, o_ref, z_ref, gi_ref, gf_ref, bi_ref, bf_ref, mhg_ref,
                  y_ref, c_ref, n_ref, m_ref):
    @pl.when(pl.program_id(1) == 0)
    def _():
        c_ref[...] = jnp.zeros_like(c_ref)
        n_ref[...] = jnp.zeros_like(n_ref)
        m_ref[...] = jnp.zeros_like(m_ref)

    ln = q_ref.shape[0]
    gi = gi_ref[...] + bi_ref[...]
    lf = _log_sigmoid(gf_ref[...] + bf_ref[...])
    row = lax.broadcasted_iota(jnp.int32, lf.shape, 0)
    b = lf
    s = 1
    while s < ln:
        b = b + jnp.where(row >= s, pltpu.roll(b, s, axis=0), 0.0)
        s *= 2
    r_t = (gi - b).T
    tt = lax.broadcasted_iota(jnp.int32, (ln, ln), 0)
    ss = lax.broadcasted_iota(jnp.int32, (ln, ln), 1)
    causal = ss <= tt
    lane = lax.broadcasted_iota(jnp.int32, (1, LANES), 1)
    m_vec = m_ref[0]
    for h in range(M_HEADS):
        hs = slice(h * M_HEAD_DIM, (h + 1) * M_HEAD_DIM)
        b_col = b[:, h:h + 1]
        ig_col = gi[:, h:h + 1]
        m_prev = m_vec[:, h:h + 1]
        inter = b_col + m_prev
        dmat = jnp.where(causal, b_col + r_t[h:h + 1, :], -jnp.inf)
        m_t = jnp.maximum(inter, jnp.max(dmat, axis=1, keepdims=True))
        dw = jnp.exp(dmat - m_t)
        iw = jnp.exp(inter - m_t)
        q = q_ref[:, hs]
        k = k_ref[:, hs] * (M_HEAD_DIM ** -0.5)
        v = v_ref[:, hs]
        qb, kb, vb = q.astype(bf16), k.astype(bf16), v.astype(bf16)
        c = c_ref[0, h]
        n = n_ref[0, h:h + 1, :]
        qk = _mm_nt(qb, kb) * dw
        num = iw * _mm(qb, c.astype(bf16)) + _mm(qk.astype(bf16), vb)
        den = iw * jnp.sum(q * n, axis=1, keepdims=True) + jnp.sum(qk, axis=1, keepdims=True)
        hc = num / jnp.maximum(jnp.abs(den), jnp.exp(-m_t))
        y_ref[:, hs] = _head_out(hc, o_ref[:, hs], z_ref[:, hs], mhg_ref[:, hs])
        m_last = m_t[ln - 1:ln, :]
        b_last = b_col[ln - 1:ln, :]
        ws = jnp.exp(b_last - b_col + ig_col - m_last)
        dec = jnp.exp(b_last + m_prev - m_last)
        kw = ws * k
        c_ref[0, h] = dec * c + _mm(kw.T.astype(bf16), vb)
        n_ref[0, h:h + 1, :] = dec * n + jnp.sum(kw, axis=0, keepdims=True)
        m_vec = jnp.where(lane == h, m_last, m_vec)
    m_ref[0] = m_vec


def mlstm_prompt(proj, gates, bias_i, bias_f, mh_norm_g, nb, t, ln):
    nc = t // ln
    col = lambda cb: pl.BlockSpec((ln, M_W), lambda b, c: (b * nc + c, cb))
    gcol = lambda cb: pl.BlockSpec((ln, LANES), lambda b, c: (b * nc + c, cb))
    vec = lambda w: pl.BlockSpec((1, w), lambda b, c: (0, 0))
    return pl.pallas_call(
        _mlstm_kernel,
        grid=(nb, nc),
        in_specs=[col(2), col(3), col(4), col(5), col(6), gcol(0), gcol(1), vec(LANES), vec(LANES), vec(M_W)],
        out_specs=[pl.BlockSpec((ln, M_W), lambda b, c: (b * nc + c, 0)),
                   pl.BlockSpec((1, M_HEADS, M_HEAD_DIM, M_HEAD_DIM), lambda b, c: (b, 0, 0, 0)),
                   pl.BlockSpec((1, M_HEADS, M_HEAD_DIM), lambda b, c: (b, 0, 0)),
                   pl.BlockSpec((1, 1, LANES), lambda b, c: (b, 0, 0))],
        out_shape=[jax.ShapeDtypeStruct((nb * t, M_W), bf16),
                   jax.ShapeDtypeStruct((nb, M_HEADS, M_HEAD_DIM, M_HEAD_DIM), f32),
                   jax.ShapeDtypeStruct((nb, M_HEADS, M_HEAD_DIM), f32),
                   jax.ShapeDtypeStruct((nb, 1, LANES), f32)],
        compiler_params=_cparams(("parallel", "arbitrary")),
        name="mlstm_prompt",
    )(proj, proj, proj, proj, proj, gates, gates, bias_i, bias_f, mh_norm_g.reshape(1, M_W))


def _mlstm_step_kernel(q_ref, k_ref, v_ref, o_ref, z_ref, gi_ref, gf_ref, bi_ref, bf_ref, mhg_ref,
                       c_ref, n_ref, m_ref, y_ref, co_ref, no_ref, mo_ref):
    nb = q_ref.shape[0]
    gi = gi_ref[...] + bi_ref[...]
    lf = _log_sigmoid(gf_ref[...] + bf_ref[...])
    inter = lf + m_ref[...]
    m_t = jnp.maximum(inter, gi)
    dw_all = jnp.exp(gi - m_t)
    iw_all = jnp.exp(inter - m_t)
    em_all = jnp.exp(-m_t)
    mo_ref[...] = m_t
    d0 = lax.broadcasted_iota(jnp.int32, (M_HEAD_DIM, M_HEAD_DIM), 0)
    d1 = lax.broadcasted_iota(jnp.int32, (M_HEAD_DIM, M_HEAD_DIM), 1)
    eye = d0 == d1
    for j in range(nb):
        for h in range(M_HEADS):
            hs = slice(h * M_HEAD_DIM, (h + 1) * M_HEAD_DIM)
            dw = dw_all[j:j + 1, h:h + 1]
            iw = iw_all[j:j + 1, h:h + 1]
            em = em_all[j:j + 1, h:h + 1]
            q = q_ref[j:j + 1, hs]
            k = k_ref[j:j + 1, hs] * (M_HEAD_DIM ** -0.5)
            v = v_ref[j:j + 1, hs]
            c = c_ref[j, h]
            n = n_ref[j, h:h + 1, :]
            qc = _mm(jnp.broadcast_to(q, (8, M_HEAD_DIM)).astype(bf16), c.astype(bf16))[0:1, :]
            qk = jnp.sum(q * k, axis=1, keepdims=True) * dw
            num = iw * qc + qk * v
            den = iw * jnp.sum(q * n, axis=1, keepdims=True) + qk
            hc = num / jnp.maximum(jnp.abs(den), em)
            y_ref[j:j + 1, hs] = _head_out(hc, o_ref[j:j + 1, hs], z_ref[j:j + 1, hs], mhg_ref[:, hs])
            kdiag = jnp.where(eye, jnp.broadcast_to(k, (M_HEAD_DIM, M_HEAD_DIM)), 0.0).astype(bf16)
            vrep = jnp.broadcast_to(v, (M_HEAD_DIM, M_HEAD_DIM)).astype(bf16)
            co_ref[j, h] = iw * c + dw * _mm(kdiag, vrep)
            no_ref[j, h:h + 1, :] = iw * n + dw * k


def mlstm_step(proj, gates, bias_i, bias_f, mh_norm_g, c0, n0, m0_pad, bb):
    nb = proj.shape[0]
    col = lambda cb: pl.BlockSpec((bb, M_W), lambda i: (i, cb))
    gcol = lambda cb: pl.BlockSpec((bb, LANES), lambda i: (i, cb))
    vec = lambda w: pl.BlockSpec((1, w), lambda i: (0, 0))
    cspec = pl.BlockSpec((bb, M_HEADS, M_HEAD_DIM, M_HEAD_DIM), lambda i: (i, 0, 0, 0))
    nspec = pl.BlockSpec((bb, M_HEADS, M_HEAD_DIM), lambda i: (i, 0, 0))
    return pl.pallas_call(
        _mlstm_step_kernel,
        grid=(nb // bb,),
        in_specs=[col(2), col(3), col(4), col(5), col(6), gcol(0), gcol(1), vec(LANES), vec(LANES), vec(M_W),
                  cspec, nspec, gcol(0)],
        out_specs=[pl.BlockSpec((bb, M_W), lambda i: (i, 0)), cspec, nspec, gcol(0)],
        out_shape=[jax.ShapeDtypeStruct((nb, M_W), bf16),
                   jax.ShapeDtypeStruct(c0.shape, f32),
                   jax.ShapeDtypeStruct(n0.shape, f32),
                   jax.ShapeDtypeStruct((nb, LANES), f32)],
        compiler_params=_cparams(("parallel",)),
        name="mlstm_step",
    )(proj, proj, proj, proj, proj, gates, gates, bias_i, bias_f, mh_norm_g.reshape(1, M_W), c0, n0, m0_pad)


def _rope(x, cos, sin_signed):
    return x * cos + pltpu.roll(x, HEAD_DIM // 2, axis=1) * sin_signed


def _nsa_prep_kernel(*refs, prompt):
    q_ref, kvc_ref, kvs_ref, kvw_ref, cos_ref, sin_ref, qr_ref = refs[:7]
    tq = q_ref.shape[0]
    cos, sin = cos_ref[...], sin_ref[...]
    for h in range(N_HEADS):
        hs = slice(h * HEAD_DIM, (h + 1) * HEAD_DIM)
        qr_ref[:, hs] = _rope(q_ref[:, hs], cos, sin).astype(bf16)

    def chunks(src, rotate):
        for c in range(KV_ROWS):
            x = src[:, c * HEAD_DIM:(c + 1) * HEAD_DIM]
            yield c, (_rope(x, cos, sin) if rotate and c < N_KV else x)

    if prompt:
        kvc_i, kvs_i, kvw_i, blk_ref, kvs_b, kvw_b = refs[7:13]
        for c, x in chunks(kvc_ref, False):
            kvc_i[pl.ds(c, tq, stride=KV_ROWS), :] = x
        for src, dst_i, dst_b in ((kvs_ref, kvs_i, kvs_b), (kvw_ref, kvw_i, kvw_b)):
            for c, x in chunks(src, True):
                dst_i[pl.ds(c, tq, stride=KV_ROWS), :] = x
                dst_b[:, c * HEAD_DIM:(c + 1) * HEAD_DIM] = x.astype(bf16)
        nblk = tq // CMP_BLOCK
        means = jnp.sum(kvc_ref[...].reshape(nblk, CMP_BLOCK, KV_W), axis=1) * (1.0 / CMP_BLOCK)
        for c in range(KV_ROWS):
            blk_ref[pl.ds(c, nblk, stride=KV_ROWS), :] = means[:, c * HEAD_DIM:(c + 1) * HEAD_DIM]
    else:
        for src, dst in ((kvs_ref, refs[7]), (kvw_ref, refs[8])):
            for c, x in chunks(src, True):
                dst[:, c * HEAD_DIM:(c + 1) * HEAD_DIM] = x


def nsa_prep(proj, cos, sin, tq, n_pos_blocks, prompt):
    m = proj.shape[0]
    row = lambda w, cb: pl.BlockSpec((tq, w), lambda i: (i, cb))
    tab = pl.BlockSpec((tq, HEAD_DIM), lambda i: (i % n_pos_blocks, 0))
    if prompt:
        inter = pl.BlockSpec((tq * KV_ROWS, HEAD_DIM), lambda i: (i, 0))
        inter_shape = jax.ShapeDtypeStruct((m * KV_ROWS, HEAD_DIM), f32)
        out_specs = [row(NSA_W, 0), inter, inter, inter,
                     pl.BlockSpec((tq // CMP_BLOCK * KV_ROWS, HEAD_DIM), lambda i: (i, 0)), row(KV_W, 0), row(KV_W, 0)]
        out_shape = [jax.ShapeDtypeStruct((m, NSA_W), bf16), inter_shape, inter_shape, inter_shape,
                     jax.ShapeDtypeStruct((m // CMP_BLOCK * KV_ROWS, HEAD_DIM), f32),
                     jax.ShapeDtypeStruct((m, KV_W), bf16), jax.ShapeDtypeStruct((m, KV_W), bf16)]
    else:
        out_specs = [row(NSA_W, 0), row(KV_W, 0), row(KV_W, 0)]
        out_shape = [jax.ShapeDtypeStruct((m, NSA_W), bf16), jax.ShapeDtypeStruct((m, KV_W), f32),
                     jax.ShapeDtypeStruct((m, KV_W), f32)]
    return pl.pallas_call(
        functools.partial(_nsa_prep_kernel, prompt=prompt),
        grid=(m // tq,),
        in_specs=[row(NSA_W, 0), row(KV_W, 4), row(KV_W, 5), row(KV_W, 6), tab, tab],
        out_specs=out_specs,
        out_shape=out_shape,
        compiler_params=_cparams(("parallel",)),
        name="nsa_prep",
    )(proj, proj, proj, proj, cos, sin)


def _select_blocks(imp, q_pos, n_cand):
    lane = lax.broadcasted_iota(jnp.int32, imp.shape, 1)
    pair = imp + pltpu.roll(imp, LANES - 1, axis=1)
    cur2 = lax.shift_left(lax.shift_right_logical(q_pos, SEL_SHIFT), 1)
    valid = ((lane & 1) == 0) & (lane <= cur2)
    v = jnp.where(lane == cur2, jnp.inf, pair)
    v = jnp.where(valid, v, -jnp.inf)
    cnt = jnp.zeros(imp.shape, f32)
    for i in range(n_cand):
        vi = v[:, 2 * i:2 * i + 1]
        before = jnp.where(lane > 2 * i, 1.0, 0.0)
        cnt = cnt + jnp.where(vi > v, 1.0, 0.0) + jnp.where(vi == v, before, 0.0)
    return jnp.where(valid & (cnt < SEL_TOPK), 1.0, 0.0)


def _pad_rows(x, rows):
    return jnp.concatenate([x, jnp.zeros((rows - x.shape[0], x.shape[1]), x.dtype)], axis=0)


def _masked_softmax(s, mask):
    s = jnp.where(mask, s, -jnp.inf)
    m = jnp.max(s, axis=-1, keepdims=True)
    m = jnp.where(m > -jnp.inf, m, 0.0)
    p = jnp.exp(s - m)
    return p / jnp.maximum(jnp.sum(p, axis=-1, keepdims=True), 1e-30)


def _select_block_rows(pair, q_pos):
    row = lax.broadcasted_iota(jnp.int32, pair.shape, 0)
    cur = lax.shift_right_logical(q_pos, SEL_SHIFT)
    valid = row <= cur
    v = jnp.where(row == cur, jnp.inf, pair)
    v = jnp.where(valid, v, -jnp.inf)
    cnt = jnp.zeros(pair.shape, f32)
    for i in range(pair.shape[0]):
        vi = v[i:i + 1, :]
        before = jnp.where(row > i, 1.0, 0.0)
        cnt = cnt + jnp.where(vi > v, 1.0, 0.0) + jnp.where(vi == v, before, 0.0)
    return jnp.where(valid & (cnt < SEL_TOPK), 1.0, 0.0)


def _cmp_prompt_kernel(q_ref, blk_ref, oc_ref, sel_ref, pair_sc, flag_sc, *, n_blocks):
    tq = q_ref.shape[0]
    n_sel = n_blocks // 2
    t0 = pl.program_id(1) * tq
    row = lax.broadcasted_iota(jnp.int32, (LANES, tq), 0)
    q_pos = lax.broadcasted_iota(jnp.int32, (LANES, tq), 1) + t0
    vis = (row < n_blocks) & ((row + 1) * CMP_BLOCK - 1 <= q_pos)
    flag_sc[...] = jnp.zeros_like(flag_sc)

    def block_rows(c):
        return _pad_rows(blk_ref[pl.ds(c, n_blocks, stride=KV_ROWS), :], LANES).astype(bf16)

    for g in range(N_KV):
        kg = block_rows(g)
        vg = block_rows(N_KV + g)
        imp = jnp.zeros((LANES, tq), f32)
        for r in range(Q_PER_KV):
            hs = slice((g * Q_PER_KV + r) * HEAD_DIM, (g * Q_PER_KV + r + 1) * HEAD_DIM)
            s = jnp.where(vis, _mm_nt(kg, q_ref[:, hs].astype(bf16)) * ATT_SCALE, -jnp.inf)
            m = jnp.max(s, axis=0, keepdims=True)
            m = jnp.where(m > -jnp.inf, m, 0.0)
            p = jnp.exp(s - m)
            p = p / jnp.maximum(jnp.sum(p, axis=0, keepdims=True), 1e-30)
            oc_ref[:, hs] = _mm(p.T.astype(bf16), vg).astype(bf16)
            imp = imp + p
        pair = imp + pltpu.roll(imp, LANES - 1, axis=0)
        for h in range(tq // LANES):
            ts = slice(h * LANES, (h + 1) * LANES)
            pos = lax.broadcasted_iota(jnp.int32, (n_sel, LANES), 1) + (t0 + h * LANES)
            pair_sc[h] = pair[:, ts]
            flag_sc[:n_sel, :] = _select_block_rows(pair_sc[h, pl.ds(0, n_sel, stride=2), :], pos)
            sel_ref[ts, g * LANES:(g + 1) * LANES] = flag_sc[...].T.astype(bf16)


def cmp_prompt(proj, blocks, nb, t, tq):
    nq = t // tq
    n_blocks = t // CMP_BLOCK
    return pl.pallas_call(
        functools.partial(_cmp_prompt_kernel, n_blocks=n_blocks),
        grid=(nb, nq),
        in_specs=[pl.BlockSpec((tq, NSA_W), lambda b, i: (b * nq + i, 0)),
                  pl.BlockSpec((n_blocks * KV_ROWS, HEAD_DIM), lambda b, i: (b, 0))],
        out_specs=[pl.BlockSpec((tq, NSA_W), lambda b, i: (b * nq + i, 0)),
                   pl.BlockSpec((tq, N_KV * LANES), lambda b, i: (b * nq + i, 0))],
        out_shape=[jax.ShapeDtypeStruct((nb * t, NSA_W), bf16),
                   jax.ShapeDtypeStruct((nb * t, N_KV * LANES), bf16)],
        scratch_shapes=[pltpu.VMEM((tq // LANES, LANES, LANES), f32), pltpu.VMEM((LANES, LANES), f32)],
        compiler_params=_cparams(("parallel", "parallel")),
        name="cmp_prompt",
    )(proj, blocks)


def _attn_kernel(*refs, mode, tile, reach):
    if mode == "sel":
        q_ref, k_ref, v_ref, sel_ref, et_ref, o_ref, qp_sc, s_sc, mx_sc, mb_sc, acc_sc, rel_sc = refs
    else:
        q_ref, k_ref, v_ref, o_ref, qp_sc, s_sc, mx_sc, mb_sc, acc_sc, rel_sc = refs
    qi = pl.program_id(2)

    @pl.when(qi == 0)
    def _():
        rel_sc[...] = ((lax.broadcasted_iota(jnp.int32, rel_sc.shape, 0) & (tile - 1))
                       - lax.broadcasted_iota(jnp.int32, rel_sc.shape, 1))

    for r in range(Q_PER_KV):
        qh = q_ref[:, r * HEAD_DIM:(r + 1) * HEAD_DIM]
        if mode == "sel":
            bias = ((1.0 - sel_ref[...].astype(f32)) * MASK_BIAS).astype(bf16)
            qh = jnp.concatenate([qh, bias], axis=1)
        qp_sc[r * tile:(r + 1) * tile, :] = qh
    lo = 0 if mode == "sel" else jnp.maximum(qi - reach, 0)

    def scores(kj, n, masked):
        start = pl.multiple_of(kj * tile, tile)
        kt = k_ref[pl.ds(start, n * tile), :]
        if mode == "sel":
            kt = jnp.concatenate([kt, et_ref[pl.ds(start, n * tile), :]], axis=1)
        s = _mm_nt(qp_sc[...], kt)
        if not masked:
            return s
        dpos = rel_sc[:, :n * tile] + (qi - kj) * tile
        if mode == "win":
            ok = pltpu.bitcast(dpos, jnp.uint32) <= WINDOW
        else:
            ok = dpos >= 0
        return jnp.where(ok, s, MASK_BIAS)

    def store_scores(kj, n, s):
        mx = mx_sc[...]
        for i in range(n):
            s_sc[kj + i] = s[:, i * tile:(i + 1) * tile]
        for c in range(n * tile // LANES):
            mx = jnp.maximum(mx, s[:, c * LANES:(c + 1) * LANES])
        mx_sc[...] = mx

    def weigh(kj, n):
        start = pl.multiple_of(kj * tile, tile)
        mb = mb_sc[...]
        p = jnp.concatenate([jnp.exp2((s_sc[kj + i][:, c * LANES:(c + 1) * LANES] - mb) * (ATT_SCALE * LOG2E))
                             for i in range(n) for c in range(tile // LANES)], axis=1)
        vt = jnp.concatenate([v_ref[pl.ds(start, n * tile), :], jnp.ones((n * tile, HEAD_DIM), bf16)], axis=1)
        acc_sc[...] += _mm(p.astype(bf16), vt)

    def in_pairs(first, count, fn, group=2):
        def trip(i, carry):
            fn(first + group * i, group)
            return carry

        lax.fori_loop(0, count // group, trip, 0)
        done = count // group * group
        n = group // 2
        while n:
            @pl.when((count - done) & n != 0)
            def _(n=n):
                fn(first + done + ((count - done) & ~(2 * n - 1)), n)
            n //= 2

    def whole_window(fn):
        for n in range(1, reach + 2):
            pl.when(qi - lo + 1 == n)(functools.partial(fn, lo, n))

    mx_sc[...] = jnp.full_like(mx_sc, MASK_BIAS)
    if mode == "sel":
        in_pairs(lo, qi - lo, lambda kj, n: store_scores(kj, n, scores(kj, n, False)), group=4)
        store_scores(qi, 1, scores(qi, 1, True))
    else:
        whole_window(lambda kj, n: store_scores(kj, n, scores(kj, n, True)))
    m = jnp.max(mx_sc[...], axis=1, keepdims=True)
    mb_sc[...] = jnp.broadcast_to(m, mb_sc.shape)
    acc_sc[...] = jnp.zeros_like(acc_sc)
    if mode == "sel":
        in_pairs(lo, qi - lo + 1, weigh, group=4)
    else:
        whole_window(weigh)
    acc = acc_sc[...]
    o = (acc[:, :HEAD_DIM] / acc[:, HEAD_DIM:]).astype(bf16)
    for r in range(Q_PER_KV):
        o_ref[:, r * HEAD_DIM:(r + 1) * HEAD_DIM] = o[r * tile:(r + 1) * tile, :]


def _attn_step_kernel(pt_ref, *refs, counts, bodies):
    n_in = sum(c[0] for c in counts)
    n_out = sum(c[1] for c in counts)
    i, o, s = 0, n_in, n_in + n_out
    for k, (body, (ci, co, cs)) in enumerate(zip(bodies, counts)):
        part = (*refs[i:i + ci], *refs[o:o + co], *refs[s:s + cs])
        body(*part) if k == 0 else body(pt_ref, *part)
        i, o, s = i + ci, o + co, s + cs


def attn_prompt_with_step(q_rot, kv_b, sel, expand_t, nb, t, tile, mode, steps):
    nq = t // tile
    reach = -(-WINDOW // tile)
    qw = Q_PER_KV * HEAD_DIM
    kdim = 2 * HEAD_DIM if mode == "sel" else HEAD_DIM
    rows = Q_PER_KV * tile
    in_specs = [pl.BlockSpec((tile, qw), lambda b, g, i, pt: (b * nq + i, g)),
                pl.BlockSpec((t, HEAD_DIM), lambda b, g, i, pt: (b, g)),
                pl.BlockSpec((t, HEAD_DIM), lambda b, g, i, pt: (b, N_KV + g))]
    args = [q_rot, kv_b, kv_b]
    if mode == "sel":
        in_specs += [pl.BlockSpec((tile, LANES), lambda b, g, i, pt: (b * nq + i, g)),
                     pl.BlockSpec((t, LANES), lambda b, g, i, pt: (0, 0))]
        args += [sel, expand_t]
    scratch = [pltpu.VMEM((rows, kdim), bf16), pltpu.VMEM((nq, rows, tile), f32), pltpu.VMEM((rows, LANES), f32),
               pltpu.VMEM((rows, LANES), f32), pltpu.VMEM((rows, 2 * HEAD_DIM), f32),
               pltpu.VMEM((rows, (reach + 1 if mode == "win" else 1) * tile), jnp.int32)]
    out_specs = [pl.BlockSpec((tile, qw), lambda b, g, i, pt: (b * nq + i, g))]
    out_shape = [jax.ShapeDtypeStruct((nb * t, NSA_W), bf16)]
    bodies = [functools.partial(_attn_kernel, mode=mode, tile=tile, reach=reach)]
    counts = [(len(in_specs), 1, len(scratch))]
    pt_flat = None
    for step in steps:
        body, pt_flat, s_args, s_in, s_out, s_shape, s_scratch = step(lambda b, g, i: (b * N_KV + g) * nq + i)
        assert s_shape[0].shape[0] == nb * N_KV * nq, "one sample per attention grid step"
        bodies.append(body)
        counts.append((len(s_in), len(s_out), len(s_scratch)))
        args, in_specs, out_specs = args + s_args, in_specs + s_in, out_specs + s_out
        out_shape, scratch = out_shape + s_shape, scratch + s_scratch
    return pl.pallas_call(
        functools.partial(_attn_step_kernel, counts=tuple(counts), bodies=tuple(bodies)),
        grid_spec=pltpu.PrefetchScalarGridSpec(
            num_scalar_prefetch=1,
            grid=(nb, N_KV, nq),
            in_specs=in_specs,
            out_specs=out_specs,
            scratch_shapes=scratch),
        out_shape=out_shape,
        compiler_params=_cparams(("parallel", "parallel", "arbitrary")),
        name="attn_" + mode + "_step",
    )(pt_flat, *args)


def _nsa_out_kernel(oc_ref, os_ref, ow_ref, g_ref, b_ref, z_ref, w_ref, x_ref, fg_ref, o_ref, a0_sc, a1_sc):
    i = pl.program_id(0)

    @pl.when(i == 0)
    def _():
        a1_sc[...] = jnp.zeros_like(a1_sc)

    def step(wr_sc, rd_sc):
        gate = _sigmoid(g_ref[...] + b_ref[...])
        for h in range(N_HEADS):
            hs = slice(h * HEAD_DIM, (h + 1) * HEAD_DIM)
            o = (gate[:, 3 * h:3 * h + 1] * oc_ref[:, hs].astype(f32)
                 + gate[:, 3 * h + 1:3 * h + 2] * os_ref[:, hs].astype(f32)
                 + gate[:, 3 * h + 2:3 * h + 3] * ow_ref[:, hs].astype(f32))
            wr_sc[:, hs] = (o * _silu(z_ref[:, hs])).astype(bf16)
        acc = x_ref[...] + _mm(rd_sc[...], w_ref[...])
        r = lax.rsqrt(jnp.mean(acc * acc, axis=-1, keepdims=True) + EPS)
        o_ref[...] = (acc * r) * fg_ref[...]

    @pl.when(i % 2 == 0)
    def _():
        step(a0_sc, a1_sc)

    @pl.when(i % 2 == 1)
    def _():
        step(a1_sc, a0_sc)


def nsa_out(o_c, o_s, o_w, gates, bias, proj, w_out, x, final_g, tm):
    m, d = x.shape
    n = m // tm
    ahead = lambda w, cb: pl.BlockSpec((tm, w), lambda i: (jnp.minimum(i, n - 1), cb))
    behind = pl.BlockSpec((tm, d), lambda i: (jnp.maximum(i - 1, 0), 0))
    return pl.pallas_call(
        _nsa_out_kernel,
        grid=(n + 1,),
        in_specs=[ahead(NSA_W, 0), ahead(NSA_W, 0), ahead(NSA_W, 0), ahead(LANES, 0),
                  pl.BlockSpec((1, LANES), lambda i: (0, 0)),
                  ahead(NSA_W, 1),
                  pl.BlockSpec(w_out.shape, lambda i: (0, 0)),
                  behind,
                  pl.BlockSpec((1, d), lambda i: (0, 0))],
        out_specs=behind,
        out_shape=jax.ShapeDtypeStruct((m, d), f32),
        scratch_shapes=[pltpu.VMEM((tm, NSA_W), bf16), pltpu.VMEM((tm, NSA_W), bf16)],
        compiler_params=_cparams(("arbitrary",)),
        name="nsa_out",
    )(o_c, o_s, o_w, gates, bias, proj, w_out, x, final_g.reshape(1, d))


def _head_group(shape):
    return lax.shift_right_logical(lax.broadcasted_iota(jnp.int32, shape, 0), GROUP_SHIFT)


def _cmp_step_kernel(pt_ref, q_ref, *refs, n_pages, q_pos):
    pages = refs[:n_pages]
    oc_ref, sel_ref, blk_sc = refs[n_pages:]
    per_page = PAGE_SIZE // CMP_BLOCK
    for p in range(n_pages):
        x = pages[p][0].reshape(per_page, CMP_BLOCK, KV_ROWS, HEAD_DIM)
        means = jnp.sum(x, axis=1) * (1.0 / CMP_BLOCK)
        blk_sc[p * per_page * KV_ROWS:(p + 1) * per_page * KV_ROWS, :] = means.reshape(per_page * KV_ROWS, HEAD_DIM)
    n_blocks = blk_sc.shape[0] // KV_ROWS
    qb = q_ref[0].astype(bf16)
    grp = _head_group((N_HEADS, LANES))
    lane = lax.broadcasted_iota(jnp.int32, (N_HEADS, LANES), 1)
    s = jnp.zeros((N_HEADS, LANES), f32)
    for g in range(N_KV):
        kg = _pad_rows(blk_sc[pl.ds(g, n_blocks, stride=KV_ROWS), :], LANES).astype(bf16)
        s = jnp.where(grp == g, _mm_nt(qb, kg), s)
    vis = (lane < n_blocks) & ((lane + 1) * CMP_BLOCK - 1 <= q_pos)
    p = _masked_softmax(s * ATT_SCALE, vis)
    pb = p.astype(bf16)
    o = jnp.zeros((N_HEADS, HEAD_DIM), f32)
    for g in range(N_KV):
        vg = _pad_rows(blk_sc[pl.ds(N_KV + g, n_blocks, stride=KV_ROWS), :], LANES).astype(bf16)
        o = jnp.where(grp == g, _mm(pb, vg), o)
    oc_ref[0] = o
    grow = lax.broadcasted_iota(jnp.int32, (SUBLANES, LANES), 0)
    imp = jnp.zeros((SUBLANES, LANES), f32)
    for g in range(N_KV):
        imp_g = jnp.sum(p[g * Q_PER_KV:(g + 1) * Q_PER_KV, :], axis=0, keepdims=True)
        imp = jnp.where(grow == g, imp_g, imp)
    sel_ref[0] = _select_blocks(imp, jnp.full((SUBLANES, LANES), q_pos, jnp.int32), q_pos // SEL_BLOCK + 1)


def _sample_specs(sample, n_pages):
    per = lambda *shape: pl.BlockSpec((1,) + shape, lambda *a: (sample(*a[:-1]), 0, 0))
    pages = [pl.BlockSpec((1, PAGE_SIZE * KV_ROWS, HEAD_DIM),
                          lambda *a, p=p: (a[-1][sample(*a[:-1]) * n_pages + p], 0, 0)) for p in range(n_pages)]
    return per, pages


def cmp_step(q3, cache, pt_flat, n_pages, q_pos):
    def build(sample):
        nb = q3.shape[0]
        per, pages = _sample_specs(sample, n_pages)
        return (functools.partial(_cmp_step_kernel, n_pages=n_pages, q_pos=q_pos), pt_flat,
                [q3] + [cache] * n_pages,
                [per(N_HEADS, HEAD_DIM)] + pages,
                [per(N_HEADS, HEAD_DIM), per(SUBLANES, LANES)],
                [jax.ShapeDtypeStruct((nb, N_HEADS, HEAD_DIM), f32),
                 jax.ShapeDtypeStruct((nb, SUBLANES, LANES), f32)],
                [pltpu.VMEM((n_pages * PAGE_SIZE // CMP_BLOCK * KV_ROWS, HEAD_DIM), f32)])
    return build


def _decode_attend(qb, n_tiles, kv_tile, flags, new_row, s_sc):
    grp = _head_group((N_HEADS, LANES))
    lane = lax.broadcasted_iota(jnp.int32, (N_HEADS, LANES), 1)
    for p in range(n_tiles):
        sp = jnp.zeros((N_HEADS, LANES), f32)
        for g in range(N_KV):
            sg = _mm_nt(qb, kv_tile(p, g).astype(bf16)) * ATT_SCALE
            if flags is not None:
                j0 = p * (LANES // SEL_BLOCK)
                f0 = flags[g:g + 1, 2 * j0:2 * j0 + 1]
                f1 = flags[g:g + 1, 2 * j0 + 2:2 * j0 + 3]
                sg = jnp.where(jnp.where(lane < SEL_BLOCK, f0, f1) > 0.5, sg, NEG)
            sp = jnp.where(grp == g, sg, sp)
        s_sc[:, p * LANES:(p + 1) * LANES] = sp
    qf = qb.astype(f32)
    s_new = jnp.zeros((N_HEADS, 1), f32)
    grp1 = _head_group((N_HEADS, 1))
    for g in range(N_KV):
        kn = new_row[:, g * HEAD_DIM:(g + 1) * HEAD_DIM].astype(bf16).astype(f32)
        s_new = jnp.where(grp1 == g, jnp.sum(qf * kn, axis=1, keepdims=True) * ATT_SCALE, s_new)
    s_all = s_sc[...]
    m = jnp.maximum(jnp.max(s_all, axis=1, keepdims=True), s_new)
    p_all = jnp.exp(s_all - m)
    p_new = jnp.exp(s_new - m)
    den = jnp.sum(p_all, axis=1, keepdims=True) + p_new
    o = jnp.zeros((N_HEADS, HEAD_DIM), f32)
    for g in range(N_KV):
        vn = new_row[:, KV_W // 2 + g * HEAD_DIM:KV_W // 2 + (g + 1) * HEAD_DIM].astype(bf16).astype(f32)
        o = jnp.where(grp == g, p_new * vn, o)
    for p in range(n_tiles):
        pb = p_all[:, p * LANES:(p + 1) * LANES].astype(bf16)
        for g in range(N_KV):
            o = o + jnp.where(grp == g, _mm(pb, kv_tile(p, N_KV + g).astype(bf16)), 0.0)
    return o / den


def _sel_step_kernel(pt_ref, q_ref, sel_ref, new_ref, *refs, n_pages):
    pages = refs[:n_pages]
    o_ref, s_sc = refs[n_pages:]
    kv_tile = lambda p, c: pages[p][0, pl.ds(c, PAGE_SIZE, stride=KV_ROWS), :]
    o_ref[0] = _decode_attend(q_ref[0], n_pages, kv_tile, sel_ref[0], new_ref[0], s_sc)


def sel_step(q3, sel, new_rows, cache, pt_flat, n_pages):
    def build(sample):
        nb = q3.shape[0]
        per, pages = _sample_specs(sample, n_pages)
        return (functools.partial(_sel_step_kernel, n_pages=n_pages), pt_flat,
                [q3, sel, new_rows] + [cache] * n_pages,
                [per(N_HEADS, HEAD_DIM), per(SUBLANES, LANES), per(1, KV_W)] + pages,
                [per(N_HEADS, HEAD_DIM)],
                [jax.ShapeDtypeStruct((nb, N_HEADS, HEAD_DIM), f32)],
                [pltpu.VMEM((N_HEADS, n_pages * PAGE_SIZE), f32)])
    return build


def _win_step_kernel(pt_ref, q_ref, new_ref, new8_ref, win_ref, o_ref, wo_ref, s_sc):
    wrows = win_ref.shape[1]
    n_tiles = wrows // (LANES * KV_ROWS)
    kv_tile = lambda p, c: win_ref[0, pl.ds(p * LANES * KV_ROWS + c, LANES, stride=KV_ROWS), :]
    o_ref[0] = _decode_attend(q_ref[0], n_tiles, kv_tile, None, new_ref[0], s_sc)
    wo_ref[0, :wrows - KV_ROWS, :] = win_ref[0, KV_ROWS:, :]
    wo_ref[0, wrows - KV_ROWS:, :] = new8_ref[0]


def win_step(q3, new_rows, win, pt_flat):
    def build(sample):
        nb, wrows = win.shape[0], win.shape[1]
        per, _ = _sample_specs(sample, 0)
        return (_win_step_kernel, pt_flat,
                [q3, new_rows, new_rows.reshape(nb, KV_ROWS, HEAD_DIM), win],
                [per(N_HEADS, HEAD_DIM), per(1, KV_W), per(KV_ROWS, HEAD_DIM), per(wrows, HEAD_DIM)],
                [per(N_HEADS, HEAD_DIM), per(wrows, HEAD_DIM)],
                [jax.ShapeDtypeStruct((nb, N_HEADS, HEAD_DIM), f32), jax.ShapeDtypeStruct(win.shape, f32)],
                [pltpu.VMEM((N_HEADS, wrows // KV_ROWS), f32)])
    return build


def _rope_tables(pos):
    half = HEAD_DIM // 2
    inv = ROPE_THETA ** (-jnp.arange(half, dtype=f32) / half)
    ang = pos.astype(f32)[:, None] * inv[None, :]
    cos, sin = jnp.cos(ang), jnp.sin(ang)
    return jnp.concatenate([cos, cos], axis=1), jnp.concatenate([-sin, sin], axis=1)


def _pad_cols(a, width):
    return jnp.pad(a, ((0, 0), (0, width - a.shape[1])))


def kernel(x_prompt, x_sample, state_pool, state_mlstm_c, state_mlstm_n, state_mlstm_m, cache_kv_cmp, cache_kv_sel, cache_kv_win, page_table, norm0_g, w_in0, b_gate0, w_pool, pool_scale, mh_norm_g, w_out0, norm1_g, w_in1, b_gate1, w_out1, final_g):
    nbp, t, d = x_prompt.shape
    nbs = x_sample.shape[0]
    mp = nbp * t
    n_pages = page_table.shape[1]
    past_len = n_pages * PAGE_SIZE
    wbuf = cache_kv_win.shape[1]

    w_in0 = w_in0.astype(bf16)
    w_in1 = w_in1.astype(bf16)
    w0 = ([w_in0], [(0, 0, MAIN_W)])
    wg0 = jnp.concatenate([_pad_cols(w_in0[:, MAIN_W:MAIN_W + M_HEADS], LANES),
                           _pad_cols(w_in0[:, MAIN_W + M_HEADS:], LANES)], axis=1)
    bias_i = _pad_cols(b_gate0[None, :M_HEADS], LANES)
    bias_f = _pad_cols(b_gate0[None, M_HEADS:], LANES)
    g_lo = NSA_W + 3 * KV_W
    g_hi = g_lo + 3 * N_HEADS
    w1 = ([w_in1, w_in1[:, g_hi:]], [(0, 0, NSA_W), (1, 0, NSA_W), (0, NSA_W, 3 * KV_W)])
    wg1 = _pad_cols(w_in1[:, g_lo:g_hi], LANES)
    bias1 = _pad_cols(b_gate1[None, :], LANES)
    w_pool_b = w_pool.astype(bf16)
    wo0_pool = w_out0[:POOL_W].astype(bf16)
    wo0_m = w_out0[POOL_W:].astype(bf16)
    wo1 = w_out1.astype(bf16)

    xp = x_prompt.reshape(mp, d)
    xs = x_sample.reshape(nbs, d)

    proj_p, gates_p = norm_proj(xp, norm0_g, w0, wg0, PROJ_TM, PROJ_TN)
    proj_s, gates_s = norm_proj(xs, norm0_g, w0, wg0, nbs, PROJ_TN)

    ypool_p = pool_prompt(proj_p, w_pool_b, pool_scale, nbp, t)
    ym_p, c_p, n_p, m_p = mlstm_prompt(proj_p, gates_p, bias_i, bias_f, mh_norm_g, nbp, t, MLSTM_CHUNK)
    xp1 = out_proj([ypool_p, ym_p], [wo0_pool, wo0_m], xp, OUT_TM)

    ypool_s = pool_step(state_pool.reshape(nbs, POOL_STATE * POOL_W), proj_s, w_pool_b, pool_scale)
    m0_pad = _pad_cols(state_mlstm_m, LANES)
    ym_s, c_s, n_s, m_s = mlstm_step(proj_s, gates_s, bias_i, bias_f, mh_norm_g,
                                     state_mlstm_c, state_mlstm_n, m0_pad, MLSTM_STEP_BB)
    xs1 = out_proj([ypool_s, ym_s], [wo0_pool, wo0_m], xs, nbs)

    pool_p = proj_p.reshape(nbp, t, MAIN_W)[:, t - POOL_STATE:, :POOL_W]
    pool_s = jnp.concatenate([state_pool[:, 1:], proj_s[:, None, :POOL_W]], axis=1)

    proj1_p, gates1_p = norm_proj(xp1, norm1_g, w1, wg1, PROJ_TM, PROJ_TN)
    proj1_s, gates1_s = norm_proj(xs1, norm1_g, w1, wg1, nbs, PROJ_TN)

    tq = ATTN_TILE
    cos_p, sin_p = _rope_tables(jnp.arange(t))
    qrot_p, kvc_p, kvs_p, kvw_p, blocks_p, kvs_pb, kvw_pb = nsa_prep(proj1_p, cos_p, sin_p, tq, t // tq, True)
    oc_p, sel_p = cmp_prompt(proj1_p, blocks_p, nbp, t, tq)
    expand_t = (jnp.arange(LANES)[None, :] == jnp.arange(t)[:, None] // SEL_BLOCK).astype(bf16)

    cos_s, sin_s = _rope_tables(jnp.full((nbs,), past_len))
    qrot_s, kvs_s, kvw_s = nsa_prep(proj1_s, cos_s, sin_s, nbs, 1, False)
    pt_flat = page_table.reshape(-1)
    q3_s = proj1_s[:, :NSA_W].reshape(nbs, N_HEADS, HEAD_DIM)
    qrot3_s = qrot_s.reshape(nbs, N_HEADS, HEAD_DIM)
    n_pool = cache_kv_cmp.shape[0]
    page_rows = PAGE_SIZE * KV_ROWS

    ow_p, oc_s, sel_s, ow_s, win_new = attn_prompt_with_step(
        qrot_p, kvw_pb, None, None, nbp, t, tq, "win",
        [cmp_step(q3_s, cache_kv_cmp.reshape(n_pool, page_rows, HEAD_DIM), pt_flat, n_pages, past_len),
         win_step(qrot3_s, kvw_s.reshape(nbs, 1, KV_W), cache_kv_win.reshape(nbs, wbuf * KV_ROWS, HEAD_DIM),
                  pt_flat)])
    os_p, os_s = attn_prompt_with_step(
        qrot_p, kvs_pb, sel_p, expand_t, nbp, t, tq, "sel",
        [sel_step(qrot3_s, sel_s, kvs_s.reshape(nbs, 1, KV_W),
                  cache_kv_sel.reshape(n_pool, page_rows, HEAD_DIM), pt_flat, n_pages)])
    y_p = nsa_out(oc_p, os_p, ow_p, gates1_p, bias1, proj1_p, wo1, xp1, final_g, NSA_OUT_TM)
    y_s = nsa_out(oc_s.reshape(nbs, NSA_W), os_s.reshape(nbs, NSA_W), ow_s.reshape(nbs, NSA_W),
                  gates1_s, bias1, proj1_s, wo1, xs1, final_g, nbs)

    kv5 = lambda a, rows: a.reshape(-1, rows, 2, N_KV, HEAD_DIM)
    return (y_p.reshape(nbp, t, d), y_s.reshape(nbs, 1, d),
            pool_p, pool_s,
            c_p, c_s, n_p, n_s, m_p[:, 0, :M_HEADS], m_s[:, :M_HEADS],
            kv5(kvc_p, t), kv5(proj1_s[:, 2 * NSA_W:2 * NSA_W + KV_W], 1),
            kv5(kvs_p, t), kv5(kvs_s, 1),
            kv5(kvw_p, t)[:, t - wbuf:],
            kv5(win_new, wbuf))
```
